```python
import functools
import jax
import jax.numpy as jnp
from jax import lax
import numpy as np

D_MODEL = 1024
BATCH = 2
SEQ = 8192
DEPTH = 4
DEC_BATCH = 128
DEC_SEQ = 4
PAST_LEN = 2048
PAGE_SIZE = 128

W_POOL = D_MODEL // 4
POOL_WINDOWS = (2, 4, 8, 16)
N_POOL_GROUPS = len(POOL_WINDOWS)
POOL_GW = W_POOL // N_POOL_GROUPS
POOL_PREFIX = max(POOL_WINDOWS) - 1
W_CONV = D_MODEL // 4
CONV_K = 31
CONV_PREFIX = CONV_K - 1
HEAD_DIM = 64
N_HEADS = (D_MODEL // 2) // HEAD_DIM
N_KV = 2
GQA = N_HEADS // N_KV
ROPE_DIM = HEAD_DIM // 4
ROPE_THETA = 500000.0
L_CMP = 32
L_SLC = 64
CMP_PER_SLC = L_SLC // L_CMP
N_SEL = 16
WINDOW = 512
Q_BLOCK = 128
D_FF = ((8 * D_MODEL + 2) // 3 + 255) // 256 * 256
EPS = 1e-6
NEG = -1e30
FORCE_SCORE = 1e4
SPLIT_SIZES = (W_POOL, 2 * W_CONV, N_HEADS * HEAD_DIM) + (N_KV * HEAD_DIM,) * 6 + (3 * N_HEADS, 3 * D_MODEL)
IN_COLS = sum(SPLIT_SIZES)
SPLIT_POINTS = tuple(int(v) for v in np.cumsum(SPLIT_SIZES)[:-1])

kernel_name = 'hybrid_pool_conv_nsa_adaln_decoder_step'


def rmsnorm(x, g):
    xf = x.astype(jnp.float32)
    y = xf * lax.rsqrt(jnp.mean(xf * xf, axis=-1, keepdims=True) + EPS)
    return (y * g.astype(jnp.float32)).astype(x.dtype)


def layernorm(x, g, b):
    xf = x.astype(jnp.float32)
    mu = jnp.mean(xf, axis=-1, keepdims=True)
    var = jnp.mean(jnp.square(xf - mu), axis=-1, keepdims=True)
    y = (xf - mu) * lax.rsqrt(var + EPS) * g.astype(jnp.float32) + b.astype(jnp.float32)
    return y.astype(x.dtype)


def ada_mod(c, w, b):
    m = jax.nn.silu(c) @ w + b
    return jnp.split(m[:, None, :], 6, axis=-1)


def masked_softmax(s, mask):
    p = jax.nn.softmax(jnp.where(mask, s, NEG), axis=-1)
    return p * mask.astype(jnp.float32)


def rope_partial(x, pos):
    half = ROPE_DIM // 2
    inv = ROPE_THETA ** (-jnp.arange(half, dtype=jnp.float32) * 2.0 / ROPE_DIM)
    ang = pos.astype(jnp.float32)[:, None] * inv[None, :]
    cos = jnp.cos(ang)[None, :, None, :]
    sin = jnp.sin(ang)[None, :, None, :]
    xr = x[..., :ROPE_DIM].astype(jnp.float32)
    x1, x2 = xr[..., :half], xr[..., half:]
    rot = jnp.concatenate([x1 * cos - x2 * sin, x2 * cos + x1 * sin], axis=-1).astype(x.dtype)
    return jnp.concatenate([rot, x[..., ROPE_DIM:]], axis=-1)


def pool_mix(u, prefix, pos, w_pool, s_pool):
    B, T, _ = u.shape
    ext = jnp.concatenate([prefix.astype(u.dtype), u], axis=1)
    cs = jnp.cumsum(ext.astype(jnp.float32), axis=1)
    cs = jnp.concatenate([jnp.zeros((B, 1, W_POOL), jnp.float32), cs], axis=1)
    hi = cs[:, POOL_PREFIX + 1:POOL_PREFIX + 1 + T]
    means = []
    for gi, w in enumerate(POOL_WINDOWS):
        ch = slice(gi * POOL_GW, (gi + 1) * POOL_GW)
        lo = cs[:, POOL_PREFIX + 1 - w:POOL_PREFIX + 1 - w + T, ch]
        cnt = jnp.minimum(w, pos + 1).astype(jnp.float32)[None, :, None]
        means.append((hi[..., ch] - lo) / cnt)
    d = jnp.concatenate(means, axis=-1) - u.astype(jnp.float32)
    z = jnp.einsum('btgc,gcd->btgd', d.reshape(B, T, N_POOL_GROUPS, POOL_GW), w_pool.astype(jnp.float32))
    z = z.reshape(B, T, W_POOL) * s_pool.astype(jnp.float32)
    return z.astype(u.dtype), ext[:, -POOL_PREFIX:]


def conv_module(uv, prefix, w_dw, b_dw, g_ln, b_ln):
    a, gate = jnp.split(uv, 2, axis=-1)
    u = a * jax.nn.sigmoid(gate)
    ext = jnp.concatenate([prefix.astype(u.dtype), u], axis=1)
    y = lax.conv_general_dilated(ext, w_dw[:, None, :].astype(u.dtype), window_strides=(1,), padding='VALID',
                                 dimension_numbers=('NWC', 'WIO', 'NWC'), feature_group_count=W_CONV)
    y = jax.nn.silu(layernorm(y + b_dw, g_ln, b_ln))
    return y, ext[:, -CONV_PREFIX:]


def global_keys(kc, vc, ks, vs, w_ck, w_cv):
    B, T = kc.shape[:2]
    tp = -(-T // L_SLC) * L_SLC
    padw = ((0, 0), (0, tp - T), (0, 0), (0, 0))
    kc, vc, ks, vs = (jnp.pad(a, padw) for a in (kc, vc, ks, vs))
    nc, ns = tp // L_CMP, tp // L_SLC
    ck = jnp.einsum('bclkd,l->bckd', kc.reshape(B, nc, L_CMP, N_KV, HEAD_DIM), w_ck)
    cv = jnp.einsum('bclkd,l->bckd', vc.reshape(B, nc, L_CMP, N_KV, HEAD_DIM), w_cv)
    ks_t = ks.reshape(B, ns, L_SLC, N_KV, HEAD_DIM).transpose(0, 3, 1, 2, 4)
    vs_t = vs.reshape(B, ns, L_SLC, N_KV, HEAD_DIM).transpose(0, 3, 1, 2, 4)
    return ck, cv, ks_t, vs_t


def nsa_core(qg, q_pos, ck, cv, ks_t, vs_t, kw, vw, kw_pos, gates):
    B, Q = qg.shape[:2]
    nc, ns = ck.shape[1], ks_t.shape[2]
    scale = HEAD_DIM ** -0.5
    s_c = jnp.einsum('bqkgd,bckd->bqkgc', qg, ck, preferred_element_type=jnp.float32) * scale
    c_end = (jnp.arange(nc) + 1) * L_CMP - 1
    valid_c = (c_end[None, :] <= q_pos[:, None])[None, :, None, None, :]
    p_c = masked_softmax(s_c, valid_c)
    o_c = jnp.einsum('bqkgc,bckd->bqkgd', p_c, cv)
    imp = p_c.sum(axis=3).reshape(B, Q, N_KV, ns, CMP_PER_SLC).sum(-1)
    blk = jnp.arange(ns)[None, :]
    cur = (q_pos // L_SLC)[:, None]
    valid_s = (blk <= cur)[None, :, None, :]
    forced = ((blk == 0) | (blk == cur) | (blk == cur - 1))[None, :, None, :]
    score = jnp.where(valid_s, jnp.where(forced, FORCE_SCORE, imp), -1.0)
    n_sel = min(N_SEL, ns)
    top_s, idx = lax.top_k(score, n_sel)
    bi = jnp.arange(B)[:, None, None, None]
    ki = jnp.arange(N_KV)[None, None, :, None]
    kb = ks_t[bi, ki, idx]
    vb = vs_t[bi, ki, idx]
    kpos = idx[..., None] * L_SLC + jnp.arange(L_SLC)
    ok = (top_s >= 0)[..., None] & (kpos <= q_pos[None, :, None, None, None])
    s_s = jnp.einsum('bqkgd,bqknld->bqkgnl', qg, kb, preferred_element_type=jnp.float32) * scale
    p_s = masked_softmax(s_s.reshape(B, Q, N_KV, GQA, n_sel * L_SLC), ok.reshape(B, Q, N_KV, 1, n_sel * L_SLC))
    o_s = jnp.einsum('bqkgnl,bqknld->bqkgd', p_s.reshape(B, Q, N_KV, GQA, n_sel, L_SLC), vb)
    s_w = jnp.einsum('bqkgd,bjkd->bqkgj', qg, kw, preferred_element_type=jnp.float32) * scale
    okw = (kw_pos[None, :] <= q_pos[:, None]) & (kw_pos[None, :] > q_pos[:, None] - WINDOW) & (kw_pos[None, :] >= 0)
    p_w = masked_softmax(s_w, okw[None, :, None, None, :])
    o_w = jnp.einsum('bqkgj,bjkd->bqkgd', p_w, vw)
    g = jax.nn.sigmoid(gates.astype(jnp.float32))
    o = g[..., 0:1] * o_c + g[..., 1:2] * o_s + g[..., 2:3] * o_w
    return o.astype(qg.dtype)


def nsa_prompt(qg, gates, pos, kc, vc, ks, vs, kw, vw, w_ck, w_cv):
    B, T = qg.shape[:2]
    ck, cv, ks_t, vs_t = global_keys(kc, vc, ks, vs, w_ck, w_cv)
    nb = T // Q_BLOCK
    pad = jnp.zeros((B, WINDOW, N_KV, HEAD_DIM), kw.dtype)
    kw_pad = jnp.concatenate([pad, kw], axis=1)
    vw_pad = jnp.concatenate([pad, vw], axis=1)
    qb = jnp.moveaxis(qg.reshape(B, nb, Q_BLOCK, N_KV, GQA, HEAD_DIM), 1, 0)
    gb = jnp.moveaxis(gates.reshape(B, nb, Q_BLOCK, N_KV, GQA, 3), 1, 0)
    pb = pos.reshape(nb, Q_BLOCK)

    def block(args):
        q_i, g_i, q_pos = args
        start = q_pos[0]
        kwb = lax.dynamic_slice_in_dim(kw_pad, start, WINDOW + Q_BLOCK, axis=1)
        vwb = lax.dynamic_slice_in_dim(vw_pad, start, WINDOW + Q_BLOCK, axis=1)
        kw_pos = start - WINDOW + jnp.arange(WINDOW + Q_BLOCK)
        return nsa_core(q_i, q_pos, ck, cv, ks_t, vs_t, kwb, vwb, kw_pos, g_i)

    o = lax.map(block, (qb, gb, pb))
    o = jnp.moveaxis(o, 0, 1).reshape(B, T, N_HEADS * HEAD_DIM)
    wl = min(WINDOW, T)
    return o, (kc, vc, ks, vs, kw[:, T - wl:], vw[:, T - wl:])


def nsa_sample(past, qg, gates, pos, kc, vc, ks, vs, kw, vw, w_ck, w_cv):
    past_kc, past_vc, past_ks, past_vs, win_k, win_v = past
    B, T = qg.shape[:2]
    cat = lambda a, b: jnp.concatenate([a, b], axis=1)
    ck, cv, ks_t, vs_t = global_keys(cat(past_kc, kc), cat(past_vc, vc), cat(past_ks, ks), cat(past_vs, vs), w_ck, w_cv)
    kw_all = cat(win_k, kw)
    vw_all = cat(win_v, vw)
    wl = win_k.shape[1]
    kw_pos = pos[0] - wl + jnp.arange(wl + T)
    o = nsa_core(qg, pos, ck, cv, ks_t, vs_t, kw_all, vw_all, kw_pos, gates)
    o = o.reshape(B, T, N_HEADS * HEAD_DIM)
    return o, (kc, vc, ks, vs, kw_all[:, -wl:], vw_all[:, -wl:])


def token_mixing(h, pos, pool_prefix, conv_prefix, nsa_fn, w_in, w_pool, s_pool, w_pool_out, w_dw, b_dw, g_cln, b_cln,
                 w_conv_out, w_ck, w_cv, w_attn_out, w_out):
    B, T, _ = h.shape
    (u_pool, u_conv, q, kc, vc, ks, vs, kw, vw, g_nsa, g_mrg) = jnp.split(h @ w_in, SPLIT_POINTS, axis=-1)
    a_out, pool_state = pool_mix(u_pool, pool_prefix, pos, w_pool, s_pool)
    b_out, conv_state = conv_module(u_conv, conv_prefix, w_dw, b_dw, g_cln, b_cln)
    kvh = lambda t: t.reshape(B, T, N_KV, HEAD_DIM)
    qg = rope_partial(q.reshape(B, T, N_HEADS, HEAD_DIM), pos).reshape(B, T, N_KV, GQA, HEAD_DIM)
    kc, ks, kw = (rope_partial(kvh(t), pos) for t in (kc, ks, kw))
    c_out, nsa_state = nsa_fn(qg, g_nsa.reshape(B, T, N_KV, GQA, 3), pos, kc, kvh(vc), ks, kvh(vs), kw, kvh(vw), w_ck, w_cv)
    gm = jax.nn.sigmoid(g_mrg.astype(jnp.float32)).reshape(B, T, 3, D_MODEL)
    merged = (gm[:, :, 0] * (a_out @ w_pool_out) + gm[:, :, 1] * (b_out @ w_conv_out)
              + gm[:, :, 2] * (c_out @ w_attn_out))
    return merged.astype(h.dtype) @ w_out, nsa_state + (pool_state, conv_state)


def decoder_layer(x, c, pos, pool_prefix, conv_prefix, nsa_fn, lw):
    (w_ada, b_ada, g_mix, g_ffn, w_in, w_pool, s_pool, w_pool_out, w_dw, b_dw, g_cln, b_cln, w_conv_out,
     w_ck, w_cv, w_attn_out, w_out, w_gu, w_down) = lw
    sh1, sc1, gt1, sh2, sc2, gt2 = ada_mod(c, w_ada, b_ada)
    h = rmsnorm(x, g_mix) * (1 + sc1) + sh1
    mix, states = token_mixing(h, pos, pool_prefix, conv_prefix, nsa_fn, w_in, w_pool, s_pool, w_pool_out, w_dw, b_dw,
                               g_cln, b_cln, w_conv_out, w_ck, w_cv, w_attn_out, w_out)
    x = x + gt1 * mix
    h = rmsnorm(x, g_ffn) * (1 + sc2) + sh2
    g, u = jnp.split(h @ w_gu, 2, axis=-1)
    x = x + gt2 * ((jax.nn.silu(g) * u) @ w_down)
    return x, states


def gather_pages(pool, page_table):
    return pool[page_table].reshape(page_table.shape[0], -1, *pool.shape[2:])


def setup_inputs(seed: int = 0) -> dict:
    key = jax.random.key(seed)
    keys = jax.random.split(key, 40)

    def nrm(i, shape, scale):
        return jax.random.normal(keys[i], shape, jnp.float32) * scale

    n_pages = PAST_LEN // PAGE_SIZE
    n_used = DEC_BATCH * n_pages
    n_pool = n_used + -(-n_used // 4)
    page_table = jax.random.permutation(keys[10], n_pool)[:n_used].reshape(DEC_BATCH, n_pages).astype(jnp.int32)
    wbuf = min(WINDOW, PAST_LEN)
    kv_pool = (DEPTH, n_pool, PAGE_SIZE, N_KV, HEAD_DIM)
    win = (DEPTH, DEC_BATCH, wbuf, N_KV, HEAD_DIM)
    D = D_MODEL
    return {
        'x_prompt': nrm(0, (BATCH, SEQ, D), 1.0),
        'x_sample': nrm(1, (DEC_BATCH, DEC_SEQ, D), 1.0),
        'cache_cmp_k': nrm(2, kv_pool, 1.0),
        'cache_cmp_v': nrm(3, kv_pool, 1.0),
        'cache_slc_k': nrm(4, kv_pool, 1.0),
        'cache_slc_v': nrm(5, kv_pool, 1.0),
        'state_win_k': nrm(6, win, 1.0),
        'state_win_v': nrm(7, win, 1.0),
        'state_pool': nrm(8, (DEPTH, DEC_BATCH, POOL_PREFIX, W_POOL), 1.0),
        'state_conv': nrm(9, (DEPTH, DEC_BATCH, CONV_PREFIX, W_CONV), 0.5),
        'page_table': page_table,
        'c_prompt': nrm(11, (BATCH, D), 1.0),
        'c_sample': nrm(12, (DEC_BATCH, D), 1.0),
        'w_ada': nrm(13, (DEPTH, D, 6 * D), 0.5 * D ** -0.5),
        'b_ada': nrm(14, (DEPTH, 6 * D), 0.01),
        'g_norm_mix': 1.0 + nrm(15, (DEPTH, D), 0.01),
        'g_norm_ffn': 1.0 + nrm(16, (DEPTH, D), 0.01),
        'w_in': nrm(17, (DEPTH, D, IN_COLS), D ** -0.5),
        'w_pool': nrm(18, (DEPTH, N_POOL_GROUPS, POOL_GW, POOL_GW), POOL_GW ** -0.5),
        's_pool': 1.0 + nrm(19, (DEPTH, W_POOL), 0.1),
        'w_pool_out': nrm(20, (DEPTH, W_POOL, D), W_POOL ** -0.5),
        'w_dw': nrm(21, (DEPTH, CONV_K, W_CONV), CONV_K ** -0.5),
        'b_dw': nrm(22, (DEPTH, W_CONV), 0.01),
        'g_conv_ln': 1.0 + nrm(23, (DEPTH, W_CONV), 0.01),
        'b_conv_ln': nrm(24, (DEPTH, W_CONV), 0.01),
        'w_conv_out': nrm(25, (DEPTH, W_CONV, D), W_CONV ** -0.5),
        'w_cmp_k': (1.0 + nrm(26, (DEPTH, L_CMP), 0.1)) / L_CMP,
        'w_cmp_v': (1.0 + nrm(27, (DEPTH, L_CMP), 0.1)) / L_CMP,
        'w_attn_out': nrm(28, (DEPTH, N_HEADS * HEAD_DIM, D), (N_HEADS * HEAD_DIM) ** -0.5),
        'w_out': nrm(29, (DEPTH, D, D), D ** -0.5),
        'w_gu': nrm(30, (DEPTH, D, 2 * D_FF), D ** -0.5),
        'w_down': nrm(31, (DEPTH, D_FF, D), D_FF ** -0.5),
        'g_final': 1.0 + nrm(32, (D,), 0.01),
    }


def reference(x_prompt, x_sample, cache_cmp_k, cache_cmp_v, cache_slc_k, cache_slc_v, state_win_k, state_win_v,
              state_pool, state_conv, page_table, c_prompt, c_sample, w_ada, b_ada, g_norm_mix, g_norm_ffn, w_in,
              w_pool, s_pool, w_pool_out, w_dw, b_dw, g_conv_ln, b_conv_ln, w_conv_out, w_cmp_k, w_cmp_v,
              w_attn_out, w_out, w_gu, w_down, g_final):
    xp, xs = x_prompt, x_sample
    bp, tp = xp.shape[:2]
    ts = xs.shape[1]
    past_len = page_table.shape[1] * PAGE_SIZE
    pos_p = jnp.arange(tp)
    pos_s = past_len + jnp.arange(ts)
    pool0 = jnp.zeros((bp, POOL_PREFIX, W_POOL), xp.dtype)
    conv0 = jnp.zeros((bp, CONV_PREFIX, W_CONV), xp.dtype)
    st_p, st_s = [], []
    for l in range(DEPTH):
        lw = (w_ada[l], b_ada[l], g_norm_mix[l], g_norm_ffn[l], w_in[l], w_pool[l], s_pool[l], w_pool_out[l],
              w_dw[l], b_dw[l], g_conv_ln[l], b_conv_ln[l], w_conv_out[l], w_cmp_k[l], w_cmp_v[l],
              w_attn_out[l], w_out[l], w_gu[l], w_down[l])
        xp, sp = decoder_layer(xp, c_prompt, pos_p, pool0, conv0, nsa_prompt, lw)
        past = (gather_pages(cache_cmp_k[l], page_table), gather_pages(cache_cmp_v[l], page_table),
                gather_pages(cache_slc_k[l], page_table), gather_pages(cache_slc_v[l], page_table),
                state_win_k[l], state_win_v[l])
        xs, ss = decoder_layer(xs, c_sample, pos_s, state_pool[l], state_conv[l],
                               functools.partial(nsa_sample, past), lw)
        st_p.append(sp)
        st_s.append(ss)
    y_prompt = rmsnorm(xp, g_final)
    y_sample = rmsnorm(xs, g_final)

    def stacked(states, i):
        return jnp.stack([s[i] for s in states])

    new_cmp_k_prompt, new_cmp_k_sample = stacked(st_p, 0), stacked(st_s, 0)
    new_cmp_v_prompt, new_cmp_v_sample = stacked(st_p, 1), stacked(st_s, 1)
    new_slc_k_prompt, new_slc_k_sample = stacked(st_p, 2), stacked(st_s, 2)
    new_slc_v_prompt, new_slc_v_sample = stacked(st_p, 3), stacked(st_s, 3)
    new_win_k_prompt, new_win_k_sample = stacked(st_p, 4), stacked(st_s, 4)
    new_win_v_prompt, new_win_v_sample = stacked(st_p, 5), stacked(st_s, 5)
    new_pool_prompt, new_pool_sample = stacked(st_p, 6), stacked(st_s, 6)
    new_conv_prompt, new_conv_sample = stacked(st_p, 7), stacked(st_s, 7)
    return (y_prompt, y_sample, new_cmp_k_prompt, new_cmp_k_sample, new_cmp_v_prompt, new_cmp_v_sample,
            new_slc_k_prompt, new_slc_k_sample, new_slc_v_prompt, new_slc_v_sample, new_win_k_prompt,
            new_win_k_sample, new_win_v_prompt, new_win_v_sample, new_pool_prompt, new_pool_sample,
            new_conv_prompt, new_conv_sample)
```

```python
import functools

import jax
import jax.numpy as jnp
from jax import lax
from jax.experimental import pallas as pl
from jax.experimental.pallas import tpu as pltpu

F32 = jnp.float32
BF16 = jnp.bfloat16

HEAD_DIM = 64
N_KV = 2
GQA = 4
N_HEADS = N_KV * GQA
ROPE_DIM = 16
ROPE_THETA = 500000.0
L_CMP = 32
L_SLC = 64
L_SLC_SHIFT = 6
N_SEL = 16
WINDOW = 512
Q_BLOCK = 128
PAGE_SIZE = 128
POOL_WINDOWS = (2, 4, 8, 16)
CONV_K = 31
EPS = 1e-6
NEG = -1e30
FORCE_SCORE = 1e4

LANES = 128
SUBLANES = 8
VMEM_LIMIT_BYTES = 56 * 1024 * 1024

HALO = 32
KV_W = N_KV * HEAD_DIM


def _params(*sem):
    return pltpu.CompilerParams(dimension_semantics=sem, vmem_limit_bytes=VMEM_LIMIT_BYTES)


def _modnorm(x, g, sc, sh):
    ms = jnp.mean(x * x, axis=-1, keepdims=True)
    return (x * lax.rsqrt(ms + EPS) * g) * (1.0 + sc) + sh


def _sigmoid(x):
    return 1.0 / (1.0 + jnp.exp(-x))


def _dot_nt(a, b):
    return lax.dot_general(a, b, (((1,), (1,)), ((), ())), preferred_element_type=F32)


def _masked_softmax(s, mask):
    sm = jnp.where(mask, s, NEG)
    e = jnp.exp(sm - jnp.max(sm, axis=-1, keepdims=True))
    return jnp.where(mask, e / jnp.sum(e, axis=-1, keepdims=True), 0.0)


def _ada_kernel(c_ref, w_ref, b_ref, o_ref):
    c = c_ref[...]
    a = (c * _sigmoid(c)).astype(BF16)
    o_ref[...] = jnp.dot(a, w_ref[...], preferred_element_type=F32) + b_ref[...]


def _ada_all(c_all, w_ada, b_ada):
    depth, d, n = w_ada.shape
    tn = 1536
    assert n % tn == 0
    nb = c_all.shape[0]
    return pl.pallas_call(
        _ada_kernel,
        grid=(depth, n // tn),
        in_specs=[
            pl.BlockSpec((nb, d), lambda l, j: (0, 0)),
            pl.BlockSpec((None, d, tn), lambda l, j: (l, 0, j)),
            pl.BlockSpec((None, 1, tn), lambda l, j: (l, 0, j)),
        ],
        out_specs=pl.BlockSpec((None, nb, tn), lambda l, j: (l, 0, j)),
        out_shape=jax.ShapeDtypeStruct((depth, nb, n), F32),
        compiler_params=_params("parallel", "parallel"),
        name="ada_mod",
    )(c_all, w_ada, b_ada.reshape(depth, 1, n))


def _inproj_kernel(x_ref, sc_ref, sh_ref, g_ref, w_ref, cos_ref, sa_ref, sb_ref, wck_ref, wcv_ref, *outs,
                   w_pool, w_conv, with_cmp):
    (up_ref, uc_ref, q_ref, kc_ref, vc_ref, ks_ref, vs_ref, kw_ref, vw_ref, gn_ref) = outs[:10]
    h = _modnorm(x_ref[0], g_ref[...], sc_ref[0], sh_ref[0]).astype(BF16)
    y = jnp.dot(h, w_ref[...], preferred_element_type=F32)
    cos, sa, sb = cos_ref[...], sa_ref[...], sb_ref[...]

    def rope(t):
        return t * cos + pltpu.roll(t, LANES - ROPE_DIM // 2, 1) * sa + pltpu.roll(t, ROPE_DIM // 2, 1) * sb

    o = 0
    up_ref[0] = y[:, o:o + w_pool]
    o += w_pool
    uc_ref[0] = y[:, o:o + 2 * w_conv]
    o += 2 * w_conv
    for j in range(N_HEADS * HEAD_DIM // LANES):
        q_ref[0, :, j * LANES:(j + 1) * LANES] = rope(y[:, o:o + LANES]).astype(q_ref.dtype)
        o += LANES
    kc = rope(y[:, o:o + KV_W])
    vc = y[:, o + KV_W:o + 2 * KV_W]
    ks = rope(y[:, o + 2 * KV_W:o + 3 * KV_W])
    vs = y[:, o + 3 * KV_W:o + 4 * KV_W]
    kw = rope(y[:, o + 4 * KV_W:o + 5 * KV_W])
    vw = y[:, o + 5 * KV_W:o + 6 * KV_W]
    o += 6 * KV_W
    kc_ref[0], vc_ref[0], ks_ref[0], vs_ref[0], kw_ref[0], vw_ref[0] = kc, vc, ks, vs, kw, vw
    gn_ref[0] = y[:, o:o + LANES]
    if with_cmp:
        ck_ref, cv_ref = outs[10:]
        tm = kc.shape[0]
        ck_ref[0] = jnp.sum(kc.reshape(tm // L_CMP, L_CMP, KV_W) * wck_ref[...][None], axis=1)
        cv_ref[0] = jnp.sum(vc.reshape(tm // L_CMP, L_CMP, KV_W) * wcv_ref[...][None], axis=1)


def _inproj(x, sc, sh, g, w_a, l, tabs, wck, wcv, *, w_pool, w_conv, with_cmp, q_dtype, tm):
    b, t, d = x.shape
    n = w_a.shape[-1]
    tm = min(tm, t)
    assert t % tm == 0 and (tm % L_CMP == 0 or not with_cmp)
    tmod = sc.shape[1]
    mod_spec = (pl.BlockSpec((1, 1, d), lambda bi, i: (bi, 0, 0)) if tmod == 1
                else pl.BlockSpec((1, tm, d), lambda bi, i: (bi, i, 0)))
    tab_spec = pl.BlockSpec((tm, LANES), lambda bi, i: (i, 0))
    row = lambda w: pl.BlockSpec((1, tm, w), lambda bi, i: (bi, i, 0))
    widths = [w_pool, 2 * w_conv, N_HEADS * HEAD_DIM] + [KV_W] * 6 + [LANES]
    dtypes = [F32, F32, q_dtype] + [F32] * 7
    out_specs = [row(w) for w in widths]
    out_shape = [jax.ShapeDtypeStruct((b, t, w), dt) for w, dt in zip(widths, dtypes)]
    if with_cmp:
        out_specs += [pl.BlockSpec((1, tm // L_CMP, KV_W), lambda bi, i: (bi, i, 0))] * 2
        out_shape += [jax.ShapeDtypeStruct((b, t // L_CMP, KV_W), F32)] * 2
    return pl.pallas_call(
        functools.partial(_inproj_kernel, w_pool=w_pool, w_conv=w_conv, with_cmp=with_cmp),
        grid=(b, t // tm),
        in_specs=[
            row(d), mod_spec, mod_spec,
            pl.BlockSpec((None, 1, d), lambda bi, i: (l, 0, 0)),
            pl.BlockSpec((None, d, n), lambda bi, i: (l, 0, 0)),
            tab_spec, tab_spec, tab_spec,
            pl.BlockSpec((None, L_CMP, KV_W), lambda bi, i: (l, 0, 0)),
            pl.BlockSpec((None, L_CMP, KV_W), lambda bi, i: (l, 0, 0)),
        ],
        out_specs=out_specs,
        out_shape=out_shape,
        compiler_params=_params("parallel", "parallel"),
        name="in_proj",
    )(x, sc, sh, g, w_a, *tabs, wck, wcv)


def _pool_means(ext_ref, base, rows, pos0):
    lane = lax.broadcasted_iota(jnp.int32, (rows, LANES), 1)
    pos = lax.broadcasted_iota(jnp.int32, (rows, LANES), 0) + pos0
    low = lane < POOL_WINDOWS_GW
    means = []
    for tile in range(len(POOL_WINDOWS) // 2):
        cols = slice(tile * LANES, (tile + 1) * LANES)
        w_small, w_big = POOL_WINDOWS[2 * tile], POOL_WINDOWS[2 * tile + 1]
        acc = ext_ref[base:base + rows, cols]
        small = acc
        for s in range(1, w_big):
            acc = acc + ext_ref[base - s:base - s + rows, cols]
            if s == w_small - 1:
                small = acc
        cnt = jnp.minimum(jnp.where(low, w_small, w_big), pos + 1).astype(F32)
        means.append(jnp.where(low, small, acc) / cnt)
    return jnp.concatenate(means, axis=1)


POOL_WINDOWS_GW = 64


def _layernorm_silu(y, g, b):
    mu = jnp.mean(y, axis=-1, keepdims=True)
    yc = y - mu
    var = jnp.mean(yc * yc, axis=-1, keepdims=True)
    z = yc * lax.rsqrt(var + EPS) * g + b
    return z * _sigmoid(z)


def _mix_prompt_kernel(up_ref, uph_ref, uc_ref, uch_ref, wpool_ref, spool_ref, wdw_ref, bdw_ref, gln_ref, bln_ref,
                       a_ref, b_ref, cst_ref, pext_ref, cext_ref, *, tt, w_conv):
    i = pl.program_id(1)
    first = i == 0
    u = up_ref[0]
    pext_ref[0:HALO] = jnp.where(first, 0.0, uph_ref[0])
    pext_ref[HALO:HALO + tt] = u
    d = _pool_means(pext_ref, HALO, tt, i * tt) - u
    z = jnp.dot(d.astype(BF16), wpool_ref[...], preferred_element_type=F32) * spool_ref[...]
    a_ref[0] = z.astype(a_ref.dtype)
    hc = uch_ref[0]
    cext_ref[0:HALO] = jnp.where(first, 0.0, hc[:, :w_conv] * _sigmoid(hc[:, w_conv:]))
    uc = uc_ref[0]
    cext_ref[HALO:HALO + tt] = uc[:, :w_conv] * _sigmoid(uc[:, w_conv:])
    off = HALO - (CONV_K - 1)
    acc = wdw_ref[0:1, :] * cext_ref[off:off + tt]
    for k in range(1, CONV_K):
        acc = acc + wdw_ref[k:k + 1, :] * cext_ref[off + k:off + k + tt]
    b_ref[0] = _layernorm_silu(acc + bdw_ref[...], gln_ref[...], bln_ref[...]).astype(b_ref.dtype)
    cst_ref[0] = cext_ref[tt:tt + HALO]


def _mix_prompt(up, uc, wpool_bd, spool, wdw, bdw, gln, bln, l, *, tt):
    b, t, w_pool = up.shape
    w_conv = uc.shape[-1] // 2
    tt = min(tt, t)
    assert t % tt == 0 and tt % HALO == 0
    r = tt // HALO
    halo = lambda w: pl.BlockSpec((1, HALO, w), lambda bi, i: (bi, jnp.maximum(i * r - 1, 0), 0))
    vec = lambda w: pl.BlockSpec((None, 1, w), lambda bi, i: (l, 0, 0))
    return pl.pallas_call(
        functools.partial(_mix_prompt_kernel, tt=tt, w_conv=w_conv),
        grid=(b, t // tt),
        in_specs=[
            pl.BlockSpec((1, tt, w_pool), lambda bi, i: (bi, i, 0)), halo(w_pool),
            pl.BlockSpec((1, tt, 2 * w_conv), lambda bi, i: (bi, i, 0)), halo(2 * w_conv),
            pl.BlockSpec((None, w_pool, w_pool), lambda bi, i: (l, 0, 0)), vec(w_pool),
            pl.BlockSpec((None, CONV_K, w_conv), lambda bi, i: (l, 0, 0)), vec(w_conv), vec(w_conv), vec(w_conv),
        ],
        out_specs=[
            pl.BlockSpec((1, tt, w_pool), lambda bi, i: (bi, i, 0)),
            pl.BlockSpec((1, tt, w_conv), lambda bi, i: (bi, i, 0)),
            pl.BlockSpec((1, HALO, w_conv), lambda bi, i: (bi, 0, 0)),
        ],
        out_shape=[
            jax.ShapeDtypeStruct((b, t, w_pool), BF16),
            jax.ShapeDtypeStruct((b, t, w_conv), BF16),
            jax.ShapeDtypeStruct((b, HALO, w_conv), F32),
        ],
        scratch_shapes=[pltpu.VMEM((HALO + tt, w_pool), F32), pltpu.VMEM((HALO + tt, w_conv), F32)],
        compiler_params=_params("parallel", "arbitrary"),
        name="mix_prompt",
    )(up, up, uc, uc, wpool_bd, spool, wdw, bdw, gln, bln)


def _mix_sample_kernel(pst_ref, up_ref, cst_ref, uc_ref, wpool_ref, spool_ref, wdw_ref, bdw_ref, gln_ref, bln_ref,
                       a_ref, b_ref, glu_ref, *, w_conv):
    n_p, n_c, ts = pst_ref.shape[0], cst_ref.shape[0], up_ref.shape[0]
    prow = lambda r: pst_ref[r] if r < n_p else up_ref[r - n_p]
    low = lax.broadcasted_iota(jnp.int32, (up_ref.shape[1], LANES), 1) < POOL_WINDOWS_GW
    glu = []
    for t in range(ts):
        uc = uc_ref[t]
        glu.append(uc[:, :w_conv] * _sigmoid(uc[:, w_conv:]))
        glu_ref[t] = glu[t]
    crow = lambda r: cst_ref[r] if r < n_c else glu[r - n_c]
    for t in range(ts):
        u = up_ref[t]
        means = None
        acc = u
        sums = {1: acc}
        for s in range(1, max(POOL_WINDOWS)):
            acc = acc + prow(n_p + t - s)
            sums[s + 1] = acc
        tiles = []
        for tile in range(len(POOL_WINDOWS) // 2):
            cols = slice(tile * LANES, (tile + 1) * LANES)
            w_small, w_big = POOL_WINDOWS[2 * tile], POOL_WINDOWS[2 * tile + 1]
            tiles.append(jnp.where(low, sums[w_small][:, cols] / float(w_small),
                                   sums[w_big][:, cols] / float(w_big)))
        means = jnp.concatenate(tiles, axis=1)
        z = jnp.dot((means - u).astype(BF16), wpool_ref[...], preferred_element_type=F32) * spool_ref[...]
        a_ref[t] = z.astype(a_ref.dtype)
        acc = wdw_ref[0:1, :] * crow(t + n_c - (CONV_K - 1))
        for k in range(1, CONV_K):
            acc = acc + wdw_ref[k:k + 1, :] * crow(t + n_c - (CONV_K - 1) + k)
        b_ref[t] = _layernorm_silu(acc + bdw_ref[...], gln_ref[...], bln_ref[...]).astype(b_ref.dtype)


def _mix_sample(pst_t, up_t, cst_t, uc_t, wpool_bd, spool, wdw, bdw, gln, bln, l):
    ts, bd, w_pool = up_t.shape
    w_conv = uc_t.shape[-1] // 2
    full = lambda a: pl.BlockSpec(a.shape, lambda i: (0,) * a.ndim)
    vec = lambda w: pl.BlockSpec((None, 1, w), lambda i: (l, 0, 0))
    return pl.pallas_call(
        functools.partial(_mix_sample_kernel, w_conv=w_conv),
        grid=(1,),
        in_specs=[
            full(pst_t), full(up_t), full(cst_t), full(uc_t),
            pl.BlockSpec((None, w_pool, w_pool), lambda i: (l, 0, 0)), vec(w_pool),
            pl.BlockSpec((None, CONV_K, w_conv), lambda i: (l, 0, 0)), vec(w_conv), vec(w_conv), vec(w_conv),
        ],
        out_specs=[pl.BlockSpec((ts, bd, w_pool), lambda i: (0, 0, 0)),
                   pl.BlockSpec((ts, bd, w_conv), lambda i: (0, 0, 0)),
                   pl.BlockSpec((ts, bd, w_conv), lambda i: (0, 0, 0))],
        out_shape=[jax.ShapeDtypeStruct((ts, bd, w_pool), BF16),
                   jax.ShapeDtypeStruct((ts, bd, w_conv), BF16),
                   jax.ShapeDtypeStruct((ts, bd, w_conv), F32)],
        compiler_params=_params("arbitrary"),
        name="mix_sample",
    )(pst_t, up_t, cst_t, uc_t, wpool_bd, spool, wdw, bdw, gln, bln)


def _topk_mask(score, n_sel):
    st = score.T
    cand = lax.broadcasted_iota(jnp.int32, st.shape, 0).astype(F32)

    def body(_, carry):
        st, sel = carry
        m = jnp.max(st, axis=0, keepdims=True)
        first = jnp.min(jnp.where(st == m, cand, float(LANES)), axis=0, keepdims=True)
        hit = cand == first
        return jnp.where(hit, -jnp.inf, st), jnp.where(hit, 1.0, sel)

    _, sel = lax.fori_loop(0, n_sel, body, (st, jnp.zeros(st.shape, F32)))
    return sel.T


def _head_rows(tile, src_half, dst_half, lane):
    if src_half != dst_half:
        tile = pltpu.roll(tile, HEAD_DIM, 1)
    return jnp.where((lane >= dst_half * HEAD_DIM) & (lane < (dst_half + 1) * HEAD_DIM), tile, 0.0)


def _nsa_prompt_kernel(q_ref, gn_ref, ck_ref, cv_ref, ks_ref, vs_ref, kw_ref, vw_ref, eneg_ref, o_ref, *, ck_keys):
    i = pl.program_id(1)
    qb = Q_BLOCK
    rows = GQA * qb
    scale = HEAD_DIM ** -0.5
    n_cmp = ck_ref.shape[1]
    n_slc = n_cmp // 2
    lane = lax.broadcasted_iota(jnp.int32, (qb, LANES), 1)

    def qpos(shape):
        return i * qb + (lax.broadcasted_iota(jnp.int32, shape, 0) & (qb - 1))

    sig = _sigmoid(gn_ref[0])
    n_wchunks = WINDOW // qb + 1
    out_tiles = [None] * (N_HEADS // 2)
    for k in range(N_KV):
        qh = []
        for g in range(GQA):
            h = k * GQA + g
            tile = q_ref[0, :, (h // 2) * LANES:(h // 2 + 1) * LANES].astype(F32) * scale
            qh.append(_head_rows(tile, h % 2, k, lane))
        qk = jnp.concatenate(qh, axis=0).astype(BF16)

        s_c = _dot_nt(qk, ck_ref[0])
        cl = lax.broadcasted_iota(jnp.int32, (rows, n_cmp), 1)
        c_end = jnp.where(cl < n_slc, cl * L_SLC + L_CMP - 1, (cl - n_slc) * L_SLC + L_SLC - 1)
        p_c = _masked_softmax(s_c, c_end <= qpos(s_c.shape))
        o_c = jnp.dot(p_c.astype(BF16), cv_ref[0], preferred_element_type=F32)
        ps = p_c[0:qb]
        for g in range(1, GQA):
            ps = ps + p_c[g * qb:(g + 1) * qb]
        imp = ps[:, :n_slc] + ps[:, n_slc:]
        if n_slc < LANES:
            imp = jnp.concatenate([imp, jnp.zeros((qb, LANES - n_slc), F32)], axis=1)

        cur = qpos(lane.shape) >> L_SLC_SHIFT
        valid_s = lane <= cur
        forced = (lane == 0) | (lane == cur) | (lane == cur - 1)
        score = jnp.where(valid_s, jnp.where(forced, FORCE_SCORE, imp), -1.0)
        score = jnp.where(lane < n_slc, score, -jnp.inf)
        sel = jnp.where(valid_s, _topk_mask(score, min(N_SEL, n_slc)), 0.0)
        notsel = (1.0 - sel).astype(BF16)
        q_aug = jnp.concatenate([qk, jnp.concatenate([notsel] * GQA, axis=0)], axis=1)

        def chunk(c, carry, causal):
            m, l, acc = carry
            start = pl.multiple_of(c * ck_keys, ck_keys)
            k_aug = jnp.concatenate([ks_ref[0, pl.ds(start, ck_keys), :], eneg_ref[pl.ds(start, ck_keys), :]], axis=1)
            s = _dot_nt(q_aug, k_aug)
            if causal:
                kpos = start + lax.broadcasted_iota(jnp.int32, (rows, ck_keys), 1)
                s = jnp.where(kpos <= qpos(s.shape), s, NEG)
            m_new = jnp.maximum(m, jnp.max(s, axis=-1, keepdims=True))
            alpha = jnp.exp(m - m_new)
            p = jnp.exp(s - m_new)
            l = alpha * l + jnp.sum(p, axis=-1, keepdims=True)
            acc = alpha * acc + jnp.dot(p.astype(BF16), vs_ref[0, pl.ds(start, ck_keys), :],
                                        preferred_element_type=F32)
            return m_new, l, acc

        n_chunks = (i * qb + qb + ck_keys - 1) // ck_keys
        init = (jnp.full((rows, 1), -jnp.inf, F32), jnp.zeros((rows, 1), F32), jnp.zeros((rows, LANES), F32))
        carry = lax.fori_loop(0, n_chunks - 1, functools.partial(chunk, causal=False), init)
        _, l_s, acc_s = chunk(n_chunks - 1, carry, True)
        o_s = acc_s / l_s

        kws, vws = [], []
        for dj in range(n_wchunks):
            st = pl.multiple_of(jnp.maximum(i - (n_wchunks - 1) + dj, 0) * qb, qb)
            kws.append(kw_ref[0, pl.ds(st, qb), :])
            vws.append(vw_ref[0, pl.ds(st, qb), :])
        s_w = _dot_nt(qk, jnp.concatenate(kws, axis=0))
        kwpos = (i - (n_wchunks - 1)) * qb + lax.broadcasted_iota(jnp.int32, s_w.shape, 1)
        qp = qpos(s_w.shape)
        okw = (kwpos <= qp) & (kwpos > qp - WINDOW) & (kwpos >= 0)
        p_w = _masked_softmax(s_w, okw)
        o_w = jnp.dot(p_w.astype(BF16), jnp.concatenate(vws, axis=0), preferred_element_type=F32)

        for g in range(GQA):
            h = k * GQA + g
            col = 3 * h
            r = slice(g * qb, (g + 1) * qb)
            o = sig[:, col:col + 1] * o_c[r] + sig[:, col + 1:col + 2] * o_s[r] + sig[:, col + 2:col + 3] * o_w[r]
            o = _head_rows(o, k, h % 2, lane)
            out_tiles[h // 2] = o if out_tiles[h // 2] is None else out_tiles[h // 2] + o
    for j, tile in enumerate(out_tiles):
        o_ref[0, :, j * LANES:(j + 1) * LANES] = tile.astype(o_ref.dtype)


def _nsa_prompt(q, gn, ckp, cvp, ks, vs, kw, vw, eneg):
    b, t, _ = q.shape
    assert t % Q_BLOCK == 0 and t // L_SLC <= LANES
    ck_keys = min(512, t)
    assert t % ck_keys == 0
    qspec = lambda w: pl.BlockSpec((1, Q_BLOCK, w), lambda bi, i: (bi, i, 0))
    seq = lambda n: pl.BlockSpec((1, n, KV_W), lambda bi, i: (bi, 0, 0))
    return pl.pallas_call(
        functools.partial(_nsa_prompt_kernel, ck_keys=ck_keys),
        grid=(b, t // Q_BLOCK),
        in_specs=[qspec(N_HEADS * HEAD_DIM), qspec(LANES), seq(t // L_CMP), seq(t // L_CMP),
                  seq(t), seq(t), seq(t), seq(t), pl.BlockSpec((t, LANES), lambda bi, i: (0, 0))],
        out_specs=qspec(N_HEADS * HEAD_DIM),
        out_shape=jax.ShapeDtypeStruct((b, t, N_HEADS * HEAD_DIM), BF16),
        compiler_params=_params("parallel", "parallel"),
        name="nsa_prompt",
    )(q, gn, ckp, cvp, ks, vs, kw, vw, eneg)


def _nsa_sample_kernel(pt_ref, q_ref, gn_ref, kcn_ref, vcn_ref, ksn_ref, vsn_ref, kwn_ref, vwn_ref, wk_ref, wv_ref,
                       wck_ref, wcv_ref, eneg_ref, *rest, n_pages, past_len, ts):
    pages = rest[:4 * n_pages]
    o_ref, nwk_ref, nwv_ref = rest[4 * n_pages:4 * n_pages + 3]
    (ck_s, cv_s, ks_s, vs_s, kw_s, vw_s, kwb_s, vwb_s, tail_s, q_s, sc_s, g_s, o_s) = rest[4 * n_pages + 3:]
    kc_pages, vc_pages = pages[:n_pages], pages[n_pages:2 * n_pages]
    ks_pages, vs_pages = pages[2 * n_pages:3 * n_pages], pages[3 * n_pages:]
    scale = HEAD_DIM ** -0.5
    wl = wk_ref.shape[1]
    n_keys = ks_s.shape[0]
    per_page = PAGE_SIZE // L_CMP
    n_cmp_past = past_len // L_CMP
    n_slc = -(-(past_len + ts) // L_SLC)
    rows = GQA * N_KV * ts
    lane_t = lax.broadcasted_iota(jnp.int32, (ts, LANES), 1)

    ck_s[...] = jnp.zeros(ck_s.shape, F32)
    cv_s[...] = jnp.zeros(cv_s.shape, F32)
    for p in range(n_pages):
        kc = kc_pages[p][0].reshape(per_page, L_CMP, KV_W)
        vc = vc_pages[p][0].reshape(per_page, L_CMP, KV_W)
        ck_s[p * per_page:(p + 1) * per_page] = jnp.sum(kc * wck_ref[...][None], axis=1)
        cv_s[p * per_page:(p + 1) * per_page] = jnp.sum(vc * wcv_ref[...][None], axis=1)
    ck_s[n_cmp_past:n_cmp_past + 1] = jnp.sum(kcn_ref[0] * wck_ref[0:ts], axis=0, keepdims=True)
    cv_s[n_cmp_past:n_cmp_past + 1] = jnp.sum(vcn_ref[0] * wcv_ref[0:ts], axis=0, keepdims=True)

    for p in range(n_pages):
        ks_s[p * PAGE_SIZE:(p + 1) * PAGE_SIZE] = ks_pages[p][0].astype(BF16)
        vs_s[p * PAGE_SIZE:(p + 1) * PAGE_SIZE] = vs_pages[p][0].astype(BF16)
    for new_ref, dst in ((ksn_ref, ks_s), (vsn_ref, vs_s)):
        tail_s[...] = jnp.zeros(tail_s.shape, F32)
        tail_s[0:ts] = new_ref[0]
        dst[past_len:n_keys] = tail_s[...].astype(BF16)

    for w_ref, new_ref, all_s, bf_s, out_ref in ((wk_ref, kwn_ref, kw_s, kwb_s, nwk_ref),
                                                 (wv_ref, vwn_ref, vw_s, vwb_s, nwv_ref)):
        all_s[0:wl] = w_ref[0]
        all_s[wl:wl + LANES] = jnp.zeros((LANES, KV_W), F32)
        all_s[wl:wl + ts] = new_ref[0]
        out_ref[0] = all_s[ts:ts + wl]
        bf_s[...] = all_s[...].astype(BF16)

    for g in range(GQA):
        for k in range(N_KV):
            h = k * GQA + g
            tile = q_ref[0, :, (h // 2) * LANES:(h // 2 + 1) * LANES] * scale
            q_s[(g * N_KV + k) * ts:(g * N_KV + k + 1) * ts] = _head_rows(tile, h % 2, k, lane_t)
    qa = q_s[...].astype(BF16)

    def qpos(shape):
        return past_len + (lax.broadcasted_iota(jnp.int32, shape, 0) & (ts - 1))

    s_c = _dot_nt(qa, ck_s[...].astype(BF16))
    cl = lax.broadcasted_iota(jnp.int32, s_c.shape, 1)
    p_c = _masked_softmax(s_c, (cl + 1) * L_CMP - 1 <= qpos(s_c.shape))
    o_c = jnp.dot(p_c.astype(BF16), cv_s[...].astype(BF16), preferred_element_type=F32)
    grp = N_KV * ts
    ps = p_c[0:grp]
    for g in range(1, GQA):
        ps = ps + p_c[g * grp:(g + 1) * grp]
    imp = ps + pltpu.roll(ps, LANES - 1, 1)

    lane_g = lax.broadcasted_iota(jnp.int32, (grp, LANES), 1)
    blk = lane_g >> 1
    is_blk = ((lane_g & 1) == 0) & (blk < n_slc)
    cur = qpos(lane_g.shape) >> L_SLC_SHIFT
    valid_s = blk <= cur
    forced = (blk == 0) | (blk == cur) | (blk == cur - 1)
    score = jnp.where(valid_s, jnp.where(forced, FORCE_SCORE, imp), -1.0)
    sc_s[...] = jnp.zeros(sc_s.shape, F32)
    sc_s[0:grp] = jnp.where(is_blk, score, -jnp.inf)
    sel = _topk_mask(sc_s[...], min(N_SEL, n_slc))[0:grp]
    sel = jnp.where(is_blk & valid_s, sel, 0.0)
    notsel = (1.0 - sel).astype(BF16)
    q_aug = jnp.concatenate([qa, jnp.concatenate([notsel] * GQA, axis=0)], axis=1)

    n_parts = 4
    part = n_keys // n_parts
    s_parts = []
    for c in range(n_parts):
        k_aug = jnp.concatenate([ks_s[c * part:(c + 1) * part], eneg_ref[c * part:(c + 1) * part]], axis=1)
        s_parts.append(_dot_nt(q_aug, k_aug))
    s_s = jnp.concatenate(s_parts, axis=1)
    kpos = lax.broadcasted_iota(jnp.int32, s_s.shape, 1)
    s_s = jnp.where(kpos <= qpos(s_s.shape), s_s, NEG)
    e = jnp.exp(s_s - jnp.max(s_s, axis=-1, keepdims=True))
    p_s = (e / jnp.sum(e, axis=-1, keepdims=True)).astype(BF16)
    o_sel = jnp.dot(p_s[:, 0:part], vs_s[0:part], preferred_element_type=F32)
    for c in range(1, n_parts):
        o_sel = o_sel + jnp.dot(p_s[:, c * part:(c + 1) * part], vs_s[c * part:(c + 1) * part],
                                preferred_element_type=F32)

    s_w = _dot_nt(qa, kwb_s[...])
    kwpos = past_len - wl + lax.broadcasted_iota(jnp.int32, s_w.shape, 1)
    qp = qpos(s_w.shape)
    okw = (kwpos <= qp) & (kwpos > qp - WINDOW) & (kwpos >= 0)
    p_w = _masked_softmax(s_w, okw)
    o_w = jnp.dot(p_w.astype(BF16), vwb_s[...], preferred_element_type=F32)

    sig = _sigmoid(gn_ref[0])
    o = None
    for j, ob in enumerate((o_c, o_sel, o_w)):
        for g in range(GQA):
            for k in range(N_KV):
                col = 3 * (k * GQA + g) + j
                g_s[(g * N_KV + k) * ts:(g * N_KV + k + 1) * ts] = jnp.broadcast_to(sig[:, col:col + 1], (ts, LANES))
        o = g_s[...] * ob if o is None else o + g_s[...] * ob
    o_s[...] = o
    for j in range(N_HEADS // 2):
        tile = None
        for h in (2 * j, 2 * j + 1):
            k, g = h // GQA, h % GQA
            piece = _head_rows(o_s[(g * N_KV + k) * ts:(g * N_KV + k + 1) * ts], k, h % 2, lane_t)
            tile = piece if tile is None else tile + piece
        o_ref[0, :, j * LANES:(j + 1) * LANES] = tile


def _nsa_sample(page_table, q, gn, new_rows, win_k, win_v, caches, wck, wcv, eneg, l, *, n_pool):
    bd, ts, _ = q.shape
    n_pages = page_table.shape[1]
    past_len = n_pages * PAGE_SIZE
    wl = win_k.shape[1]
    n_keys = -(-(past_len + ts) // (4 * LANES)) * 4 * LANES
    assert -(-(past_len + ts) // L_SLC) * 2 <= LANES and eneg.shape == (n_keys, LANES)
    rows = GQA * N_KV * ts
    tok = lambda w: pl.BlockSpec((1, ts, w), lambda b, pt: (b, 0, 0))
    win = pl.BlockSpec((1, wl, KV_W), lambda b, pt: (l * bd + b, 0, 0))
    wrow = pl.BlockSpec((None, L_CMP, KV_W), lambda b, pt: (l, 0, 0))

    def page(p):
        return pl.BlockSpec((1, PAGE_SIZE, KV_W), lambda b, pt: (l * n_pool + pt[b * n_pages + p], 0, 0))

    in_specs = ([tok(N_HEADS * HEAD_DIM), tok(LANES)] + [tok(KV_W)] * 6 + [win, win, wrow, wrow,
                pl.BlockSpec((n_keys, LANES), lambda b, pt: (0, 0))] + [page(p) for _ in range(4) for p in range(n_pages)])
    grid_spec = pltpu.PrefetchScalarGridSpec(
        num_scalar_prefetch=1,
        grid=(bd,),
        in_specs=in_specs,
        out_specs=[tok(N_HEADS * HEAD_DIM), pl.BlockSpec((1, wl, KV_W), lambda b, pt: (b, 0, 0)),
                   pl.BlockSpec((1, wl, KV_W), lambda b, pt: (b, 0, 0))],
        scratch_shapes=[
            pltpu.VMEM((LANES, KV_W), F32), pltpu.VMEM((LANES, KV_W), F32),
            pltpu.VMEM((n_keys, KV_W), BF16), pltpu.VMEM((n_keys, KV_W), BF16),
            pltpu.VMEM((wl + LANES, KV_W), F32), pltpu.VMEM((wl + LANES, KV_W), F32),
            pltpu.VMEM((wl + LANES, KV_W), BF16), pltpu.VMEM((wl + LANES, KV_W), BF16),
            pltpu.VMEM((n_keys - past_len, KV_W), F32),
            pltpu.VMEM((rows, LANES), F32), pltpu.VMEM((LANES, LANES), F32),
            pltpu.VMEM((rows, LANES), F32), pltpu.VMEM((rows, LANES), F32),
        ],
    )
    cache_args = [c for c in caches for _ in range(n_pages)]
    return pl.pallas_call(
        functools.partial(_nsa_sample_kernel, n_pages=n_pages, past_len=past_len, ts=ts),
        grid_spec=grid_spec,
        out_shape=[jax.ShapeDtypeStruct((bd, ts, N_HEADS * HEAD_DIM), F32),
                   jax.ShapeDtypeStruct((bd, wl, KV_W), F32), jax.ShapeDtypeStruct((bd, wl, KV_W), F32)],
        compiler_params=_params("arbitrary"),
        name="nsa_sample",
    )(page_table.reshape(-1), q, gn, *new_rows, win_k, win_v, wck, wcv, eneg, *cache_args)


def _merge_kernel(x_ref, sc_ref, sh_ref, gt_ref, g_ref, a_ref, b_ref, c_ref, wgm_ref, wp_ref, wc_ref, wa_ref,
                  wo_ref, o_ref):
    x = x_ref[0]
    d = x.shape[-1]
    h = _modnorm(x, g_ref[...], sc_ref[0], sh_ref[0]).astype(BF16)
    branches = (
        jnp.dot(a_ref[0].astype(BF16), wp_ref[...], preferred_element_type=F32),
        jnp.dot(b_ref[0].astype(BF16), wc_ref[...], preferred_element_type=F32),
        jnp.dot(c_ref[0].astype(BF16), wa_ref[...], preferred_element_type=F32),
    )
    merged = None
    for j, br in enumerate(branches):
        gm = _sigmoid(jnp.dot(h, wgm_ref[:, j * d:(j + 1) * d], preferred_element_type=F32))
        merged = gm * br if merged is None else merged + gm * br
    out = jnp.dot(merged.astype(BF16), wo_ref[...], preferred_element_type=F32)
    o_ref[0] = x + gt_ref[0] * out


def _mod_spec(tmod, tm, d):
    if tmod == 1:
        return pl.BlockSpec((1, 1, d), lambda bi, i: (bi, 0, 0))
    return pl.BlockSpec((1, tm, d), lambda bi, i: (bi, i, 0))


def _merge(x, sc, sh, gt, g, a, bo, c, wgm, wp, wc, wa, wo, l, *, tm):
    b, t, d = x.shape
    tm = min(tm, t)
    assert t % tm == 0
    ms = _mod_spec(sc.shape[1], tm, d)
    row = lambda w: pl.BlockSpec((1, tm, w), lambda bi, i: (bi, i, 0))
    wsp = lambda w: pl.BlockSpec((None,) + w.shape[1:], lambda bi, i: (l, 0, 0))
    return pl.pallas_call(
        _merge_kernel,
        grid=(b, t // tm),
        in_specs=[row(d), ms, ms, ms, pl.BlockSpec((None, 1, d), lambda bi, i: (l, 0, 0)),
                  row(a.shape[-1]), row(bo.shape[-1]), row(c.shape[-1]),
                  wsp(wgm), wsp(wp), wsp(wc), wsp(wa), wsp(wo)],
        out_specs=row(d),
        out_shape=jax.ShapeDtypeStruct((b, t, d), F32),
        compiler_params=_params("parallel", "parallel"),
        name="merge_out",
    )(x, sc, sh, gt, g, a, bo, c, wgm, wp, wc, wa, wo)


def _ffn_kernel(x_ref, sc_ref, sh_ref, gt_ref, g_ref, wgu_ref, wd_ref, gf_ref, *outs, d_ff, n_parts, final):
    x = x_ref[0]
    h = _modnorm(x, g_ref[...], sc_ref[0], sh_ref[0]).astype(BF16)
    part = d_ff // n_parts
    acc = None
    for c in range(n_parts):
        gp = jnp.dot(h, wgu_ref[:, c * part:(c + 1) * part], preferred_element_type=F32)
        up = jnp.dot(h, wgu_ref[:, d_ff + c * part:d_ff + (c + 1) * part], preferred_element_type=F32)
        act = (gp * _sigmoid(gp) * up).astype(BF16)
        dn = jnp.dot(act, wd_ref[c * part:(c + 1) * part, :], preferred_element_type=F32)
        acc = dn if acc is None else acc + dn
    y = x + gt_ref[0] * acc
    outs[0][0] = y
    if final:
        ms = jnp.mean(y * y, axis=-1, keepdims=True)
        outs[1][0] = y * lax.rsqrt(ms + EPS) * gf_ref[...]


def _ffn(x, sc, sh, gt, g, wgu, wd, gf, l, *, tm, final):
    b, t, d = x.shape
    d_ff = wd.shape[1]
    tm = min(tm, t)
    n_parts = 2
    assert t % tm == 0 and d_ff % (n_parts * LANES) == 0
    ms = _mod_spec(sc.shape[1], tm, d)
    row = pl.BlockSpec((1, tm, d), lambda bi, i: (bi, i, 0))
    n_out = 2 if final else 1
    return pl.pallas_call(
        functools.partial(_ffn_kernel, d_ff=d_ff, n_parts=n_parts, final=final),
        grid=(b, t // tm),
        in_specs=[row, ms, ms, ms, pl.BlockSpec((None, 1, d), lambda bi, i: (l, 0, 0)),
                  pl.BlockSpec((None, d, 2 * d_ff), lambda bi, i: (l, 0, 0)),
                  pl.BlockSpec((None, d_ff, d), lambda bi, i: (l, 0, 0)),
                  pl.BlockSpec((1, d), lambda bi, i: (0, 0))],
        out_specs=[row] * n_out,
        out_shape=[jax.ShapeDtypeStruct((b, t, d), F32)] * n_out,
        compiler_params=_params("parallel", "parallel"),
        name="ffn",
    )(x, sc, sh, gt, g, wgu, wd, gf)


def _rope_tables(pos):
    half = ROPE_DIM // 2
    inv = ROPE_THETA ** (-jnp.arange(half, dtype=F32) * 2.0 / ROPE_DIM)
    ang = pos.astype(F32)[:, None] * inv[None, :]
    cos, sin = jnp.cos(ang), jnp.sin(ang)
    n = pos.shape[0]
    rest = HEAD_DIM - ROPE_DIM
    c = jnp.concatenate([cos, cos, jnp.ones((n, rest), F32)], axis=1)
    sa = jnp.concatenate([-sin, jnp.zeros((n, half + rest), F32)], axis=1)
    sb = jnp.concatenate([jnp.zeros((n, half), F32), sin, jnp.zeros((n, rest), F32)], axis=1)
    rep = LANES // HEAD_DIM
    return tuple(jnp.tile(a, (1, rep)) for a in (c, sa, sb))


def _block_bias(n_keys, lane_stride):
    key_blk = (jnp.arange(n_keys) // L_SLC)[:, None] * lane_stride
    return jnp.where(key_blk == jnp.arange(LANES)[None, :], NEG, 0.0).astype(BF16)


def kernel(x_prompt, x_sample, cache_cmp_k, cache_cmp_v, cache_slc_k, cache_slc_v, state_win_k, state_win_v,
           state_pool, state_conv, page_table, c_prompt, c_sample, w_ada, b_ada, g_norm_mix, g_norm_ffn, w_in,
           w_pool, s_pool, w_pool_out, w_dw, b_dw, g_conv_ln, b_conv_ln, w_conv_out, w_cmp_k, w_cmp_v,
           w_attn_out, w_out, w_gu, w_down, g_final):
    depth, d, _ = w_in.shape
    bp, tp, _ = x_prompt.shape
    bd, ts, _ = x_sample.shape
    wp_dim = w_pool_out.shape[1]
    wc_dim = w_conv_out.shape[1]
    n_pool = cache_cmp_k.shape[1]
    n_pages = page_table.shape[1]
    past_len = n_pages * PAGE_SIZE
    wl = state_win_k.shape[2]
    n_groups, gw = w_pool.shape[1], w_pool.shape[2]
    assert gw == POOL_WINDOWS_GW and n_groups == len(POOL_WINDOWS) and wp_dim == n_groups * gw

    n_a = wp_dim + 2 * wc_dim + N_HEADS * HEAD_DIM + 6 * KV_W + 3 * N_HEADS
    n_a_pad = n_a - 3 * N_HEADS + LANES
    w_in_a = jnp.pad(w_in[:, :, :n_a], ((0, 0), (0, 0), (0, n_a_pad - n_a))).astype(BF16)
    w_gm = w_in[:, :, n_a:].astype(BF16)
    w_ada_b = w_ada.astype(BF16)
    eye = jnp.eye(n_groups, dtype=F32)
    wpool_bd = (w_pool[:, :, :, None, :] * eye[None, :, None, :, None]).reshape(depth, wp_dim, wp_dim).astype(BF16)
    wpo, wco, wao, wo = (w.astype(BF16) for w in (w_pool_out, w_conv_out, w_attn_out, w_out))
    wgu, wdn = w_gu.astype(BF16), w_down.astype(BF16)
    wck = jnp.broadcast_to(w_cmp_k[:, :, None], (depth, L_CMP, KV_W))
    wcv = jnp.broadcast_to(w_cmp_v[:, :, None], (depth, L_CMP, KV_W))
    vec3 = lambda a: a.reshape(depth, 1, -1)
    g_mix, g_ffn = vec3(g_norm_mix), vec3(g_norm_ffn)
    spool, bdw, gln, bln = vec3(s_pool), vec3(b_dw), vec3(g_conv_ln), vec3(b_conv_ln)
    gf = g_final.reshape(1, d)

    tabs_p = _rope_tables(jnp.arange(tp))
    tabs_s = _rope_tables(jnp.tile(past_len + jnp.arange(ts), bd))
    eneg_p = _block_bias(tp, 1)
    n_keys_s = -(-(past_len + ts) // (4 * LANES)) * 4 * LANES
    eneg_s = _block_bias(n_keys_s, 2)

    ada = _ada_all(jnp.concatenate([c_prompt, c_sample], axis=0), w_ada_b, b_ada)
    caches = tuple(c.reshape(depth * n_pool, PAGE_SIZE, KV_W)
                   for c in (cache_cmp_k, cache_cmp_v, cache_slc_k, cache_slc_v))
    win_k_all = state_win_k.reshape(depth * bd, wl, KV_W)
    win_v_all = state_win_v.reshape(depth * bd, wl, KV_W)

    xp = x_prompt
    xs = x_sample.reshape(1, bd * ts, d)
    st_p, st_s = [], []
    yp = ys = None
    for l in range(depth):
        final = l == depth - 1
        mp = [m[:, None, :] for m in jnp.split(ada[l, :bp], 6, axis=-1)]
        ms = [jnp.repeat(m, ts, axis=0)[None] for m in jnp.split(ada[l, bp:], 6, axis=-1)]

        (up, uc, q, kc, vc, ks, vs, kw, vw, gn, ck, cv) = _inproj(
            xp, mp[1], mp[0], g_mix, w_in_a, l, tabs_p, wck, wcv, w_pool=wp_dim, w_conv=wc_dim, with_cmp=True,
            q_dtype=BF16, tm=512)
        a_o, b_o, cst = _mix_prompt(up, uc, wpool_bd, spool, w_dw, bdw, gln, bln, l, tt=512)
        n_cmp = tp // L_CMP
        perm = lambda a: a.reshape(bp, n_cmp // 2, 2, KV_W).transpose(0, 2, 1, 3).reshape(bp, n_cmp, KV_W).astype(BF16)
        c_o = _nsa_prompt(q, gn, perm(ck), perm(cv), ks.astype(BF16), vs.astype(BF16), kw.astype(BF16),
                          vw.astype(BF16), eneg_p)
        xp = _merge(xp, mp[1], mp[0], mp[2], g_mix, a_o, b_o, c_o, w_gm, wpo, wco, wao, wo, l, tm=512)
        res = _ffn(xp, mp[4], mp[3], mp[5], g_ffn, wgu, wdn, gf, l, tm=512, final=final)
        xp = res[0]
        if final:
            yp = res[1]
        kv4 = lambda a: a.reshape(a.shape[0], a.shape[1], N_KV, HEAD_DIM)
        wlp = min(WINDOW, tp)
        st_p.append((kv4(kc), kv4(vc), kv4(ks), kv4(vs), kv4(kw[:, tp - wlp:]), kv4(vw[:, tp - wlp:]),
                     up[:, tp - (max(POOL_WINDOWS) - 1):], cst[:, HALO - (CONV_K - 1):]))

        (up, uc, q, kc, vc, ks, vs, kw, vw, gn) = _inproj(
            xs, ms[1], ms[0], g_mix, w_in_a, l, tabs_s, wck, wcv, w_pool=wp_dim, w_conv=wc_dim, with_cmp=False,
            q_dtype=F32, tm=512)
        tmaj = lambda a: a.reshape(bd, ts, a.shape[-1]).transpose(1, 0, 2)
        a_t, b_t, glu_t = _mix_sample(state_pool[l].transpose(1, 0, 2), tmaj(up), state_conv[l].transpose(1, 0, 2),
                                      tmaj(uc), wpool_bd, spool, w_dw, bdw, gln, bln, l)
        bmaj = lambda a: a.transpose(1, 0, 2).reshape(1, bd * ts, a.shape[-1])
        per_seq = lambda a: a.reshape(bd, ts, a.shape[-1])
        c_o, nwk, nwv = _nsa_sample(page_table, per_seq(q), per_seq(gn),
                                    [per_seq(a) for a in (kc, vc, ks, vs, kw, vw)], win_k_all, win_v_all, caches,
                                    wck, wcv, eneg_s, l, n_pool=n_pool)
        xs = _merge(xs, ms[1], ms[0], ms[2], g_mix, bmaj(a_t), bmaj(b_t), c_o.reshape(1, bd * ts, -1), w_gm, wpo,
                    wco, wao, wo, l, tm=512)
        res = _ffn(xs, ms[4], ms[3], ms[5], g_ffn, wgu, wdn, gf, l, tm=512, final=final)
        xs = res[0]
        if final:
            ys = res[1]
        kv4s = lambda a: a.reshape(bd, -1, N_KV, HEAD_DIM)
        new_pool = jnp.concatenate([state_pool[l][:, ts:], per_seq(up)], axis=1)
        new_conv = jnp.concatenate([state_conv[l][:, ts:], glu_t.transpose(1, 0, 2)], axis=1)
        st_s.append((kv4s(kc), kv4s(vc), kv4s(ks), kv4s(vs), kv4s(nwk), kv4s(nwv), new_pool, new_conv))

    outs = [yp, ys.reshape(bd, ts, d)]
    for i in range(8):
        outs.append(jnp.stack([s[i] for s in st_p]))
        outs.append(jnp.stack([s[i] for s in st_s]))
    return tuple(outs)
```

```python
import functools

import jax
import jax.numpy as jnp
from jax import lax
from jax.experimental import pallas as pl
from jax.experimental.pallas import tpu as pltpu

F32 = jnp.float32
BF16 = jnp.bfloat16

HEAD_DIM = 64
N_KV = 2
GQA = 4
N_HEADS = N_KV * GQA
ROPE_DIM = 16
ROPE_THETA = 500000.0
L_CMP = 32
L_SLC = 64
L_SLC_SHIFT = 6
N_SEL = 16
WINDOW = 512
Q_BLOCK = 128
PAGE_SIZE = 128
POOL_WINDOWS = (2, 4, 8, 16)
CONV_K = 31
EPS = 1e-6
NEG = -1e30
FORCE_SCORE = 1e4

LANES = 128
SUBLANES = 8
VMEM_LIMIT_BYTES = 56 * 1024 * 1024

NSA_CHUNK_KEYS = 512
HALO = 32
KV_W = N_KV * HEAD_DIM


def _params(*sem):
    return pltpu.CompilerParams(dimension_semantics=sem, vmem_limit_bytes=VMEM_LIMIT_BYTES)


def _modnorm(x, g, sc, sh):
    ms = jnp.mean(x * x, axis=-1, keepdims=True)
    return (x * lax.rsqrt(ms + EPS) * g) * (1.0 + sc) + sh


def _sigmoid(x):
    return 1.0 / (1.0 + jnp.exp(-x))


def _dot_nt(a, b):
    return lax.dot_general(a, b, (((1,), (1,)), ((), ())), preferred_element_type=F32)


def _masked_softmax(s, mask):
    sm = jnp.where(mask, s, NEG)
    e = jnp.exp(sm - jnp.max(sm, axis=-1, keepdims=True))
    return jnp.where(mask, e / jnp.sum(e, axis=-1, keepdims=True), 0.0)


def _ada_kernel(c_ref, w_ref, b_ref, o_ref):
    c = c_ref[...]
    a = (c * _sigmoid(c)).astype(BF16)
    o_ref[...] = jnp.dot(a, w_ref[...], preferred_element_type=F32) + b_ref[...]


def _ada_all(c_all, w_ada, b_ada):
    depth, d, n = w_ada.shape
    tn = 1536
    assert n % tn == 0
    nb = c_all.shape[0]
    return pl.pallas_call(
        _ada_kernel,
        grid=(depth, n // tn),
        in_specs=[
            pl.BlockSpec((nb, d), lambda l, j: (0, 0)),
            pl.BlockSpec((None, d, tn), lambda l, j: (l, 0, j)),
            pl.BlockSpec((None, 1, tn), lambda l, j: (l, 0, j)),
        ],
        out_specs=pl.BlockSpec((None, nb, tn), lambda l, j: (l, 0, j)),
        out_shape=jax.ShapeDtypeStruct((depth, nb, n), F32),
        compiler_params=_params("parallel", "parallel"),
        name="ada_mod",
    )(c_all, w_ada, b_ada.reshape(depth, 1, n))


def _inproj_kernel(x_ref, sc_ref, sh_ref, g_ref, w_ref, cos_ref, sa_ref, sb_ref, wck_ref, wcv_ref, *outs,
                   w_pool, w_conv, with_cmp):
    (up_ref, uc_ref, q_ref, kc_ref, vc_ref, ks_ref, vs_ref, kw_ref, vw_ref, gn_ref) = outs[:10]
    h = _modnorm(x_ref[0], g_ref[...], sc_ref[0], sh_ref[0]).astype(BF16)
    y = jnp.dot(h, w_ref[...], preferred_element_type=F32)
    cos, sa, sb = cos_ref[...], sa_ref[...], sb_ref[...]

    def rope(t):
        return t * cos + pltpu.roll(t, LANES - ROPE_DIM // 2, 1) * sa + pltpu.roll(t, ROPE_DIM // 2, 1) * sb

    o = 0
    up_ref[0] = y[:, o:o + w_pool]
    o += w_pool
    uc_ref[0] = y[:, o:o + 2 * w_conv]
    o += 2 * w_conv
    for j in range(N_HEADS * HEAD_DIM // LANES):
        q_ref[0, :, j * LANES:(j + 1) * LANES] = rope(y[:, o:o + LANES]).astype(q_ref.dtype)
        o += LANES
    kc = rope(y[:, o:o + KV_W])
    vc = y[:, o + KV_W:o + 2 * KV_W]
    ks = rope(y[:, o + 2 * KV_W:o + 3 * KV_W])
    vs = y[:, o + 3 * KV_W:o + 4 * KV_W]
    kw = rope(y[:, o + 4 * KV_W:o + 5 * KV_W])
    vw = y[:, o + 5 * KV_W:o + 6 * KV_W]
    o += 6 * KV_W
    kc_ref[0], vc_ref[0], ks_ref[0], vs_ref[0], kw_ref[0], vw_ref[0] = kc, vc, ks, vs, kw, vw
    gn_ref[0] = y[:, o:o + LANES]
    if with_cmp:
        ck_ref, cv_ref = outs[10:]
        tm = kc.shape[0]
        ck_ref[0] = jnp.sum(kc.reshape(tm // L_CMP, L_CMP, KV_W) * wck_ref[...][None], axis=1)
        cv_ref[0] = jnp.sum(vc.reshape(tm // L_CMP, L_CMP, KV_W) * wcv_ref[...][None], axis=1)


def _inproj(x, sc, sh, g, w_a, l, tabs, wck, wcv, *, w_pool, w_conv, with_cmp, q_dtype, tm):
    b, t, d = x.shape
    n = w_a.shape[-1]
    tm = min(tm, t)
    assert t % tm == 0 and (tm % L_CMP == 0 or not with_cmp)
    tmod = sc.shape[1]
    mod_spec = (pl.BlockSpec((1, 1, d), lambda bi, i: (bi, 0, 0)) if tmod == 1
                else pl.BlockSpec((1, tm, d), lambda bi, i: (bi, i, 0)))
    tab_spec = pl.BlockSpec((tm, LANES), lambda bi, i: (i, 0))
    row = lambda w: pl.BlockSpec((1, tm, w), lambda bi, i: (bi, i, 0))
    widths = [w_pool, 2 * w_conv, N_HEADS * HEAD_DIM] + [KV_W] * 6 + [LANES]
    dtypes = [F32, F32, q_dtype] + [F32] * 7
    out_specs = [row(w) for w in widths]
    out_shape = [jax.ShapeDtypeStruct((b, t, w), dt) for w, dt in zip(widths, dtypes)]
    if with_cmp:
        out_specs += [pl.BlockSpec((1, tm // L_CMP, KV_W), lambda bi, i: (bi, i, 0))] * 2
        out_shape += [jax.ShapeDtypeStruct((b, t // L_CMP, KV_W), F32)] * 2
    return pl.pallas_call(
        functools.partial(_inproj_kernel, w_pool=w_pool, w_conv=w_conv, with_cmp=with_cmp),
        grid=(b, t // tm),
        in_specs=[
            row(d), mod_spec, mod_spec,
            pl.BlockSpec((None, 1, d), lambda bi, i: (l, 0, 0)),
            pl.BlockSpec((None, d, n), lambda bi, i: (l, 0, 0)),
            tab_spec, tab_spec, tab_spec,
            pl.BlockSpec((None, L_CMP, KV_W), lambda bi, i: (l, 0, 0)),
            pl.BlockSpec((None, L_CMP, KV_W), lambda bi, i: (l, 0, 0)),
        ],
        out_specs=out_specs,
        out_shape=out_shape,
        compiler_params=_params("parallel", "parallel"),
        name="in_proj",
    )(x, sc, sh, g, w_a, *tabs, wck, wcv)


def _pool_means(ext_ref, base, rows, pos0):
    lane = lax.broadcasted_iota(jnp.int32, (rows, LANES), 1)
    pos = lax.broadcasted_iota(jnp.int32, (rows, LANES), 0) + pos0
    low = lane < POOL_WINDOWS_GW
    means = []
    for tile in range(len(POOL_WINDOWS) // 2):
        cols = slice(tile * LANES, (tile + 1) * LANES)
        w_small, w_big = POOL_WINDOWS[2 * tile], POOL_WINDOWS[2 * tile + 1]
        acc = ext_ref[base:base + rows, cols]
        small = acc
        for s in range(1, w_big):
            acc = acc + ext_ref[base - s:base - s + rows, cols]
            if s == w_small - 1:
                small = acc
        cnt = jnp.minimum(jnp.where(low, w_small, w_big), pos + 1).astype(F32)
        means.append(jnp.where(low, small, acc) / cnt)
    return jnp.concatenate(means, axis=1)


POOL_WINDOWS_GW = 64


def _layernorm_silu(y, g, b):
    mu = jnp.mean(y, axis=-1, keepdims=True)
    yc = y - mu
    var = jnp.mean(yc * yc, axis=-1, keepdims=True)
    z = yc * lax.rsqrt(var + EPS) * g + b
    return z * _sigmoid(z)


def _mix_prompt_kernel(up_ref, uph_ref, uc_ref, uch_ref, wpool_ref, spool_ref, wdw_ref, bdw_ref, gln_ref, bln_ref,
                       a_ref, b_ref, cst_ref, pext_ref, cext_ref, *, tt, w_conv):
    i = pl.program_id(1)
    first = i == 0
    u = up_ref[0]
    pext_ref[0:HALO] = jnp.where(first, 0.0, uph_ref[0])
    pext_ref[HALO:HALO + tt] = u
    d = _pool_means(pext_ref, HALO, tt, i * tt) - u
    z = jnp.dot(d.astype(BF16), wpool_ref[...], preferred_element_type=F32) * spool_ref[...]
    a_ref[0] = z.astype(a_ref.dtype)
    hc = uch_ref[0]
    cext_ref[0:HALO] = jnp.where(first, 0.0, hc[:, :w_conv] * _sigmoid(hc[:, w_conv:]))
    uc = uc_ref[0]
    cext_ref[HALO:HALO + tt] = uc[:, :w_conv] * _sigmoid(uc[:, w_conv:])
    off = HALO - (CONV_K - 1)
    acc = wdw_ref[0:1, :] * cext_ref[off:off + tt]
    for k in range(1, CONV_K):
        acc = acc + wdw_ref[k:k + 1, :] * cext_ref[off + k:off + k + tt]
    b_ref[0] = _layernorm_silu(acc + bdw_ref[...], gln_ref[...], bln_ref[...]).astype(b_ref.dtype)
    cst_ref[0] = cext_ref[tt:tt + HALO]


def _mix_prompt(up, uc, wpool_bd, spool, wdw, bdw, gln, bln, l, *, tt):
    b, t, w_pool = up.shape
    w_conv = uc.shape[-1] // 2
    tt = min(tt, t)
    assert t % tt == 0 and tt % HALO == 0
    r = tt // HALO
    halo = lambda w: pl.BlockSpec((1, HALO, w), lambda bi, i: (bi, jnp.maximum(i * r - 1, 0), 0))
    vec = lambda w: pl.BlockSpec((None, 1, w), lambda bi, i: (l, 0, 0))
    return pl.pallas_call(
        functools.partial(_mix_prompt_kernel, tt=tt, w_conv=w_conv),
        grid=(b, t // tt),
        in_specs=[
            pl.BlockSpec((1, tt, w_pool), lambda bi, i: (bi, i, 0)), halo(w_pool),
            pl.BlockSpec((1, tt, 2 * w_conv), lambda bi, i: (bi, i, 0)), halo(2 * w_conv),
            pl.BlockSpec((None, w_pool, w_pool), lambda bi, i: (l, 0, 0)), vec(w_pool),
            pl.BlockSpec((None, CONV_K, w_conv), lambda bi, i: (l, 0, 0)), vec(w_conv), vec(w_conv), vec(w_conv),
        ],
        out_specs=[
            pl.BlockSpec((1, tt, w_pool), lambda bi, i: (bi, i, 0)),
            pl.BlockSpec((1, tt, w_conv), lambda bi, i: (bi, i, 0)),
            pl.BlockSpec((1, HALO, w_conv), lambda bi, i: (bi, 0, 0)),
        ],
        out_shape=[
            jax.ShapeDtypeStruct((b, t, w_pool), BF16),
            jax.ShapeDtypeStruct((b, t, w_conv), BF16),
            jax.ShapeDtypeStruct((b, HALO, w_conv), F32),
        ],
        scratch_shapes=[pltpu.VMEM((HALO + tt, w_pool), F32), pltpu.VMEM((HALO + tt, w_conv), F32)],
        compiler_params=_params("parallel", "arbitrary"),
        name="mix_prompt",
    )(up, up, uc, uc, wpool_bd, spool, wdw, bdw, gln, bln)


def _mix_sample_kernel(pst_ref, up_ref, cst_ref, uc_ref, wpool_ref, spool_ref, wdw_ref, bdw_ref, gln_ref, bln_ref,
                       a_ref, b_ref, glu_ref, *, w_conv):
    n_p, n_c, ts = pst_ref.shape[0], cst_ref.shape[0], up_ref.shape[0]
    prow = lambda r: pst_ref[r] if r < n_p else up_ref[r - n_p]
    low = lax.broadcasted_iota(jnp.int32, (up_ref.shape[1], LANES), 1) < POOL_WINDOWS_GW
    glu = []
    for t in range(ts):
        uc = uc_ref[t]
        glu.append(uc[:, :w_conv] * _sigmoid(uc[:, w_conv:]))
        glu_ref[t] = glu[t]
    crow = lambda r: cst_ref[r] if r < n_c else glu[r - n_c]
    for t in range(ts):
        u = up_ref[t]
        means = None
        acc = u
        sums = {1: acc}
        for s in range(1, max(POOL_WINDOWS)):
            acc = acc + prow(n_p + t - s)
            sums[s + 1] = acc
        tiles = []
        for tile in range(len(POOL_WINDOWS) // 2):
            cols = slice(tile * LANES, (tile + 1) * LANES)
            w_small, w_big = POOL_WINDOWS[2 * tile], POOL_WINDOWS[2 * tile + 1]
            tiles.append(jnp.where(low, sums[w_small][:, cols] / float(w_small),
                                   sums[w_big][:, cols] / float(w_big)))
        means = jnp.concatenate(tiles, axis=1)
        z = jnp.dot((means - u).astype(BF16), wpool_ref[...], preferred_element_type=F32) * spool_ref[...]
        a_ref[t] = z.astype(a_ref.dtype)
        acc = wdw_ref[0:1, :] * crow(t + n_c - (CONV_K - 1))
        for k in range(1, CONV_K):
            acc = acc + wdw_ref[k:k + 1, :] * crow(t + n_c - (CONV_K - 1) + k)
        b_ref[t] = _layernorm_silu(acc + bdw_ref[...], gln_ref[...], bln_ref[...]).astype(b_ref.dtype)


def _mix_sample(pst_t, up_t, cst_t, uc_t, wpool_bd, spool, wdw, bdw, gln, bln, l):
    ts, bd, w_pool = up_t.shape
    w_conv = uc_t.shape[-1] // 2
    full = lambda a: pl.BlockSpec(a.shape, lambda i: (0,) * a.ndim)
    vec = lambda w: pl.BlockSpec((None, 1, w), lambda i: (l, 0, 0))
    return pl.pallas_call(
        functools.partial(_mix_sample_kernel, w_conv=w_conv),
        grid=(1,),
        in_specs=[
            full(pst_t), full(up_t), full(cst_t), full(uc_t),
            pl.BlockSpec((None, w_pool, w_pool), lambda i: (l, 0, 0)), vec(w_pool),
            pl.BlockSpec((None, CONV_K, w_conv), lambda i: (l, 0, 0)), vec(w_conv), vec(w_conv), vec(w_conv),
        ],
        out_specs=[pl.BlockSpec((ts, bd, w_pool), lambda i: (0, 0, 0)),
                   pl.BlockSpec((ts, bd, w_conv), lambda i: (0, 0, 0)),
                   pl.BlockSpec((ts, bd, w_conv), lambda i: (0, 0, 0))],
        out_shape=[jax.ShapeDtypeStruct((ts, bd, w_pool), BF16),
                   jax.ShapeDtypeStruct((ts, bd, w_conv), BF16),
                   jax.ShapeDtypeStruct((ts, bd, w_conv), F32)],
        compiler_params=_params("arbitrary"),
        name="mix_sample",
    )(pst_t, up_t, cst_t, uc_t, wpool_bd, spool, wdw, bdw, gln, bln)


def _topk_mask(score, n_sel):
    st = score.T
    cand = lax.broadcasted_iota(jnp.int32, st.shape, 0).astype(F32)

    def body(_, carry):
        st, sel = carry
        m = jnp.max(st, axis=0, keepdims=True)
        first = jnp.min(jnp.where(st == m, cand, float(LANES)), axis=0, keepdims=True)
        hit = cand == first
        return jnp.where(hit, -jnp.inf, st), jnp.where(hit, 1.0, sel)

    _, sel = lax.fori_loop(0, n_sel, body, (st, jnp.zeros(st.shape, F32)))
    return sel.T


def _head_rows(tile, src_half, dst_half, lane):
    if src_half != dst_half:
        tile = pltpu.roll(tile, HEAD_DIM, 1)
    return jnp.where((lane >= dst_half * HEAD_DIM) & (lane < (dst_half + 1) * HEAD_DIM), tile, 0.0)


def _nsa_prompt_kernel(q_ref, gn_ref, ck_ref, cv_ref, ks_ref, vs_ref, kw_ref, vw_ref, eneg_ref, o_ref,
                       qa_scr, s_scr, p_scr, *, ck_keys):
    i = pl.program_id(1)
    qb = Q_BLOCK
    rows = GQA * qb
    scale = HEAD_DIM ** -0.5
    n_cmp = ck_ref.shape[1]
    n_slc = n_cmp // 2
    lane = lax.broadcasted_iota(jnp.int32, (qb, LANES), 1)

    def qpos(shape):
        return i * qb + (lax.broadcasted_iota(jnp.int32, shape, 0) & (qb - 1))

    sig = _sigmoid(gn_ref[0])
    n_wchunks = WINDOW // qb + 1
    out_tiles = [None] * (N_HEADS // 2)
    qks, q_augs, o_cs = [], [], []
    for k in range(N_KV):
        qh = []
        for g in range(GQA):
            h = k * GQA + g
            tile = q_ref[0, :, (h // 2) * LANES:(h // 2 + 1) * LANES].astype(F32) * scale
            qh.append(_head_rows(tile, h % 2, k, lane))
        qk = jnp.concatenate(qh, axis=0).astype(BF16)

        s_c = _dot_nt(qk, ck_ref[0])
        cl = lax.broadcasted_iota(jnp.int32, (rows, n_cmp), 1)
        c_end = jnp.where(cl < n_slc, cl * L_SLC + L_CMP - 1, (cl - n_slc) * L_SLC + L_SLC - 1)
        p_c = _masked_softmax(s_c, c_end <= qpos(s_c.shape))
        o_c = jnp.dot(p_c.astype(BF16), cv_ref[0], preferred_element_type=F32)
        ps = p_c[0:qb]
        for g in range(1, GQA):
            ps = ps + p_c[g * qb:(g + 1) * qb]
        imp = ps[:, :n_slc] + ps[:, n_slc:]
        if n_slc < LANES:
            imp = jnp.concatenate([imp, jnp.zeros((qb, LANES - n_slc), F32)], axis=1)

        cur = qpos(lane.shape) >> L_SLC_SHIFT
        valid_s = lane <= cur
        forced = (lane == 0) | (lane == cur) | (lane == cur - 1)
        score = jnp.where(valid_s, jnp.where(forced, FORCE_SCORE, imp), -1.0)
        score = jnp.where(lane < n_slc, score, -jnp.inf)
        sel = jnp.where(valid_s, _topk_mask(score, min(N_SEL, n_slc)), 0.0)
        notsel = (1.0 - sel).astype(BF16)
        q_augs.append(jnp.concatenate([qk, jnp.concatenate([notsel] * GQA, axis=0)], axis=1))
        qks.append(qk)
        o_cs.append(o_c)

    for k in range(N_KV):
        qa_scr[k] = q_augs[k]
    own = lax.broadcasted_iota(jnp.int32, (ck_keys, LANES), 1) < HEAD_DIM

    def scores(c, buf):
        start = pl.multiple_of(c * ck_keys, ck_keys)
        k_aug = jnp.concatenate([ks_ref[0, pl.ds(start, ck_keys), :], eneg_ref[pl.ds(start, ck_keys), :]], axis=1)
        for k in range(N_KV):
            s_scr[buf, k] = _dot_nt(qa_scr[k], k_aug)

    def probs(c, buf, ms, causal):
        new_m, alphas = [], []
        for k in range(N_KV):
            s = s_scr[buf, k]
            if causal:
                kpos = c * ck_keys + lax.broadcasted_iota(jnp.int32, s.shape, 1)
                s = jnp.where(kpos <= qpos(s.shape), s, NEG)
            m_new = jnp.maximum(ms[k], jnp.max(s, axis=-1, keepdims=True))
            p_scr[buf, k] = jnp.exp(s - m_new).astype(BF16)
            alphas.append(jnp.exp(ms[k] - m_new))
            new_m.append(m_new)
        return new_m, alphas

    def values(c, buf, alphas, accs):
        start = pl.multiple_of(c * ck_keys, ck_keys)
        v = vs_ref[0, pl.ds(start, ck_keys), :]
        v_one = [jnp.where(own, v, 1.0), jnp.where(own, 1.0, v)]
        return [alphas[k] * accs[k] + jnp.dot(p_scr[buf, k], v_one[k], preferred_element_type=F32)
                for k in range(N_KV)]

    def pair(c, carry, causal, last):
        ms, alphas, accs = carry
        accs = values(jnp.maximum(c - 1, 0), 1, alphas, accs)
        ms, alphas = probs(c, 0, ms, causal)
        scores(c + 1, 1)
        accs = values(c, 0, alphas, accs)
        ms, alphas = probs(c + 1, 1, ms, causal)
        if last:
            accs = values(c + 1, 1, alphas, accs)
        else:
            scores(c + 2, 0)
        return ms, alphas, accs

    n_pairs = (i * qb + qb + 2 * ck_keys - 1) // (2 * ck_keys)
    p_scr[1] = jnp.zeros(p_scr.shape[1:], BF16)
    scores(0, 0)
    carry = ([jnp.full((rows, 1), -jnp.inf, F32)] * N_KV, [jnp.ones((rows, 1), F32)] * N_KV,
             [jnp.zeros((rows, LANES), F32)] * N_KV)
    carry = lax.fori_loop(0, n_pairs - 1, lambda j, cr: pair(2 * j, cr, False, False), carry)
    _, _, accs = pair(2 * (n_pairs - 1), carry, True, True)
    o_ss = [acc / pltpu.roll(acc, HEAD_DIM, 1) for acc in accs]

    for k in range(N_KV):
        qk, o_c, o_s = qks[k], o_cs[k], o_ss[k]
        kws, vws = [], []
        for dj in range(n_wchunks):
            st = pl.multiple_of(jnp.maximum(i - (n_wchunks - 1) + dj, 0) * qb, qb)
            kws.append(kw_ref[0, pl.ds(st, qb), :])
            vws.append(vw_ref[0, pl.ds(st, qb), :])
        s_w = _dot_nt(qk, jnp.concatenate(kws, axis=0))
        kwpos = (i - (n_wchunks - 1)) * qb + lax.broadcasted_iota(jnp.int32, s_w.shape, 1)
        qp = qpos(s_w.shape)
        okw = (kwpos <= qp) & (kwpos > qp - WINDOW) & (kwpos >= 0)
        p_w = _masked_softmax(s_w, okw)
        o_w = jnp.dot(p_w.astype(BF16), jnp.concatenate(vws, axis=0), preferred_element_type=F32)

        for g in range(GQA):
            h = k * GQA + g
            col = 3 * h
            r = slice(g * qb, (g + 1) * qb)
            o = sig[:, col:col + 1] * o_c[r] + sig[:, col + 1:col + 2] * o_s[r] + sig[:, col + 2:col + 3] * o_w[r]
            o = _head_rows(o, k, h % 2, lane)
            out_tiles[h // 2] = o if out_tiles[h // 2] is None else out_tiles[h // 2] + o
    for j, tile in enumerate(out_tiles):
        o_ref[0, :, j * LANES:(j + 1) * LANES] = tile.astype(o_ref.dtype)


def _nsa_prompt(q, gn, ckp, cvp, ks, vs, kw, vw, eneg):
    b, t, _ = q.shape
    assert t % Q_BLOCK == 0 and t // L_SLC <= LANES
    ck_keys = min(NSA_CHUNK_KEYS, t // 2)
    assert t % (2 * ck_keys) == 0
    rows = GQA * Q_BLOCK
    qspec = lambda w: pl.BlockSpec((1, Q_BLOCK, w), lambda bi, i: (bi, i, 0))
    seq = lambda n: pl.BlockSpec((1, n, KV_W), lambda bi, i: (bi, 0, 0))
    return pl.pallas_call(
        functools.partial(_nsa_prompt_kernel, ck_keys=ck_keys),
        scratch_shapes=[pltpu.VMEM((N_KV, rows, 2 * LANES), BF16), pltpu.VMEM((2, N_KV, rows, ck_keys), F32),
                        pltpu.VMEM((2, N_KV, rows, ck_keys), BF16)],
        grid=(b, t // Q_BLOCK),
        in_specs=[qspec(N_HEADS * HEAD_DIM), qspec(LANES), seq(t // L_CMP), seq(t // L_CMP),
                  seq(t), seq(t), seq(t), seq(t), pl.BlockSpec((t, LANES), lambda bi, i: (0, 0))],
        out_specs=qspec(N_HEADS * HEAD_DIM),
        out_shape=jax.ShapeDtypeStruct((b, t, N_HEADS * HEAD_DIM), BF16),
        compiler_params=_params("parallel", "parallel"),
        name="nsa_prompt",
    )(q, gn, ckp, cvp, ks, vs, kw, vw, eneg)


def _nsa_sample_kernel(pt_ref, q_ref, gn_ref, kcn_ref, vcn_ref, ksn_ref, vsn_ref, kwn_ref, vwn_ref, wk_ref, wv_ref,
                       wck_ref, wcv_ref, eneg_ref, *rest, n_pages, past_len, ts):
    pages = rest[:4 * n_pages]
    o_ref, nwk_ref, nwv_ref = rest[4 * n_pages:4 * n_pages + 3]
    (ck_s, cv_s, ks_s, vs_s, kw_s, vw_s, kwb_s, vwb_s, tail_s, q_s, sc_s, g_s, o_s) = rest[4 * n_pages + 3:]
    kc_pages, vc_pages = pages[:n_pages], pages[n_pages:2 * n_pages]
    ks_pages, vs_pages = pages[2 * n_pages:3 * n_pages], pages[3 * n_pages:]
    scale = HEAD_DIM ** -0.5
    wl = wk_ref.shape[1]
    n_keys = ks_s.shape[0]
    per_page = PAGE_SIZE // L_CMP
    n_cmp_past = past_len // L_CMP
    n_slc = -(-(past_len + ts) // L_SLC)
    rows = GQA * N_KV * ts
    lane_t = lax.broadcasted_iota(jnp.int32, (ts, LANES), 1)

    ck_s[...] = jnp.zeros(ck_s.shape, F32)
    cv_s[...] = jnp.zeros(cv_s.shape, F32)
    for p in range(n_pages):
        kc = kc_pages[p][0].reshape(per_page, L_CMP, KV_W)
        vc = vc_pages[p][0].reshape(per_page, L_CMP, KV_W)
        ck_s[p * per_page:(p + 1) * per_page] = jnp.sum(kc * wck_ref[...][None], axis=1)
        cv_s[p * per_page:(p + 1) * per_page] = jnp.sum(vc * wcv_ref[...][None], axis=1)
    ck_s[n_cmp_past:n_cmp_past + 1] = jnp.sum(kcn_ref[0] * wck_ref[0:ts], axis=0, keepdims=True)
    cv_s[n_cmp_past:n_cmp_past + 1] = jnp.sum(vcn_ref[0] * wcv_ref[0:ts], axis=0, keepdims=True)

    for p in range(n_pages):
        ks_s[p * PAGE_SIZE:(p + 1) * PAGE_SIZE] = ks_pages[p][0].astype(BF16)
        vs_s[p * PAGE_SIZE:(p + 1) * PAGE_SIZE] = vs_pages[p][0].astype(BF16)
    for new_ref, dst in ((ksn_ref, ks_s), (vsn_ref, vs_s)):
        tail_s[...] = jnp.zeros(tail_s.shape, F32)
        tail_s[0:ts] = new_ref[0]
        dst[past_len:n_keys] = tail_s[...].astype(BF16)

    for w_ref, new_ref, all_s, bf_s, out_ref in ((wk_ref, kwn_ref, kw_s, kwb_s, nwk_ref),
                                                 (wv_ref, vwn_ref, vw_s, vwb_s, nwv_ref)):
        all_s[0:wl] = w_ref[0]
        all_s[wl:wl + LANES] = jnp.zeros((LANES, KV_W), F32)
        all_s[wl:wl + ts] = new_ref[0]
        out_ref[0] = all_s[ts:ts + wl]
        bf_s[...] = all_s[...].astype(BF16)

    for g in range(GQA):
        for k in range(N_KV):
            h = k * GQA + g
            tile = q_ref[0, :, (h // 2) * LANES:(h // 2 + 1) * LANES] * scale
            q_s[(g * N_KV + k) * ts:(g * N_KV + k + 1) * ts] = _head_rows(tile, h % 2, k, lane_t)
    qa = q_s[...].astype(BF16)

    def qpos(shape):
        return past_len + (lax.broadcasted_iota(jnp.int32, shape, 0) & (ts - 1))

    s_c = _dot_nt(qa, ck_s[...].astype(BF16))
    cl = lax.broadcasted_iota(jnp.int32, s_c.shape, 1)
    p_c = _masked_softmax(s_c, (cl + 1) * L_CMP - 1 <= qpos(s_c.shape))
    o_c = jnp.dot(p_c.astype(BF16), cv_s[...].astype(BF16), preferred_element_type=F32)
    grp = N_KV * ts
    ps = p_c[0:grp]
    for g in range(1, GQA):
        ps = ps + p_c[g * grp:(g + 1) * grp]
    imp = ps + pltpu.roll(ps, LANES - 1, 1)

    lane_g = lax.broadcasted_iota(jnp.int32, (grp, LANES), 1)
    blk = lane_g >> 1
    is_blk = ((lane_g & 1) == 0) & (blk < n_slc)
    cur = qpos(lane_g.shape) >> L_SLC_SHIFT
    valid_s = blk <= cur
    forced = (blk == 0) | (blk == cur) | (blk == cur - 1)
    score = jnp.where(valid_s, jnp.where(forced, FORCE_SCORE, imp), -1.0)
    sc_s[...] = jnp.zeros(sc_s.shape, F32)
    sc_s[0:grp] = jnp.where(is_blk, score, -jnp.inf)
    sel = _topk_mask(sc_s[...], min(N_SEL, n_slc))[0:grp]
    sel = jnp.where(is_blk & valid_s, sel, 0.0)
    notsel = (1.0 - sel).astype(BF16)
    q_aug = jnp.concatenate([qa, jnp.concatenate([notsel] * GQA, axis=0)], axis=1)

    n_parts = 4
    part = n_keys // n_parts
    s_parts = []
    for c in range(n_parts):
        k_aug = jnp.concatenate([ks_s[c * part:(c + 1) * part], eneg_ref[c * part:(c + 1) * part]], axis=1)
        s_parts.append(_dot_nt(q_aug, k_aug))
    s_s = jnp.concatenate(s_parts, axis=1)
    kpos = lax.broadcasted_iota(jnp.int32, s_s.shape, 1)
    s_s = jnp.where(kpos <= qpos(s_s.shape), s_s, NEG)
    e = jnp.exp(s_s - jnp.max(s_s, axis=-1, keepdims=True))
    p_s = (e / jnp.sum(e, axis=-1, keepdims=True)).astype(BF16)
    o_sel = jnp.dot(p_s[:, 0:part], vs_s[0:part], preferred_element_type=F32)
    for c in range(1, n_parts):
        o_sel = o_sel + jnp.dot(p_s[:, c * part:(c + 1) * part], vs_s[c * part:(c + 1) * part],
                                preferred_element_type=F32)

    s_w = _dot_nt(qa, kwb_s[...])
    kwpos = past_len - wl + lax.broadcasted_iota(jnp.int32, s_w.shape, 1)
    qp = qpos(s_w.shape)
    okw = (kwpos <= qp) & (kwpos > qp - WINDOW) & (kwpos >= 0)
    p_w = _masked_softmax(s_w, okw)
    o_w = jnp.dot(p_w.astype(BF16), vwb_s[...], preferred_element_type=F32)

    sig = _sigmoid(gn_ref[0])
    o = None
    for j, ob in enumerate((o_c, o_sel, o_w)):
        for g in range(GQA):
            for k in range(N_KV):
                col = 3 * (k * GQA + g) + j
                g_s[(g * N_KV + k) * ts:(g * N_KV + k + 1) * ts] = jnp.broadcast_to(sig[:, col:col + 1], (ts, LANES))
        o = g_s[...] * ob if o is None else o + g_s[...] * ob
    o_s[...] = o
    for j in range(N_HEADS // 2):
        tile = None
        for h in (2 * j, 2 * j + 1):
            k, g = h // GQA, h % GQA
            piece = _head_rows(o_s[(g * N_KV + k) * ts:(g * N_KV + k + 1) * ts], k, h % 2, lane_t)
            tile = piece if tile is None else tile + piece
        o_ref[0, :, j * LANES:(j + 1) * LANES] = tile


def _nsa_sample(page_table, q, gn, new_rows, win_k, win_v, caches, wck, wcv, eneg, l, *, n_pool):
    bd, ts, _ = q.shape
    n_pages = page_table.shape[1]
    past_len = n_pages * PAGE_SIZE
    wl = win_k.shape[1]
    n_keys = -(-(past_len + ts) // (4 * LANES)) * 4 * LANES
    assert -(-(past_len + ts) // L_SLC) * 2 <= LANES and eneg.shape == (n_keys, LANES)
    rows = GQA * N_KV * ts
    tok = lambda w: pl.BlockSpec((1, ts, w), lambda b, pt: (b, 0, 0))
    win = pl.BlockSpec((1, wl, KV_W), lambda b, pt: (l * bd + b, 0, 0))
    wrow = pl.BlockSpec((None, L_CMP, KV_W), lambda b, pt: (l, 0, 0))

    def page(p):
        return pl.BlockSpec((1, PAGE_SIZE, KV_W), lambda b, pt: (l * n_pool + pt[b * n_pages + p], 0, 0))

    in_specs = ([tok(N_HEADS * HEAD_DIM), tok(LANES)] + [tok(KV_W)] * 6 + [win, win, wrow, wrow,
                pl.BlockSpec((n_keys, LANES), lambda b, pt: (0, 0))] + [page(p) for _ in range(4) for p in range(n_pages)])
    grid_spec = pltpu.PrefetchScalarGridSpec(
        num_scalar_prefetch=1,
        grid=(bd,),
        in_specs=in_specs,
        out_specs=[tok(N_HEADS * HEAD_DIM), pl.BlockSpec((1, wl, KV_W), lambda b, pt: (b, 0, 0)),
                   pl.BlockSpec((1, wl, KV_W), lambda b, pt: (b, 0, 0))],
        scratch_shapes=[
            pltpu.VMEM((LANES, KV_W), F32), pltpu.VMEM((LANES, KV_W), F32),
            pltpu.VMEM((n_keys, KV_W), BF16), pltpu.VMEM((n_keys, KV_W), BF16),
            pltpu.VMEM((wl + LANES, KV_W), F32), pltpu.VMEM((wl + LANES, KV_W), F32),
            pltpu.VMEM((wl + LANES, KV_W), BF16), pltpu.VMEM((wl + LANES, KV_W), BF16),
            pltpu.VMEM((n_keys - past_len, KV_W), F32),
            pltpu.VMEM((rows, LANES), F32), pltpu.VMEM((LANES, LANES), F32),
            pltpu.VMEM((rows, LANES), F32), pltpu.VMEM((rows, LANES), F32),
        ],
    )
    cache_args = [c for c in caches for _ in range(n_pages)]
    return pl.pallas_call(
        functools.partial(_nsa_sample_kernel, n_pages=n_pages, past_len=past_len, ts=ts),
        grid_spec=grid_spec,
        out_shape=[jax.ShapeDtypeStruct((bd, ts, N_HEADS * HEAD_DIM), F32),
                   jax.ShapeDtypeStruct((bd, wl, KV_W), F32), jax.ShapeDtypeStruct((bd, wl, KV_W), F32)],
        compiler_params=_params("arbitrary"),
        name="nsa_sample",
    )(page_table.reshape(-1), q, gn, *new_rows, win_k, win_v, wck, wcv, eneg, *cache_args)


def _merge_kernel(x_ref, sc_ref, sh_ref, gt_ref, g_ref, a_ref, b_ref, c_ref, wgm_ref, wp_ref, wc_ref, wa_ref,
                  wo_ref, o_ref):
    x = x_ref[0]
    d = x.shape[-1]
    h = _modnorm(x, g_ref[...], sc_ref[0], sh_ref[0]).astype(BF16)
    branches = (
        jnp.dot(a_ref[0].astype(BF16), wp_ref[...], preferred_element_type=F32),
        jnp.dot(b_ref[0].astype(BF16), wc_ref[...], preferred_element_type=F32),
        jnp.dot(c_ref[0].astype(BF16), wa_ref[...], preferred_element_type=F32),
    )
    merged = None
    for j, br in enumerate(branches):
        gm = _sigmoid(jnp.dot(h, wgm_ref[:, j * d:(j + 1) * d], preferred_element_type=F32))
        merged = gm * br if merged is None else merged + gm * br
    out = jnp.dot(merged.astype(BF16), wo_ref[...], preferred_element_type=F32)
    o_ref[0] = x + gt_ref[0] * out


def _mod_spec(tmod, tm, d):
    if tmod == 1:
        return pl.BlockSpec((1, 1, d), lambda bi, i: (bi, 0, 0))
    return pl.BlockSpec((1, tm, d), lambda bi, i: (bi, i, 0))


def _merge(x, sc, sh, gt, g, a, bo, c, wgm, wp, wc, wa, wo, l, *, tm):
    b, t, d = x.shape
    tm = min(tm, t)
    assert t % tm == 0
    ms = _mod_spec(sc.shape[1], tm, d)
    row = lambda w: pl.BlockSpec((1, tm, w), lambda bi, i: (bi, i, 0))
    wsp = lambda w: pl.BlockSpec((None,) + w.shape[1:], lambda bi, i: (l, 0, 0))
    return pl.pallas_call(
        _merge_kernel,
        grid=(b, t // tm),
        in_specs=[row(d), ms, ms, ms, pl.BlockSpec((None, 1, d), lambda bi, i: (l, 0, 0)),
                  row(a.shape[-1]), row(bo.shape[-1]), row(c.shape[-1]),
                  wsp(wgm), wsp(wp), wsp(wc), wsp(wa), wsp(wo)],
        out_specs=row(d),
        out_shape=jax.ShapeDtypeStruct((b, t, d), F32),
        compiler_params=_params("parallel", "parallel"),
        name="merge_out",
    )(x, sc, sh, gt, g, a, bo, c, wgm, wp, wc, wa, wo)


def _ffn_kernel(x_ref, sc_ref, sh_ref, gt_ref, g_ref, wgu_ref, wd_ref, gf_ref, *outs, d_ff, n_parts, final):
    x = x_ref[0]
    h = _modnorm(x, g_ref[...], sc_ref[0], sh_ref[0]).astype(BF16)
    part = d_ff // n_parts
    acc = None
    for c in range(n_parts):
        gp = jnp.dot(h, wgu_ref[:, c * part:(c + 1) * part], preferred_element_type=F32)
        up = jnp.dot(h, wgu_ref[:, d_ff + c * part:d_ff + (c + 1) * part], preferred_element_type=F32)
        act = (gp * _sigmoid(gp) * up).astype(BF16)
        dn = jnp.dot(act, wd_ref[c * part:(c + 1) * part, :], preferred_element_type=F32)
        acc = dn if acc is None else acc + dn
    y = x + gt_ref[0] * acc
    outs[0][0] = y
    if final:
        ms = jnp.mean(y * y, axis=-1, keepdims=True)
        outs[1][0] = y * lax.rsqrt(ms + EPS) * gf_ref[...]


def _ffn(x, sc, sh, gt, g, wgu, wd, gf, l, *, tm, final):
    b, t, d = x.shape
    d_ff = wd.shape[1]
    tm = min(tm, t)
    n_parts = 2
    assert t % tm == 0 and d_ff % (n_parts * LANES) == 0
    ms = _mod_spec(sc.shape[1], tm, d)
    row = pl.BlockSpec((1, tm, d), lambda bi, i: (bi, i, 0))
    n_out = 2 if final else 1
    return pl.pallas_call(
        functools.partial(_ffn_kernel, d_ff=d_ff, n_parts=n_parts, final=final),
        grid=(b, t // tm),
        in_specs=[row, ms, ms, ms, pl.BlockSpec((None, 1, d), lambda bi, i: (l, 0, 0)),
                  pl.BlockSpec((None, d, 2 * d_ff), lambda bi, i: (l, 0, 0)),
                  pl.BlockSpec((None, d_ff, d), lambda bi, i: (l, 0, 0)),
                  pl.BlockSpec((1, d), lambda bi, i: (0, 0))],
        out_specs=[row] * n_out,
        out_shape=[jax.ShapeDtypeStruct((b, t, d), F32)] * n_out,
        compiler_params=_params("parallel", "parallel"),
        name="ffn",
    )(x, sc, sh, gt, g, wgu, wd, gf)


def _rope_tables(pos):
    half = ROPE_DIM // 2
    inv = ROPE_THETA ** (-jnp.arange(half, dtype=F32) * 2.0 / ROPE_DIM)
    ang = pos.astype(F32)[:, None] * inv[None, :]
    cos, sin = jnp.cos(ang), jnp.sin(ang)
    n = pos.shape[0]
    rest = HEAD_DIM - ROPE_DIM
    c = jnp.concatenate([cos, cos, jnp.ones((n, rest), F32)], axis=1)
    sa = jnp.concatenate([-sin, jnp.zeros((n, half + rest), F32)], axis=1)
    sb = jnp.concatenate([jnp.zeros((n, half), F32), sin, jnp.zeros((n, rest), F32)], axis=1)
    rep = LANES // HEAD_DIM
    return tuple(jnp.tile(a, (1, rep)) for a in (c, sa, sb))


def _block_bias(n_keys, lane_stride):
    key_blk = (jnp.arange(n_keys) // L_SLC)[:, None] * lane_stride
    return jnp.where(key_blk == jnp.arange(LANES)[None, :], NEG, 0.0).astype(BF16)


def kernel(x_prompt, x_sample, cache_cmp_k, cache_cmp_v, cache_slc_k, cache_slc_v, state_win_k, state_win_v,
           state_pool, state_conv, page_table, c_prompt, c_sample, w_ada, b_ada, g_norm_mix, g_norm_ffn, w_in,
           w_pool, s_pool, w_pool_out, w_dw, b_dw, g_conv_ln, b_conv_ln, w_conv_out, w_cmp_k, w_cmp_v,
           w_attn_out, w_out, w_gu, w_down, g_final):
    depth, d, _ = w_in.shape
    bp, tp, _ = x_prompt.shape
    bd, ts, _ = x_sample.shape
    wp_dim = w_pool_out.shape[1]
    wc_dim = w_conv_out.shape[1]
    n_pool = cache_cmp_k.shape[1]
    n_pages = page_table.shape[1]
    past_len = n_pages * PAGE_SIZE
    wl = state_win_k.shape[2]
    n_groups, gw = w_pool.shape[1], w_pool.shape[2]
    assert gw == POOL_WINDOWS_GW and n_groups == len(POOL_WINDOWS) and wp_dim == n_groups * gw

    n_a = wp_dim + 2 * wc_dim + N_HEADS * HEAD_DIM + 6 * KV_W + 3 * N_HEADS
    n_a_pad = n_a - 3 * N_HEADS + LANES
    w_in_a = jnp.pad(w_in[:, :, :n_a], ((0, 0), (0, 0), (0, n_a_pad - n_a))).astype(BF16)
    w_gm = w_in[:, :, n_a:].astype(BF16)
    w_ada_b = w_ada.astype(BF16)
    eye = jnp.eye(n_groups, dtype=F32)
    wpool_bd = (w_pool[:, :, :, None, :] * eye[None, :, None, :, None]).reshape(depth, wp_dim, wp_dim).astype(BF16)
    wpo, wco, wao, wo = (w.astype(BF16) for w in (w_pool_out, w_conv_out, w_attn_out, w_out))
    wgu, wdn = w_gu.astype(BF16), w_down.astype(BF16)
    wck = jnp.broadcast_to(w_cmp_k[:, :, None], (depth, L_CMP, KV_W))
    wcv = jnp.broadcast_to(w_cmp_v[:, :, None], (depth, L_CMP, KV_W))
    vec3 = lambda a: a.reshape(depth, 1, -1)
    g_mix, g_ffn = vec3(g_norm_mix), vec3(g_norm_ffn)
    spool, bdw, gln, bln = vec3(s_pool), vec3(b_dw), vec3(g_conv_ln), vec3(b_conv_ln)
    gf = g_final.reshape(1, d)

    tabs_p = _rope_tables(jnp.arange(tp))
    tabs_s = _rope_tables(jnp.tile(past_len + jnp.arange(ts), bd))
    eneg_p = _block_bias(tp, 1)
    n_keys_s = -(-(past_len + ts) // (4 * LANES)) * 4 * LANES
    eneg_s = _block_bias(n_keys_s, 2)

    ada = _ada_all(jnp.concatenate([c_prompt, c_sample], axis=0), w_ada_b, b_ada)
    caches = tuple(c.reshape(depth * n_pool, PAGE_SIZE, KV_W)
                   for c in (cache_cmp_k, cache_cmp_v, cache_slc_k, cache_slc_v))
    win_k_all = state_win_k.reshape(depth * bd, wl, KV_W)
    win_v_all = state_win_v.reshape(depth * bd, wl, KV_W)

    xp = x_prompt
    xs = x_sample.reshape(1, bd * ts, d)
    st_p, st_s = [], []
    yp = ys = None
    for l in range(depth):
        final = l == depth - 1
        mp = [m[:, None, :] for m in jnp.split(ada[l, :bp], 6, axis=-1)]
        ms = [jnp.repeat(m, ts, axis=0)[None] for m in jnp.split(ada[l, bp:], 6, axis=-1)]

        (up, uc, q, kc, vc, ks, vs, kw, vw, gn, ck, cv) = _inproj(
            xp, mp[1], mp[0], g_mix, w_in_a, l, tabs_p, wck, wcv, w_pool=wp_dim, w_conv=wc_dim, with_cmp=True,
            q_dtype=BF16, tm=512)
        a_o, b_o, cst = _mix_prompt(up, uc, wpool_bd, spool, w_dw, bdw, gln, bln, l, tt=512)
        n_cmp = tp // L_CMP
        perm = lambda a: a.reshape(bp, n_cmp // 2, 2, KV_W).transpose(0, 2, 1, 3).reshape(bp, n_cmp, KV_W).astype(BF16)
        c_o = _nsa_prompt(q, gn, perm(ck), perm(cv), ks.astype(BF16), vs.astype(BF16), kw.astype(BF16),
                          vw.astype(BF16), eneg_p)
        xp = _merge(xp, mp[1], mp[0], mp[2], g_mix, a_o, b_o, c_o, w_gm, wpo, wco, wao, wo, l, tm=512)
        res = _ffn(xp, mp[4], mp[3], mp[5], g_ffn, wgu, wdn, gf, l, tm=512, final=final)
        xp = res[0]
        if final:
            yp = res[1]
        kv4 = lambda a: a.reshape(a.shape[0], a.shape[1], N_KV, HEAD_DIM)
        wlp = min(WINDOW, tp)
        st_p.append((kv4(kc), kv4(vc), kv4(ks), kv4(vs), kv4(kw[:, tp - wlp:]), kv4(vw[:, tp - wlp:]),
                     up[:, tp - (max(POOL_WINDOWS) - 1):], cst[:, HALO - (CONV_K - 1):]))

        (up, uc, q, kc, vc, ks, vs, kw, vw, gn) = _inproj(
            xs, ms[1], ms[0], g_mix, w_in_a, l, tabs_s, wck, wcv, w_pool=wp_dim, w_conv=wc_dim, with_cmp=False,
            q_dtype=F32, tm=512)
        tmaj = lambda a: a.reshape(bd, ts, a.shape[-1]).transpose(1, 0, 2)
        a_t, b_t, glu_t = _mix_sample(state_pool[l].transpose(1, 0, 2), tmaj(up), state_conv[l].transpose(1, 0, 2),
                                      tmaj(uc), wpool_bd, spool, w_dw, bdw, gln, bln, l)
        bmaj = lambda a: a.transpose(1, 0, 2).reshape(1, bd * ts, a.shape[-1])
        per_seq = lambda a: a.reshape(bd, ts, a.shape[-1])
        c_o, nwk, nwv = _nsa_sample(page_table, per_seq(q), per_seq(gn),
                                    [per_seq(a) for a in (kc, vc, ks, vs, kw, vw)], win_k_all, win_v_all, caches,
                                    wck, wcv, eneg_s, l, n_pool=n_pool)
        xs = _merge(xs, ms[1], ms[0], ms[2], g_mix, bmaj(a_t), bmaj(b_t), c_o.reshape(1, bd * ts, -1), w_gm, wpo,
                    wco, wao, wo, l, tm=512)
        res = _ffn(xs, ms[4], ms[3], ms[5], g_ffn, wgu, wdn, gf, l, tm=512, final=final)
        xs = res[0]
        if final:
            ys = res[1]
        kv4s = lambda a: a.reshape(bd, -1, N_KV, HEAD_DIM)
        new_pool = jnp.concatenate([state_pool[l][:, ts:], per_seq(up)], axis=1)
        new_conv = jnp.concatenate([state_conv[l][:, ts:], glu_t.transpose(1, 0, 2)], axis=1)
        st_s.append((kv4s(kc), kv4s(vc), kv4s(ks), kv4s(vs), kv4s(nwk), kv4s(nwv), new_pool, new_conv))

    outs = [yp, ys.reshape(bd, ts, d)]
    for i in range(8):
        outs.append(jnp.stack([s[i] for s in st_p]))
        outs.append(jnp.stack([s[i] for s in st_s]))
    return tuple(outs)
```

```python
import functools

import jax
import jax.numpy as jnp
from jax import lax
from jax.experimental import pallas as pl
from jax.experimental.pallas import tpu as pltpu

F32 = jnp.float32
BF16 = jnp.bfloat16

HEAD_DIM = 64
N_KV = 2
GQA = 4
N_HEADS = N_KV * GQA
ROPE_DIM = 16
ROPE_THETA = 500000.0
L_CMP = 32
L_SLC = 64
L_SLC_SHIFT = 6
N_SEL = 16
WINDOW = 512
Q_BLOCK = 128
PAGE_SIZE = 128
POOL_WINDOWS = (2, 4, 8, 16)
CONV_K = 31
EPS = 1e-6
NEG = -1e30
FORCE_SCORE = 1e4

LANES = 128
SUBLANES = 8
VMEM_LIMIT_BYTES = 56 * 1024 * 1024

NSA_CHUNK_KEYS = 512
HALO = 32
KV_W = N_KV * HEAD_DIM


def _params(*sem):
    return pltpu.CompilerParams(dimension_semantics=sem, vmem_limit_bytes=VMEM_LIMIT_BYTES)


def _modnorm(x, g, sc, sh):
    ms = jnp.mean(x * x, axis=-1, keepdims=True)
    return (x * lax.rsqrt(ms + EPS) * g) * (1.0 + sc) + sh


def _sigmoid(x):
    return 1.0 / (1.0 + jnp.exp(-x))


def _dot_nt(a, b):
    return lax.dot_general(a, b, (((1,), (1,)), ((), ())), preferred_element_type=F32)


def _masked_softmax(s, mask):
    sm = jnp.where(mask, s, NEG)
    e = jnp.exp(sm - jnp.max(sm, axis=-1, keepdims=True))
    return jnp.where(mask, e / jnp.sum(e, axis=-1, keepdims=True), 0.0)


def _ada_kernel(c_ref, w_ref, b_ref, o_ref):
    c = c_ref[...]
    a = (c * _sigmoid(c)).astype(BF16)
    o_ref[...] = jnp.dot(a, w_ref[...], preferred_element_type=F32) + b_ref[...]


def _ada_all(c_all, w_ada, b_ada):
    depth, d, n = w_ada.shape
    tn = 1536
    assert n % tn == 0
    nb = c_all.shape[0]
    return pl.pallas_call(
        _ada_kernel,
        grid=(depth, n // tn),
        in_specs=[
            pl.BlockSpec((nb, d), lambda l, j: (0, 0)),
            pl.BlockSpec((None, d, tn), lambda l, j: (l, 0, j)),
            pl.BlockSpec((None, 1, tn), lambda l, j: (l, 0, j)),
        ],
        out_specs=pl.BlockSpec((None, nb, tn), lambda l, j: (l, 0, j)),
        out_shape=jax.ShapeDtypeStruct((depth, nb, n), F32),
        compiler_params=_params("parallel", "parallel"),
        name="ada_mod",
    )(c_all, w_ada, b_ada.reshape(depth, 1, n))


def _inproj_kernel(x_ref, sc_ref, sh_ref, g_ref, w_ref, cos_ref, sa_ref, sb_ref, wck_ref, wcv_ref, *outs,
                   w_pool, w_conv, with_cmp):
    (up_ref, uc_ref, q_ref, kc_ref, vc_ref, ks_ref, vs_ref, kw_ref, vw_ref, gn_ref) = outs[:10]
    h = _modnorm(x_ref[0], g_ref[...], sc_ref[0], sh_ref[0]).astype(BF16)
    y = jnp.dot(h, w_ref[...], preferred_element_type=F32)
    cos, sa, sb = cos_ref[...], sa_ref[...], sb_ref[...]

    def rope(t):
        return t * cos + pltpu.roll(t, LANES - ROPE_DIM // 2, 1) * sa + pltpu.roll(t, ROPE_DIM // 2, 1) * sb

    o = 0
    up_ref[0] = y[:, o:o + w_pool]
    o += w_pool
    uc_ref[0] = y[:, o:o + 2 * w_conv]
    o += 2 * w_conv
    for j in range(N_HEADS * HEAD_DIM // LANES):
        q_ref[0, :, j * LANES:(j + 1) * LANES] = rope(y[:, o:o + LANES]).astype(q_ref.dtype)
        o += LANES
    kc = rope(y[:, o:o + KV_W])
    vc = y[:, o + KV_W:o + 2 * KV_W]
    ks = rope(y[:, o + 2 * KV_W:o + 3 * KV_W])
    vs = y[:, o + 3 * KV_W:o + 4 * KV_W]
    kw = rope(y[:, o + 4 * KV_W:o + 5 * KV_W])
    vw = y[:, o + 5 * KV_W:o + 6 * KV_W]
    o += 6 * KV_W
    kc_ref[0], vc_ref[0], ks_ref[0], vs_ref[0], kw_ref[0], vw_ref[0] = kc, vc, ks, vs, kw, vw
    gn_ref[0] = y[:, o:o + LANES]
    if with_cmp:
        ck_ref, cv_ref = outs[10:]
        tm = kc.shape[0]
        ck_ref[0] = jnp.sum(kc.reshape(tm // L_CMP, L_CMP, KV_W) * wck_ref[...][None], axis=1)
        cv_ref[0] = jnp.sum(vc.reshape(tm // L_CMP, L_CMP, KV_W) * wcv_ref[...][None], axis=1)


def _inproj(x, sc, sh, g, w_a, l, tabs, wck, wcv, *, w_pool, w_conv, with_cmp, q_dtype, tm):
    b, t, d = x.shape
    n = w_a.shape[-1]
    tm = min(tm, t)
    assert t % tm == 0 and (tm % L_CMP == 0 or not with_cmp)
    tmod = sc.shape[1]
    mod_spec = (pl.BlockSpec((1, 1, d), lambda bi, i: (bi, 0, 0)) if tmod == 1
                else pl.BlockSpec((1, tm, d), lambda bi, i: (bi, i, 0)))
    tab_spec = pl.BlockSpec((tm, LANES), lambda bi, i: (i, 0))
    row = lambda w: pl.BlockSpec((1, tm, w), lambda bi, i: (bi, i, 0))
    widths = [w_pool, 2 * w_conv, N_HEADS * HEAD_DIM] + [KV_W] * 6 + [LANES]
    dtypes = [F32, F32, q_dtype] + [F32] * 7
    out_specs = [row(w) for w in widths]
    out_shape = [jax.ShapeDtypeStruct((b, t, w), dt) for w, dt in zip(widths, dtypes)]
    if with_cmp:
        out_specs += [pl.BlockSpec((1, tm // L_CMP, KV_W), lambda bi, i: (bi, i, 0))] * 2
        out_shape += [jax.ShapeDtypeStruct((b, t // L_CMP, KV_W), F32)] * 2
    return pl.pallas_call(
        functools.partial(_inproj_kernel, w_pool=w_pool, w_conv=w_conv, with_cmp=with_cmp),
        grid=(b, t // tm),
        in_specs=[
            row(d), mod_spec, mod_spec,
            pl.BlockSpec((None, 1, d), lambda bi, i: (l, 0, 0)),
            pl.BlockSpec((None, d, n), lambda bi, i: (l, 0, 0)),
            tab_spec, tab_spec, tab_spec,
            pl.BlockSpec((None, L_CMP, KV_W), lambda bi, i: (l, 0, 0)),
            pl.BlockSpec((None, L_CMP, KV_W), lambda bi, i: (l, 0, 0)),
        ],
        out_specs=out_specs,
        out_shape=out_shape,
        compiler_params=_params("parallel", "parallel"),
        name="in_proj",
    )(x, sc, sh, g, w_a, *tabs, wck, wcv)


def _pool_means(ext_ref, base, rows, pos0):
    lane = lax.broadcasted_iota(jnp.int32, (rows, LANES), 1)
    pos = lax.broadcasted_iota(jnp.int32, (rows, LANES), 0) + pos0
    low = lane < POOL_WINDOWS_GW
    means = []
    for tile in range(len(POOL_WINDOWS) // 2):
        cols = slice(tile * LANES, (tile + 1) * LANES)
        w_small, w_big = POOL_WINDOWS[2 * tile], POOL_WINDOWS[2 * tile + 1]
        acc = ext_ref[base:base + rows, cols]
        small = acc
        for s in range(1, w_big):
            acc = acc + ext_ref[base - s:base - s + rows, cols]
            if s == w_small - 1:
                small = acc
        cnt = jnp.minimum(jnp.where(low, w_small, w_big), pos + 1).astype(F32)
        means.append(jnp.where(low, small, acc) / cnt)
    return jnp.concatenate(means, axis=1)


POOL_WINDOWS_GW = 64


def _layernorm_silu(y, g, b):
    mu = jnp.mean(y, axis=-1, keepdims=True)
    yc = y - mu
    var = jnp.mean(yc * yc, axis=-1, keepdims=True)
    z = yc * lax.rsqrt(var + EPS) * g + b
    return z * _sigmoid(z)


def _mix_prompt_kernel(up_ref, uph_ref, uc_ref, uch_ref, wpool_ref, spool_ref, wdw_ref, bdw_ref, gln_ref, bln_ref,
                       a_ref, b_ref, cst_ref, pext_ref, cext_ref, *, tt, w_conv):
    i = pl.program_id(1)
    first = i == 0
    u = up_ref[0]
    pext_ref[0:HALO] = jnp.where(first, 0.0, uph_ref[0])
    pext_ref[HALO:HALO + tt] = u
    d = _pool_means(pext_ref, HALO, tt, i * tt) - u
    z = jnp.dot(d.astype(BF16), wpool_ref[...], preferred_element_type=F32) * spool_ref[...]
    a_ref[0] = z.astype(a_ref.dtype)
    hc = uch_ref[0]
    cext_ref[0:HALO] = jnp.where(first, 0.0, hc[:, :w_conv] * _sigmoid(hc[:, w_conv:]))
    uc = uc_ref[0]
    cext_ref[HALO:HALO + tt] = uc[:, :w_conv] * _sigmoid(uc[:, w_conv:])
    off = HALO - (CONV_K - 1)
    acc = wdw_ref[0:1, :] * cext_ref[off:off + tt]
    for k in range(1, CONV_K):
        acc = acc + wdw_ref[k:k + 1, :] * cext_ref[off + k:off + k + tt]
    b_ref[0] = _layernorm_silu(acc + bdw_ref[...], gln_ref[...], bln_ref[...]).astype(b_ref.dtype)
    cst_ref[0] = cext_ref[tt:tt + HALO]


def _mix_prompt(up, uc, wpool_bd, spool, wdw, bdw, gln, bln, l, *, tt):
    b, t, w_pool = up.shape
    w_conv = uc.shape[-1] // 2
    tt = min(tt, t)
    assert t % tt == 0 and tt % HALO == 0
    r = tt // HALO
    halo = lambda w: pl.BlockSpec((1, HALO, w), lambda bi, i: (bi, jnp.maximum(i * r - 1, 0), 0))
    vec = lambda w: pl.BlockSpec((None, 1, w), lambda bi, i: (l, 0, 0))
    return pl.pallas_call(
        functools.partial(_mix_prompt_kernel, tt=tt, w_conv=w_conv),
        grid=(b, t // tt),
        in_specs=[
            pl.BlockSpec((1, tt, w_pool), lambda bi, i: (bi, i, 0)), halo(w_pool),
            pl.BlockSpec((1, tt, 2 * w_conv), lambda bi, i: (bi, i, 0)), halo(2 * w_conv),
            pl.BlockSpec((None, w_pool, w_pool), lambda bi, i: (l, 0, 0)), vec(w_pool),
            pl.BlockSpec((None, CONV_K, w_conv), lambda bi, i: (l, 0, 0)), vec(w_conv), vec(w_conv), vec(w_conv),
        ],
        out_specs=[
            pl.BlockSpec((1, tt, w_pool), lambda bi, i: (bi, i, 0)),
            pl.BlockSpec((1, tt, w_conv), lambda bi, i: (bi, i, 0)),
            pl.BlockSpec((1, HALO, w_conv), lambda bi, i: (bi, 0, 0)),
        ],
        out_shape=[
            jax.ShapeDtypeStruct((b, t, w_pool), BF16),
            jax.ShapeDtypeStruct((b, t, w_conv), BF16),
            jax.ShapeDtypeStruct((b, HALO, w_conv), F32),
        ],
        scratch_shapes=[pltpu.VMEM((HALO + tt, w_pool), F32), pltpu.VMEM((HALO + tt, w_conv), F32)],
        compiler_params=_params("parallel", "arbitrary"),
        name="mix_prompt",
    )(up, up, uc, uc, wpool_bd, spool, wdw, bdw, gln, bln)


def _mix_sample_kernel(pst_ref, up_ref, cst_ref, uc_ref, wpool_ref, spool_ref, wdw_ref, bdw_ref, gln_ref, bln_ref,
                       a_ref, b_ref, glu_ref, *, w_conv):
    n_p, n_c, ts = pst_ref.shape[0], cst_ref.shape[0], up_ref.shape[0]
    prow = lambda r: pst_ref[r] if r < n_p else up_ref[r - n_p]
    low = lax.broadcasted_iota(jnp.int32, (up_ref.shape[1], LANES), 1) < POOL_WINDOWS_GW
    glu = []
    for t in range(ts):
        uc = uc_ref[t]
        glu.append(uc[:, :w_conv] * _sigmoid(uc[:, w_conv:]))
        glu_ref[t] = glu[t]
    crow = lambda r: cst_ref[r] if r < n_c else glu[r - n_c]
    for t in range(ts):
        u = up_ref[t]
        means = None
        acc = u
        sums = {1: acc}
        for s in range(1, max(POOL_WINDOWS)):
            acc = acc + prow(n_p + t - s)
            sums[s + 1] = acc
        tiles = []
        for tile in range(len(POOL_WINDOWS) // 2):
            cols = slice(tile * LANES, (tile + 1) * LANES)
            w_small, w_big = POOL_WINDOWS[2 * tile], POOL_WINDOWS[2 * tile + 1]
            tiles.append(jnp.where(low, sums[w_small][:, cols] / float(w_small),
                                   sums[w_big][:, cols] / float(w_big)))
        means = jnp.concatenate(tiles, axis=1)
        z = jnp.dot((means - u).astype(BF16), wpool_ref[...], preferred_element_type=F32) * spool_ref[...]
        a_ref[t] = z.astype(a_ref.dtype)
        acc = wdw_ref[0:1, :] * crow(t + n_c - (CONV_K - 1))
        for k in range(1, CONV_K):
            acc = acc + wdw_ref[k:k + 1, :] * crow(t + n_c - (CONV_K - 1) + k)
        b_ref[t] = _layernorm_silu(acc + bdw_ref[...], gln_ref[...], bln_ref[...]).astype(b_ref.dtype)


def _mix_sample(pst_t, up_t, cst_t, uc_t, wpool_bd, spool, wdw, bdw, gln, bln, l):
    ts, bd, w_pool = up_t.shape
    w_conv = uc_t.shape[-1] // 2
    full = lambda a: pl.BlockSpec(a.shape, lambda i: (0,) * a.ndim)
    vec = lambda w: pl.BlockSpec((None, 1, w), lambda i: (l, 0, 0))
    return pl.pallas_call(
        functools.partial(_mix_sample_kernel, w_conv=w_conv),
        grid=(1,),
        in_specs=[
            full(pst_t), full(up_t), full(cst_t), full(uc_t),
            pl.BlockSpec((None, w_pool, w_pool), lambda i: (l, 0, 0)), vec(w_pool),
            pl.BlockSpec((None, CONV_K, w_conv), lambda i: (l, 0, 0)), vec(w_conv), vec(w_conv), vec(w_conv),
        ],
        out_specs=[pl.BlockSpec((ts, bd, w_pool), lambda i: (0, 0, 0)),
                   pl.BlockSpec((ts, bd, w_conv), lambda i: (0, 0, 0)),
                   pl.BlockSpec((ts, bd, w_conv), lambda i: (0, 0, 0))],
        out_shape=[jax.ShapeDtypeStruct((ts, bd, w_pool), BF16),
                   jax.ShapeDtypeStruct((ts, bd, w_conv), BF16),
                   jax.ShapeDtypeStruct((ts, bd, w_conv), F32)],
        compiler_params=_params("arbitrary"),
        name="mix_sample",
    )(pst_t, up_t, cst_t, uc_t, wpool_bd, spool, wdw, bdw, gln, bln)


def _topk_mask(score, n_sel):
    st = score.T
    cand = lax.broadcasted_iota(jnp.int32, st.shape, 0).astype(F32)

    def body(_, carry):
        st, sel = carry
        m = jnp.max(st, axis=0, keepdims=True)
        first = jnp.min(jnp.where(st == m, cand, float(LANES)), axis=0, keepdims=True)
        hit = cand == first
        return jnp.where(hit, -jnp.inf, st), jnp.where(hit, 1.0, sel)

    _, sel = lax.fori_loop(0, n_sel, body, (st, jnp.zeros(st.shape, F32)))
    return sel.T


def _head_rows(tile, src_half, dst_half, lane):
    if src_half != dst_half:
        tile = pltpu.roll(tile, HEAD_DIM, 1)
    return jnp.where((lane >= dst_half * HEAD_DIM) & (lane < (dst_half + 1) * HEAD_DIM), tile, 0.0)


def _nsa_prompt_kernel(q_ref, gn_ref, ck_ref, cv_ref, ks_ref, vs_ref, kw_ref, vw_ref, eneg_ref, o_ref,
                       qa_scr, s_scr, p_scr, *, ck_keys):
    i = pl.program_id(1)
    qb = Q_BLOCK
    rows = GQA * qb
    scale = HEAD_DIM ** -0.5
    n_cmp = ck_ref.shape[1]
    n_slc = n_cmp // 2
    lane = lax.broadcasted_iota(jnp.int32, (qb, LANES), 1)

    def qpos(shape):
        return i * qb + (lax.broadcasted_iota(jnp.int32, shape, 0) & (qb - 1))

    sig = _sigmoid(gn_ref[0])
    n_wchunks = WINDOW // qb + 1
    out_tiles = [None] * (N_HEADS // 2)
    qks, q_augs, o_cs = [], [], []
    for k in range(N_KV):
        qh = []
        for g in range(GQA):
            h = k * GQA + g
            tile = q_ref[0, :, (h // 2) * LANES:(h // 2 + 1) * LANES].astype(F32) * scale
            qh.append(_head_rows(tile, h % 2, k, lane))
        qk = jnp.concatenate(qh, axis=0).astype(BF16)

        s_c = _dot_nt(qk, ck_ref[0])
        cl = lax.broadcasted_iota(jnp.int32, (rows, n_cmp), 1)
        c_end = jnp.where(cl < n_slc, cl * L_SLC + L_CMP - 1, (cl - n_slc) * L_SLC + L_SLC - 1)
        p_c = _masked_softmax(s_c, c_end <= qpos(s_c.shape))
        o_c = jnp.dot(p_c.astype(BF16), cv_ref[0], preferred_element_type=F32)
        ps = p_c[0:qb]
        for g in range(1, GQA):
            ps = ps + p_c[g * qb:(g + 1) * qb]
        imp = ps[:, :n_slc] + ps[:, n_slc:]
        if n_slc < LANES:
            imp = jnp.concatenate([imp, jnp.zeros((qb, LANES - n_slc), F32)], axis=1)

        cur = qpos(lane.shape) >> L_SLC_SHIFT
        valid_s = lane <= cur
        forced = (lane == 0) | (lane == cur) | (lane == cur - 1)
        score = jnp.where(valid_s, jnp.where(forced, FORCE_SCORE, imp), -1.0)
        score = jnp.where(lane < n_slc, score, -jnp.inf)
        sel = jnp.where(valid_s, _topk_mask(score, min(N_SEL, n_slc)), 0.0)
        notsel = (1.0 - sel).astype(BF16)
        q_augs.append(jnp.concatenate([qk, jnp.concatenate([notsel] * GQA, axis=0)], axis=1))
        qks.append(qk)
        o_cs.append(o_c)

    for k in range(N_KV):
        qa_scr[k] = q_augs[k]
    own = lax.broadcasted_iota(jnp.int32, (ck_keys, LANES), 1) < HEAD_DIM

    def scores(c, buf):
        start = pl.multiple_of(c * ck_keys, ck_keys)
        k_aug = jnp.concatenate([ks_ref[0, pl.ds(start, ck_keys), :], eneg_ref[pl.ds(start, ck_keys), :]], axis=1)
        for k in range(N_KV):
            s_scr[buf, k] = _dot_nt(qa_scr[k], k_aug)

    def probs(c, buf, ms, causal):
        new_m, alphas = [], []
        for k in range(N_KV):
            s = s_scr[buf, k]
            if causal:
                kpos = c * ck_keys + lax.broadcasted_iota(jnp.int32, s.shape, 1)
                s = jnp.where(kpos <= qpos(s.shape), s, NEG)
            m_new = jnp.maximum(ms[k], jnp.max(s, axis=-1, keepdims=True))
            p_scr[buf, k] = jnp.exp(s - m_new).astype(BF16)
            alphas.append(jnp.exp(ms[k] - m_new))
            new_m.append(m_new)
        return new_m, alphas

    def values(c, buf, alphas, accs):
        start = pl.multiple_of(c * ck_keys, ck_keys)
        v = vs_ref[0, pl.ds(start, ck_keys), :]
        v_one = [jnp.where(own, v, 1.0), jnp.where(own, 1.0, v)]
        return [alphas[k] * accs[k] + jnp.dot(p_scr[buf, k], v_one[k], preferred_element_type=F32)
                for k in range(N_KV)]

    def pair(c, carry, causal, last):
        ms, alphas, accs = carry
        accs = values(jnp.maximum(c - 1, 0), 1, alphas, accs)
        ms, alphas = probs(c, 0, ms, causal)
        scores(c + 1, 1)
        accs = values(c, 0, alphas, accs)
        ms, alphas = probs(c + 1, 1, ms, causal)
        if last:
            accs = values(c + 1, 1, alphas, accs)
        else:
            scores(c + 2, 0)
        return ms, alphas, accs

    n_pairs = (i * qb + qb + 2 * ck_keys - 1) // (2 * ck_keys)
    p_scr[1] = jnp.zeros(p_scr.shape[1:], BF16)
    scores(0, 0)
    carry = ([jnp.full((rows, 1), -jnp.inf, F32)] * N_KV, [jnp.ones((rows, 1), F32)] * N_KV,
             [jnp.zeros((rows, LANES), F32)] * N_KV)
    carry = lax.fori_loop(0, n_pairs - 1, lambda j, cr: pair(2 * j, cr, False, False), carry)
    _, _, accs = pair(2 * (n_pairs - 1), carry, True, True)
    o_ss = [acc / pltpu.roll(acc, HEAD_DIM, 1) for acc in accs]

    for k in range(N_KV):
        qk, o_c, o_s = qks[k], o_cs[k], o_ss[k]
        kws, vws = [], []
        for dj in range(n_wchunks):
            st = pl.multiple_of(jnp.maximum(i - (n_wchunks - 1) + dj, 0) * qb, qb)
            kws.append(kw_ref[0, pl.ds(st, qb), :])
            vws.append(vw_ref[0, pl.ds(st, qb), :])
        s_w = _dot_nt(qk, jnp.concatenate(kws, axis=0))
        kwpos = (i - (n_wchunks - 1)) * qb + lax.broadcasted_iota(jnp.int32, s_w.shape, 1)
        qp = qpos(s_w.shape)
        okw = (kwpos <= qp) & (kwpos > qp - WINDOW) & (kwpos >= 0)
        p_w = _masked_softmax(s_w, okw)
        o_w = jnp.dot(p_w.astype(BF16), jnp.concatenate(vws, axis=0), preferred_element_type=F32)

        for g in range(GQA):
            h = k * GQA + g
            col = 3 * h
            r = slice(g * qb, (g + 1) * qb)
            o = sig[:, col:col + 1] * o_c[r] + sig[:, col + 1:col + 2] * o_s[r] + sig[:, col + 2:col + 3] * o_w[r]
            o = _head_rows(o, k, h % 2, lane)
            out_tiles[h // 2] = o if out_tiles[h // 2] is None else out_tiles[h // 2] + o
    for j, tile in enumerate(out_tiles):
        o_ref[0, :, j * LANES:(j + 1) * LANES] = tile.astype(o_ref.dtype)


def _nsa_prompt(q, gn, ckp, cvp, ks, vs, kw, vw, eneg):
    b, t, _ = q.shape
    assert t % Q_BLOCK == 0 and t // L_SLC <= LANES
    ck_keys = min(NSA_CHUNK_KEYS, t // 2)
    assert t % (2 * ck_keys) == 0
    rows = GQA * Q_BLOCK
    qspec = lambda w: pl.BlockSpec((1, Q_BLOCK, w), lambda bi, i: (bi, i, 0))
    seq = lambda n: pl.BlockSpec((1, n, KV_W), lambda bi, i: (bi, 0, 0))
    return pl.pallas_call(
        functools.partial(_nsa_prompt_kernel, ck_keys=ck_keys),
        scratch_shapes=[pltpu.VMEM((N_KV, rows, 2 * LANES), BF16), pltpu.VMEM((2, N_KV, rows, ck_keys), F32),
                        pltpu.VMEM((2, N_KV, rows, ck_keys), BF16)],
        grid=(b, t // Q_BLOCK),
        in_specs=[qspec(N_HEADS * HEAD_DIM), qspec(LANES), seq(t // L_CMP), seq(t // L_CMP),
                  seq(t), seq(t), seq(t), seq(t), pl.BlockSpec((t, LANES), lambda bi, i: (0, 0))],
        out_specs=qspec(N_HEADS * HEAD_DIM),
        out_shape=jax.ShapeDtypeStruct((b, t, N_HEADS * HEAD_DIM), BF16),
        compiler_params=_params("parallel", "parallel"),
        name="nsa_prompt",
    )(q, gn, ckp, cvp, ks, vs, kw, vw, eneg)


def _nsa_sample_kernel(pt_ref, q_ref, gn_ref, kcn_ref, vcn_ref, ksn_ref, vsn_ref, kwn_ref, vwn_ref, wk_ref, wv_ref,
                       selk_ref, selv_ref, eneg_ref, *rest, n_pages, past_len, ts, aliased):
    rest = rest[2:] if aliased else rest
    pages = rest[:4 * n_pages]
    o_ref, nwk_ref, nwv_ref = rest[4 * n_pages:4 * n_pages + 3]
    (kc_x, vc_x, ks_x, vs_x, tail_s, q_s, sc_s, g_s, o_s) = rest[4 * n_pages + 3:]
    scale = HEAD_DIM ** -0.5
    wl = wk_ref.shape[2]
    n_keys = ks_x.shape[1]
    n_slc = -(-(past_len + ts) // L_SLC)
    rows = GQA * N_KV * ts
    lane_t = lax.broadcasted_iota(jnp.int32, (ts, LANES), 1)

    def new_cols(j, new_ref):
        tail_s[j] = jnp.zeros(tail_s.shape[1:], F32)
        tail_s[j, 0:ts] = new_ref[0]
        return tail_s[j].T

    for j, (new_ref, dst) in enumerate(((kcn_ref, kc_x), (vcn_ref, vc_x), (ksn_ref, ks_x), (vsn_ref, vs_x))):
        for p in range(n_pages):
            dst[:, p * PAGE_SIZE:(p + 1) * PAGE_SIZE] = pages[j * n_pages + p][0].astype(BF16)
        dst[:, past_len:n_keys] = new_cols(j, new_ref).astype(BF16)

    kw_all = jnp.concatenate([wk_ref[0], new_cols(4, kwn_ref)], axis=1)
    vw_all = jnp.concatenate([wv_ref[0], new_cols(5, vwn_ref)], axis=1)
    nwk_ref[0] = kw_all[:, ts:ts + wl]
    nwv_ref[0] = vw_all[:, ts:ts + wl]

    for g in range(GQA):
        for k in range(N_KV):
            h = k * GQA + g
            tile = q_ref[0, :, (h // 2) * LANES:(h // 2 + 1) * LANES] * scale
            q_s[(g * N_KV + k) * ts:(g * N_KV + k + 1) * ts] = _head_rows(tile, h % 2, k, lane_t)
    qa = q_s[...].astype(BF16)

    def qpos(shape):
        return past_len + (lax.broadcasted_iota(jnp.int32, shape, 0) & (ts - 1))

    ck_t = jnp.dot(kc_x[...], selk_ref[...], preferred_element_type=F32).astype(BF16)
    cv_t = jnp.dot(vc_x[...], selv_ref[...], preferred_element_type=F32).astype(BF16)
    s_c = jnp.dot(qa, ck_t, preferred_element_type=F32)
    cl = lax.broadcasted_iota(jnp.int32, s_c.shape, 1)
    p_c = _masked_softmax(s_c, (cl + 1) * L_CMP - 1 <= qpos(s_c.shape))
    o_c = _dot_nt(p_c.astype(BF16), cv_t)
    grp = N_KV * ts
    ps = p_c[0:grp]
    for g in range(1, GQA):
        ps = ps + p_c[g * grp:(g + 1) * grp]
    imp = ps + pltpu.roll(ps, LANES - 1, 1)

    lane_g = lax.broadcasted_iota(jnp.int32, (grp, LANES), 1)
    blk = lane_g >> 1
    is_blk = ((lane_g & 1) == 0) & (blk < n_slc)
    cur = qpos(lane_g.shape) >> L_SLC_SHIFT
    valid_s = blk <= cur
    forced = (blk == 0) | (blk == cur) | (blk == cur - 1)
    score = jnp.where(valid_s, jnp.where(forced, FORCE_SCORE, imp), -1.0)
    sc_s[...] = jnp.zeros(sc_s.shape, F32)
    sc_s[0:grp] = jnp.where(is_blk, score, -jnp.inf)
    sel = _topk_mask(sc_s[...], min(N_SEL, n_slc))[0:grp]
    sel = jnp.where(is_blk & valid_s, sel, 0.0)
    notsel = (1.0 - sel).astype(BF16)
    q_aug = jnp.concatenate([qa, jnp.concatenate([notsel] * GQA, axis=0)], axis=1)

    k_aug = jnp.concatenate([ks_x[...], eneg_ref[...]], axis=0)
    s_s = jnp.dot(q_aug, k_aug, preferred_element_type=F32)
    kpos = lax.broadcasted_iota(jnp.int32, s_s.shape, 1)
    s_s = jnp.where(kpos <= qpos(s_s.shape), s_s, NEG)
    e = jnp.exp(s_s - jnp.max(s_s, axis=-1, keepdims=True))
    p_s = (e / jnp.sum(e, axis=-1, keepdims=True)).astype(BF16)
    o_sel = _dot_nt(p_s, vs_x[...])

    s_w = jnp.dot(qa, kw_all.astype(BF16), preferred_element_type=F32)
    kwpos = past_len - wl + lax.broadcasted_iota(jnp.int32, s_w.shape, 1)
    qp = qpos(s_w.shape)
    okw = (kwpos <= qp) & (kwpos > qp - WINDOW) & (kwpos >= 0)
    p_w = _masked_softmax(s_w, okw)
    o_w = _dot_nt(p_w.astype(BF16), vw_all.astype(BF16))

    sig = _sigmoid(gn_ref[0])
    o = None
    for j, ob in enumerate((o_c, o_sel, o_w)):
        for g in range(GQA):
            for k in range(N_KV):
                col = 3 * (k * GQA + g) + j
                g_s[(g * N_KV + k) * ts:(g * N_KV + k + 1) * ts] = jnp.broadcast_to(sig[:, col:col + 1], (ts, LANES))
        o = g_s[...] * ob if o is None else o + g_s[...] * ob
    o_s[...] = o
    for j in range(N_HEADS // 2):
        tile = None
        for h in (2 * j, 2 * j + 1):
            k, g = h // GQA, h % GQA
            piece = _head_rows(o_s[(g * N_KV + k) * ts:(g * N_KV + k + 1) * ts], k, h % 2, lane_t)
            tile = piece if tile is None else tile + piece
        o_ref[0, :, j * LANES:(j + 1) * LANES] = tile


def _nsa_sample(page_table, q, gn, new_rows, win_k, win_v, caches, selk, selv, eneg, l, new_win, *, n_pool, depth):
    bd, ts, _ = q.shape
    n_pages = page_table.shape[1]
    past_len = n_pages * PAGE_SIZE
    wl = win_k.shape[2]
    n_keys = past_len + LANES
    assert ts <= LANES and -(-(past_len + ts) // L_SLC) * 2 <= LANES and eneg.shape == (LANES, n_keys)
    rows = GQA * N_KV * ts
    tok = lambda w: pl.BlockSpec((1, ts, w), lambda b, pt: (b, 0, 0))
    win = pl.BlockSpec((1, KV_W, wl), lambda b, pt: (l * bd + b, 0, 0))
    sel = pl.BlockSpec((None, n_keys, LANES), lambda b, pt: (l, 0, 0))

    def page(p):
        return pl.BlockSpec((1, KV_W, PAGE_SIZE), lambda b, pt: (l * n_pool + pt[b * n_pages + p], 0, 0))

    aliased = new_win is not None
    in_specs = ([tok(N_HEADS * HEAD_DIM), tok(LANES)] + [tok(KV_W)] * 6 + [win, win, sel, sel,
                pl.BlockSpec((LANES, n_keys), lambda b, pt: (0, 0))]
                + ([pl.BlockSpec(memory_space=pl.ANY)] * 2 if aliased else [])
                + [page(p) for _ in range(4) for p in range(n_pages)])
    grid_spec = pltpu.PrefetchScalarGridSpec(
        num_scalar_prefetch=1,
        grid=(bd,),
        in_specs=in_specs,
        out_specs=[tok(N_HEADS * HEAD_DIM), win, win],
        scratch_shapes=[pltpu.VMEM((KV_W, n_keys), BF16)] * 4 + [
            pltpu.VMEM((6, LANES, KV_W), F32),
            pltpu.VMEM((rows, LANES), F32), pltpu.VMEM((LANES, LANES), F32),
            pltpu.VMEM((rows, LANES), F32), pltpu.VMEM((rows, LANES), F32),
        ],
    )
    cache_args = [c for c in caches for _ in range(n_pages)]
    win_shape = jax.ShapeDtypeStruct((depth * bd, KV_W, wl), F32)
    n_fixed = 1 + 2 + 6 + 5
    return pl.pallas_call(
        functools.partial(_nsa_sample_kernel, n_pages=n_pages, past_len=past_len, ts=ts, aliased=aliased),
        grid_spec=grid_spec,
        out_shape=[jax.ShapeDtypeStruct((bd, ts, N_HEADS * HEAD_DIM), F32), win_shape, win_shape],
        input_output_aliases={n_fixed: 1, n_fixed + 1: 2} if aliased else {},
        compiler_params=_params("arbitrary"),
        name="nsa_sample",
    )(page_table.reshape(-1), q, gn, *new_rows, win_k, win_v, selk, selv, eneg, *(new_win or ()), *cache_args)


def _merge_kernel(x_ref, sc_ref, sh_ref, gt_ref, g_ref, a_ref, b_ref, c_ref, wgm_ref, wp_ref, wc_ref, wa_ref,
                  wo_ref, o_ref):
    x = x_ref[0]
    d = x.shape[-1]
    h = _modnorm(x, g_ref[...], sc_ref[0], sh_ref[0]).astype(BF16)
    branches = (
        jnp.dot(a_ref[0].astype(BF16), wp_ref[...], preferred_element_type=F32),
        jnp.dot(b_ref[0].astype(BF16), wc_ref[...], preferred_element_type=F32),
        jnp.dot(c_ref[0].astype(BF16), wa_ref[...], preferred_element_type=F32),
    )
    merged = None
    for j, br in enumerate(branches):
        gm = _sigmoid(jnp.dot(h, wgm_ref[:, j * d:(j + 1) * d], preferred_element_type=F32))
        merged = gm * br if merged is None else merged + gm * br
    out = jnp.dot(merged.astype(BF16), wo_ref[...], preferred_element_type=F32)
    o_ref[0] = x + gt_ref[0] * out


def _mod_spec(tmod, tm, d):
    if tmod == 1:
        return pl.BlockSpec((1, 1, d), lambda bi, i: (bi, 0, 0))
    return pl.BlockSpec((1, tm, d), lambda bi, i: (bi, i, 0))


def _merge(x, sc, sh, gt, g, a, bo, c, wgm, wp, wc, wa, wo, l, *, tm):
    b, t, d = x.shape
    tm = min(tm, t)
    assert t % tm == 0
    ms = _mod_spec(sc.shape[1], tm, d)
    row = lambda w: pl.BlockSpec((1, tm, w), lambda bi, i: (bi, i, 0))
    wsp = lambda w: pl.BlockSpec((None,) + w.shape[1:], lambda bi, i: (l, 0, 0))
    return pl.pallas_call(
        _merge_kernel,
        grid=(b, t // tm),
        in_specs=[row(d), ms, ms, ms, pl.BlockSpec((None, 1, d), lambda bi, i: (l, 0, 0)),
                  row(a.shape[-1]), row(bo.shape[-1]), row(c.shape[-1]),
                  wsp(wgm), wsp(wp), wsp(wc), wsp(wa), wsp(wo)],
        out_specs=row(d),
        out_shape=jax.ShapeDtypeStruct((b, t, d), F32),
        compiler_params=_params("parallel", "parallel"),
        name="merge_out",
    )(x, sc, sh, gt, g, a, bo, c, wgm, wp, wc, wa, wo)


def _ffn_kernel(x_ref, sc_ref, sh_ref, gt_ref, g_ref, wgu_ref, wd_ref, gf_ref, *outs, d_ff, n_parts, final):
    x = x_ref[0]
    h = _modnorm(x, g_ref[...], sc_ref[0], sh_ref[0]).astype(BF16)
    part = d_ff // n_parts
    acc = None
    for c in range(n_parts):
        gp = jnp.dot(h, wgu_ref[:, c * part:(c + 1) * part], preferred_element_type=F32)
        up = jnp.dot(h, wgu_ref[:, d_ff + c * part:d_ff + (c + 1) * part], preferred_element_type=F32)
        act = (gp * _sigmoid(gp) * up).astype(BF16)
        dn = jnp.dot(act, wd_ref[c * part:(c + 1) * part, :], preferred_element_type=F32)
        acc = dn if acc is None else acc + dn
    y = x + gt_ref[0] * acc
    outs[0][0] = y
    if final:
        ms = jnp.mean(y * y, axis=-1, keepdims=True)
        outs[1][0] = y * lax.rsqrt(ms + EPS) * gf_ref[...]


def _ffn(x, sc, sh, gt, g, wgu, wd, gf, l, *, tm, final):
    b, t, d = x.shape
    d_ff = wd.shape[1]
    tm = min(tm, t)
    n_parts = 2
    assert t % tm == 0 and d_ff % (n_parts * LANES) == 0
    ms = _mod_spec(sc.shape[1], tm, d)
    row = pl.BlockSpec((1, tm, d), lambda bi, i: (bi, i, 0))
    n_out = 2 if final else 1
    return pl.pallas_call(
        functools.partial(_ffn_kernel, d_ff=d_ff, n_parts=n_parts, final=final),
        grid=(b, t // tm),
        in_specs=[row, ms, ms, ms, pl.BlockSpec((None, 1, d), lambda bi, i: (l, 0, 0)),
                  pl.BlockSpec((None, d, 2 * d_ff), lambda bi, i: (l, 0, 0)),
                  pl.BlockSpec((None, d_ff, d), lambda bi, i: (l, 0, 0)),
                  pl.BlockSpec((1, d), lambda bi, i: (0, 0))],
        out_specs=[row] * n_out,
        out_shape=[jax.ShapeDtypeStruct((b, t, d), F32)] * n_out,
        compiler_params=_params("parallel", "parallel"),
        name="ffn",
    )(x, sc, sh, gt, g, wgu, wd, gf)


def _rope_tables(pos):
    half = ROPE_DIM // 2
    inv = ROPE_THETA ** (-jnp.arange(half, dtype=F32) * 2.0 / ROPE_DIM)
    ang = pos.astype(F32)[:, None] * inv[None, :]
    cos, sin = jnp.cos(ang), jnp.sin(ang)
    n = pos.shape[0]
    rest = HEAD_DIM - ROPE_DIM
    c = jnp.concatenate([cos, cos, jnp.ones((n, rest), F32)], axis=1)
    sa = jnp.concatenate([-sin, jnp.zeros((n, half + rest), F32)], axis=1)
    sb = jnp.concatenate([jnp.zeros((n, half), F32), sin, jnp.zeros((n, rest), F32)], axis=1)
    rep = LANES // HEAD_DIM
    return tuple(jnp.tile(a, (1, rep)) for a in (c, sa, sb))


def _block_bias(n_keys, lane_stride):
    key_blk = (jnp.arange(n_keys) // L_SLC)[:, None] * lane_stride
    return jnp.where(key_blk == jnp.arange(LANES)[None, :], NEG, 0.0).astype(BF16)


def kernel(x_prompt, x_sample, cache_cmp_k, cache_cmp_v, cache_slc_k, cache_slc_v, state_win_k, state_win_v,
           state_pool, state_conv, page_table, c_prompt, c_sample, w_ada, b_ada, g_norm_mix, g_norm_ffn, w_in,
           w_pool, s_pool, w_pool_out, w_dw, b_dw, g_conv_ln, b_conv_ln, w_conv_out, w_cmp_k, w_cmp_v,
           w_attn_out, w_out, w_gu, w_down, g_final):
    depth, d, _ = w_in.shape
    bp, tp, _ = x_prompt.shape
    bd, ts, _ = x_sample.shape
    wp_dim = w_pool_out.shape[1]
    wc_dim = w_conv_out.shape[1]
    n_pool = cache_cmp_k.shape[1]
    n_pages = page_table.shape[1]
    past_len = n_pages * PAGE_SIZE
    wl = state_win_k.shape[2]
    n_groups, gw = w_pool.shape[1], w_pool.shape[2]
    assert gw == POOL_WINDOWS_GW and n_groups == len(POOL_WINDOWS) and wp_dim == n_groups * gw

    n_a = wp_dim + 2 * wc_dim + N_HEADS * HEAD_DIM + 6 * KV_W + 3 * N_HEADS
    n_a_pad = n_a - 3 * N_HEADS + LANES
    w_in_a = jnp.pad(w_in[:, :, :n_a], ((0, 0), (0, 0), (0, n_a_pad - n_a))).astype(BF16)
    w_gm = w_in[:, :, n_a:].astype(BF16)
    w_ada_b = w_ada.astype(BF16)
    eye = jnp.eye(n_groups, dtype=F32)
    wpool_bd = (w_pool[:, :, :, None, :] * eye[None, :, None, :, None]).reshape(depth, wp_dim, wp_dim).astype(BF16)
    wpo, wco, wao, wo = (w.astype(BF16) for w in (w_pool_out, w_conv_out, w_attn_out, w_out))
    wgu, wdn = w_gu.astype(BF16), w_down.astype(BF16)
    wck = jnp.broadcast_to(w_cmp_k[:, :, None], (depth, L_CMP, KV_W))
    wcv = jnp.broadcast_to(w_cmp_v[:, :, None], (depth, L_CMP, KV_W))
    vec3 = lambda a: a.reshape(depth, 1, -1)
    g_mix, g_ffn = vec3(g_norm_mix), vec3(g_norm_ffn)
    spool, bdw, gln, bln = vec3(s_pool), vec3(b_dw), vec3(g_conv_ln), vec3(b_conv_ln)
    gf = g_final.reshape(1, d)

    tabs_p = _rope_tables(jnp.arange(tp))
    tabs_s = _rope_tables(jnp.tile(past_len + jnp.arange(ts), bd))
    eneg_p = _block_bias(tp, 1)
    n_keys_s = past_len + LANES
    eneg_s = _block_bias(n_keys_s, 2).T
    key = jnp.arange(n_keys_s)
    blk_of_key = (key[:, None] // L_CMP == jnp.arange(LANES)[None, :]).astype(F32)
    selk = (w_cmp_k[:, key % L_CMP, None] * blk_of_key[None]).astype(BF16)
    selv = (w_cmp_v[:, key % L_CMP, None] * blk_of_key[None]).astype(BF16)

    ada = _ada_all(jnp.concatenate([c_prompt, c_sample], axis=0), w_ada_b, b_ada)
    pos_minor = lambda a: a.transpose(0, 1, 3, 4, 2).reshape(a.shape[0] * a.shape[1], KV_W, a.shape[2])
    caches = tuple(pos_minor(c) for c in (cache_cmp_k, cache_cmp_v, cache_slc_k, cache_slc_v))
    win_k_all, win_v_all = pos_minor(state_win_k), pos_minor(state_win_v)
    new_win = None

    xp = x_prompt
    xs = x_sample.reshape(1, bd * ts, d)
    st_p, st_s = [], []
    yp = ys = None
    for l in range(depth):
        final = l == depth - 1
        mp = [m[:, None, :] for m in jnp.split(ada[l, :bp], 6, axis=-1)]
        ms = [jnp.repeat(m, ts, axis=0)[None] for m in jnp.split(ada[l, bp:], 6, axis=-1)]

        (up, uc, q, kc, vc, ks, vs, kw, vw, gn, ck, cv) = _inproj(
            xp, mp[1], mp[0], g_mix, w_in_a, l, tabs_p, wck, wcv, w_pool=wp_dim, w_conv=wc_dim, with_cmp=True,
            q_dtype=BF16, tm=512)
        a_o, b_o, cst = _mix_prompt(up, uc, wpool_bd, spool, w_dw, bdw, gln, bln, l, tt=512)
        n_cmp = tp // L_CMP
        perm = lambda a: a.reshape(bp, n_cmp // 2, 2, KV_W).transpose(0, 2, 1, 3).reshape(bp, n_cmp, KV_W).astype(BF16)
        c_o = _nsa_prompt(q, gn, perm(ck), perm(cv), ks.astype(BF16), vs.astype(BF16), kw.astype(BF16),
                          vw.astype(BF16), eneg_p)
        xp = _merge(xp, mp[1], mp[0], mp[2], g_mix, a_o, b_o, c_o, w_gm, wpo, wco, wao, wo, l, tm=512)
        res = _ffn(xp, mp[4], mp[3], mp[5], g_ffn, wgu, wdn, gf, l, tm=512, final=final)
        xp = res[0]
        if final:
            yp = res[1]
        kv4 = lambda a: a.reshape(a.shape[0], a.shape[1], N_KV, HEAD_DIM)
        wlp = min(WINDOW, tp)
        st_p.append((kv4(kc), kv4(vc), kv4(ks), kv4(vs), kv4(kw[:, tp - wlp:]), kv4(vw[:, tp - wlp:]),
                     up[:, tp - (max(POOL_WINDOWS) - 1):], cst[:, HALO - (CONV_K - 1):]))

        (up, uc, q, kc, vc, ks, vs, kw, vw, gn) = _inproj(
            xs, ms[1], ms[0], g_mix, w_in_a, l, tabs_s, wck, wcv, w_pool=wp_dim, w_conv=wc_dim, with_cmp=False,
            q_dtype=F32, tm=512)
        tmaj = lambda a: a.reshape(bd, ts, a.shape[-1]).transpose(1, 0, 2)
        a_t, b_t, glu_t = _mix_sample(state_pool[l].transpose(1, 0, 2), tmaj(up), state_conv[l].transpose(1, 0, 2),
                                      tmaj(uc), wpool_bd, spool, w_dw, bdw, gln, bln, l)
        bmaj = lambda a: a.transpose(1, 0, 2).reshape(1, bd * ts, a.shape[-1])
        per_seq = lambda a: a.reshape(bd, ts, a.shape[-1])
        c_o, *new_win = _nsa_sample(page_table, per_seq(q), per_seq(gn),
                                    [per_seq(a) for a in (kc, vc, ks, vs, kw, vw)], win_k_all, win_v_all, caches,
                                    selk, selv, eneg_s, l, new_win, n_pool=n_pool, depth=depth)
        xs = _merge(xs, ms[1], ms[0], ms[2], g_mix, bmaj(a_t), bmaj(b_t), c_o.reshape(1, bd * ts, -1), w_gm, wpo,
                    wco, wao, wo, l, tm=512)
        res = _ffn(xs, ms[4], ms[3], ms[5], g_ffn, wgu, wdn, gf, l, tm=512, final=final)
        xs = res[0]
        if final:
            ys = res[1]
        kv4s = lambda a: a.reshape(bd, -1, N_KV, HEAD_DIM)
        new_pool = jnp.concatenate([state_pool[l][:, ts:], per_seq(up)], axis=1)
        new_conv = jnp.concatenate([state_conv[l][:, ts:], glu_t.transpose(1, 0, 2)], axis=1)
        st_s.append((kv4s(kc), kv4s(vc), kv4s(ks), kv4s(vs), None, None, new_pool, new_conv))

    win_out = [w.reshape(depth, bd, N_KV, HEAD_DIM, wl).transpose(0, 1, 4, 2, 3) for w in new_win]
    outs = [yp, ys.reshape(bd, ts, d)]
    for i in range(8):
        outs.append(jnp.stack([s[i] for s in st_p]))
        outs.append(win_out[i - 4] if i in (4, 5) else jnp.stack([s[i] for s in st_s]))
    return tuple(outs)
```

```python
import functools

import jax
import jax.numpy as jnp
from jax import lax
from jax.experimental import pallas as pl
from jax.experimental.pallas import tpu as pltpu

F32 = jnp.float32
BF16 = jnp.bfloat16

HEAD_DIM = 64
N_KV = 2
GQA = 4
N_HEADS = N_KV * GQA
ROPE_DIM = 16
ROPE_THETA = 500000.0
L_CMP = 32
L_SLC = 64
L_SLC_SHIFT = 6
N_SEL = 16
WINDOW = 512
Q_BLOCK = 128
PAGE_SIZE = 128
POOL_WINDOWS = (2, 4, 8, 16)
CONV_K = 31
EPS = 1e-6
NEG = -1e30
FORCE_SCORE = 1e4

LANES = 128
SUBLANES = 8
VMEM_LIMIT_BYTES = 56 * 1024 * 1024

NSA_CHUNK_KEYS = 512
NSA_ROW_TILE = 64
HALO = 32
KV_W = N_KV * HEAD_DIM


def _params(*sem):
    return pltpu.CompilerParams(dimension_semantics=sem, vmem_limit_bytes=VMEM_LIMIT_BYTES)


def _modnorm(x, g, sc, sh):
    ms = jnp.mean(x * x, axis=-1, keepdims=True)
    return (x * lax.rsqrt(ms + EPS) * g) * (1.0 + sc) + sh


def _sigmoid(x):
    return 1.0 / (1.0 + jnp.exp(-x))


def _dot_nt(a, b):
    return lax.dot_general(a, b, (((1,), (1,)), ((), ())), preferred_element_type=F32)


def _masked_softmax(s, mask):
    sm = jnp.where(mask, s, NEG)
    e = jnp.exp(sm - jnp.max(sm, axis=-1, keepdims=True))
    return jnp.where(mask, e / jnp.sum(e, axis=-1, keepdims=True), 0.0)


def _ada_kernel(c_ref, w_ref, b_ref, o_ref):
    c = c_ref[...]
    a = (c * _sigmoid(c)).astype(BF16)
    o_ref[...] = jnp.dot(a, w_ref[...], preferred_element_type=F32) + b_ref[...]


def _ada_all(c_all, w_ada, b_ada):
    depth, d, n = w_ada.shape
    tn = 1536
    assert n % tn == 0
    nb = c_all.shape[0]
    return pl.pallas_call(
        _ada_kernel,
        grid=(depth, n // tn),
        in_specs=[
            pl.BlockSpec((nb, d), lambda l, j: (0, 0)),
            pl.BlockSpec((None, d, tn), lambda l, j: (l, 0, j)),
            pl.BlockSpec((None, 1, tn), lambda l, j: (l, 0, j)),
        ],
        out_specs=pl.BlockSpec((None, nb, tn), lambda l, j: (l, 0, j)),
        out_shape=jax.ShapeDtypeStruct((depth, nb, n), F32),
        compiler_params=_params("parallel", "parallel"),
        name="ada_mod",
    )(c_all, w_ada, b_ada.reshape(depth, 1, n))


def _inproj_kernel(x_ref, sc_ref, sh_ref, g_ref, w_ref, cos_ref, sa_ref, sb_ref, wck_ref, wcv_ref, *outs,
                   w_pool, w_conv, with_cmp):
    (up_ref, uc_ref, q_ref, kc_ref, vc_ref, ks_ref, vs_ref, kw_ref, vw_ref, gn_ref) = outs[:10]
    h = _modnorm(x_ref[0], g_ref[...], sc_ref[0], sh_ref[0]).astype(BF16)
    y = jnp.dot(h, w_ref[...], preferred_element_type=F32)
    cos, sa, sb = cos_ref[...], sa_ref[...], sb_ref[...]

    def rope(t):
        return t * cos + pltpu.roll(t, LANES - ROPE_DIM // 2, 1) * sa + pltpu.roll(t, ROPE_DIM // 2, 1) * sb

    o = 0
    up_ref[0] = y[:, o:o + w_pool]
    o += w_pool
    uc_ref[0] = y[:, o:o + 2 * w_conv]
    o += 2 * w_conv
    for j in range(N_HEADS * HEAD_DIM // LANES):
        q_ref[0, :, j * LANES:(j + 1) * LANES] = rope(y[:, o:o + LANES]).astype(q_ref.dtype)
        o += LANES
    kc = rope(y[:, o:o + KV_W])
    vc = y[:, o + KV_W:o + 2 * KV_W]
    ks = rope(y[:, o + 2 * KV_W:o + 3 * KV_W])
    vs = y[:, o + 3 * KV_W:o + 4 * KV_W]
    kw = rope(y[:, o + 4 * KV_W:o + 5 * KV_W])
    vw = y[:, o + 5 * KV_W:o + 6 * KV_W]
    o += 6 * KV_W
    kc_ref[0], vc_ref[0], ks_ref[0], vs_ref[0], kw_ref[0], vw_ref[0] = kc, vc, ks, vs, kw, vw
    gn_ref[0] = y[:, o:o + LANES]
    if with_cmp:
        ck_ref, cv_ref = outs[10:12]
        tm = kc.shape[0]
        ck_ref[0] = jnp.sum(kc.reshape(tm // L_CMP, L_CMP, KV_W) * wck_ref[...][None], axis=1)
        cv_ref[0] = jnp.sum(vc.reshape(tm // L_CMP, L_CMP, KV_W) * wcv_ref[...][None], axis=1)
        for ref, val in zip(outs[12:], (ks, vs, kw, vw)):
            ref[0] = val.astype(BF16)


def _inproj(x, sc, sh, g, w_a, l, tabs, wck, wcv, *, w_pool, w_conv, with_cmp, q_dtype, tm):
    b, t, d = x.shape
    n = w_a.shape[-1]
    tm = min(tm, t)
    assert t % tm == 0 and (tm % L_CMP == 0 or not with_cmp)
    tmod = sc.shape[1]
    mod_spec = (pl.BlockSpec((1, 1, d), lambda bi, i: (bi, 0, 0)) if tmod == 1
                else pl.BlockSpec((1, tm, d), lambda bi, i: (bi, i, 0)))
    tab_spec = pl.BlockSpec((tm, LANES), lambda bi, i: (i, 0))
    row = lambda w: pl.BlockSpec((1, tm, w), lambda bi, i: (bi, i, 0))
    widths = [w_pool, 2 * w_conv, N_HEADS * HEAD_DIM] + [KV_W] * 6 + [LANES]
    dtypes = [F32, F32, q_dtype] + [F32] * 7
    out_specs = [row(w) for w in widths]
    out_shape = [jax.ShapeDtypeStruct((b, t, w), dt) for w, dt in zip(widths, dtypes)]
    if with_cmp:
        out_specs += [pl.BlockSpec((1, tm // L_CMP, KV_W), lambda bi, i: (bi, i, 0))] * 2 + [row(KV_W)] * 4
        out_shape += ([jax.ShapeDtypeStruct((b, t // L_CMP, KV_W), F32)] * 2
                      + [jax.ShapeDtypeStruct((b, t, KV_W), BF16)] * 4)
    return pl.pallas_call(
        functools.partial(_inproj_kernel, w_pool=w_pool, w_conv=w_conv, with_cmp=with_cmp),
        grid=(b, t // tm),
        in_specs=[
            row(d), mod_spec, mod_spec,
            pl.BlockSpec((None, 1, d), lambda bi, i: (l, 0, 0)),
            pl.BlockSpec((None, d, n), lambda bi, i: (l, 0, 0)),
            tab_spec, tab_spec, tab_spec,
            pl.BlockSpec((None, L_CMP, KV_W), lambda bi, i: (l, 0, 0)),
            pl.BlockSpec((None, L_CMP, KV_W), lambda bi, i: (l, 0, 0)),
        ],
        out_specs=out_specs,
        out_shape=out_shape,
        compiler_params=_params("parallel", "parallel"),
        name="in_proj",
    )(x, sc, sh, g, w_a, *tabs, wck, wcv)


def _pool_means(ext_ref, base, rows, pos0):
    lane = lax.broadcasted_iota(jnp.int32, (rows, LANES), 1)
    pos = lax.broadcasted_iota(jnp.int32, (rows, LANES), 0) + pos0
    low = lane < POOL_WINDOWS_GW
    means = []
    for tile in range(len(POOL_WINDOWS) // 2):
        cols = slice(tile * LANES, (tile + 1) * LANES)
        w_small, w_big = POOL_WINDOWS[2 * tile], POOL_WINDOWS[2 * tile + 1]
        acc = ext_ref[base:base + rows, cols]
        small = acc
        for s in range(1, w_big):
            acc = acc + ext_ref[base - s:base - s + rows, cols]
            if s == w_small - 1:
                small = acc
        cnt = jnp.minimum(jnp.where(low, w_small, w_big), pos + 1).astype(F32)
        means.append(jnp.where(low, small, acc) / cnt)
    return jnp.concatenate(means, axis=1)


POOL_WINDOWS_GW = 64


def _layernorm_silu(y, g, b):
    mu = jnp.mean(y, axis=-1, keepdims=True)
    yc = y - mu
    var = jnp.mean(yc * yc, axis=-1, keepdims=True)
    z = yc * lax.rsqrt(var + EPS) * g + b
    return z * _sigmoid(z)


def _mix_prompt_kernel(up_ref, uph_ref, uc_ref, uch_ref, wpool_ref, spool_ref, wdw_ref, bdw_ref, gln_ref, bln_ref,
                       a_ref, b_ref, cst_ref, pext_ref, cext_ref, *, tt, w_conv):
    i = pl.program_id(1)
    first = i == 0
    u = up_ref[0]
    pext_ref[0:HALO] = jnp.where(first, 0.0, uph_ref[0])
    pext_ref[HALO:HALO + tt] = u
    d = _pool_means(pext_ref, HALO, tt, i * tt) - u
    z = jnp.dot(d.astype(BF16), wpool_ref[...], preferred_element_type=F32) * spool_ref[...]
    a_ref[0] = z.astype(a_ref.dtype)
    hc = uch_ref[0]
    cext_ref[0:HALO] = jnp.where(first, 0.0, hc[:, :w_conv] * _sigmoid(hc[:, w_conv:]))
    uc = uc_ref[0]
    cext_ref[HALO:HALO + tt] = uc[:, :w_conv] * _sigmoid(uc[:, w_conv:])
    off = HALO - (CONV_K - 1)
    acc = wdw_ref[0:1, :] * cext_ref[off:off + tt]
    for k in range(1, CONV_K):
        acc = acc + wdw_ref[k:k + 1, :] * cext_ref[off + k:off + k + tt]
    b_ref[0] = _layernorm_silu(acc + bdw_ref[...], gln_ref[...], bln_ref[...]).astype(b_ref.dtype)
    cst_ref[0] = cext_ref[tt:tt + HALO]


def _mix_prompt(up, uc, wpool_bd, spool, wdw, bdw, gln, bln, l, *, tt):
    b, t, w_pool = up.shape
    w_conv = uc.shape[-1] // 2
    tt = min(tt, t)
    assert t % tt == 0 and tt % HALO == 0
    r = tt // HALO
    halo = lambda w: pl.BlockSpec((1, HALO, w), lambda bi, i: (bi, jnp.maximum(i * r - 1, 0), 0))
    vec = lambda w: pl.BlockSpec((None, 1, w), lambda bi, i: (l, 0, 0))
    return pl.pallas_call(
        functools.partial(_mix_prompt_kernel, tt=tt, w_conv=w_conv),
        grid=(b, t // tt),
        in_specs=[
            pl.BlockSpec((1, tt, w_pool), lambda bi, i: (bi, i, 0)), halo(w_pool),
            pl.BlockSpec((1, tt, 2 * w_conv), lambda bi, i: (bi, i, 0)), halo(2 * w_conv),
            pl.BlockSpec((None, w_pool, w_pool), lambda bi, i: (l, 0, 0)), vec(w_pool),
            pl.BlockSpec((None, CONV_K, w_conv), lambda bi, i: (l, 0, 0)), vec(w_conv), vec(w_conv), vec(w_conv),
        ],
        out_specs=[
            pl.BlockSpec((1, tt, w_pool), lambda bi, i: (bi, i, 0)),
            pl.BlockSpec((1, tt, w_conv), lambda bi, i: (bi, i, 0)),
            pl.BlockSpec((1, HALO, w_conv), lambda bi, i: (bi, 0, 0)),
        ],
        out_shape=[
            jax.ShapeDtypeStruct((b, t, w_pool), BF16),
            jax.ShapeDtypeStruct((b, t, w_conv), BF16),
            jax.ShapeDtypeStruct((b, HALO, w_conv), F32),
        ],
        scratch_shapes=[pltpu.VMEM((HALO + tt, w_pool), F32), pltpu.VMEM((HALO + tt, w_conv), F32)],
        compiler_params=_params("parallel", "arbitrary"),
        name="mix_prompt",
    )(up, up, uc, uc, wpool_bd, spool, wdw, bdw, gln, bln)


def _mix_sample_kernel(pst_ref, up_ref, cst_ref, uc_ref, wpool_ref, spool_ref, wdw_ref, bdw_ref, gln_ref, bln_ref,
                       a_ref, b_ref, glu_ref, *, w_conv):
    n_p, n_c, ts = pst_ref.shape[0], cst_ref.shape[0], up_ref.shape[0]
    prow = lambda r: pst_ref[r] if r < n_p else up_ref[r - n_p]
    low = lax.broadcasted_iota(jnp.int32, (up_ref.shape[1], LANES), 1) < POOL_WINDOWS_GW
    glu = []
    for t in range(ts):
        uc = uc_ref[t]
        glu.append(uc[:, :w_conv] * _sigmoid(uc[:, w_conv:]))
        glu_ref[t] = glu[t]
    crow = lambda r: cst_ref[r] if r < n_c else glu[r - n_c]
    for t in range(ts):
        u = up_ref[t]
        means = None
        acc = u
        sums = {1: acc}
        for s in range(1, max(POOL_WINDOWS)):
            acc = acc + prow(n_p + t - s)
            sums[s + 1] = acc
        tiles = []
        for tile in range(len(POOL_WINDOWS) // 2):
            cols = slice(tile * LANES, (tile + 1) * LANES)
            w_small, w_big = POOL_WINDOWS[2 * tile], POOL_WINDOWS[2 * tile + 1]
            tiles.append(jnp.where(low, sums[w_small][:, cols] / float(w_small),
                                   sums[w_big][:, cols] / float(w_big)))
        means = jnp.concatenate(tiles, axis=1)
        z = jnp.dot((means - u).astype(BF16), wpool_ref[...], preferred_element_type=F32) * spool_ref[...]
        a_ref[t] = z.astype(a_ref.dtype)
        acc = wdw_ref[0:1, :] * crow(t + n_c - (CONV_K - 1))
        for k in range(1, CONV_K):
            acc = acc + wdw_ref[k:k + 1, :] * crow(t + n_c - (CONV_K - 1) + k)
        b_ref[t] = _layernorm_silu(acc + bdw_ref[...], gln_ref[...], bln_ref[...]).astype(b_ref.dtype)


def _mix_sample(pst_t, up_t, cst_t, uc_t, wpool_bd, spool, wdw, bdw, gln, bln, l):
    ts, bd, w_pool = up_t.shape
    w_conv = uc_t.shape[-1] // 2
    full = lambda a: pl.BlockSpec(a.shape, lambda i: (0,) * a.ndim)
    vec = lambda w: pl.BlockSpec((None, 1, w), lambda i: (l, 0, 0))
    return pl.pallas_call(
        functools.partial(_mix_sample_kernel, w_conv=w_conv),
        grid=(1,),
        in_specs=[
            full(pst_t), full(up_t), full(cst_t), full(uc_t),
            pl.BlockSpec((None, w_pool, w_pool), lambda i: (l, 0, 0)), vec(w_pool),
            pl.BlockSpec((None, CONV_K, w_conv), lambda i: (l, 0, 0)), vec(w_conv), vec(w_conv), vec(w_conv),
        ],
        out_specs=[pl.BlockSpec((ts, bd, w_pool), lambda i: (0, 0, 0)),
                   pl.BlockSpec((ts, bd, w_conv), lambda i: (0, 0, 0)),
                   pl.BlockSpec((ts, bd, w_conv), lambda i: (0, 0, 0))],
        out_shape=[jax.ShapeDtypeStruct((ts, bd, w_pool), BF16),
                   jax.ShapeDtypeStruct((ts, bd, w_conv), BF16),
                   jax.ShapeDtypeStruct((ts, bd, w_conv), F32)],
        compiler_params=_params("arbitrary"),
        name="mix_sample",
    )(pst_t, up_t, cst_t, uc_t, wpool_bd, spool, wdw, bdw, gln, bln)


def _topk_masks(scores, n_sel):
    sts = tuple(s.T for s in scores)
    cand = lax.broadcasted_iota(jnp.int32, sts[0].shape, 0).astype(F32)

    def body(_, carry):
        out = []
        for st, sel in carry:
            m = jnp.max(st, axis=0, keepdims=True)
            first = jnp.min(jnp.where(st == m, cand, float(LANES)), axis=0, keepdims=True)
            hit = cand == first
            out.append((jnp.where(hit, -jnp.inf, st), jnp.where(hit, 1.0, sel)))
        return tuple(out)

    res = lax.fori_loop(0, n_sel, body, tuple((st, jnp.zeros(st.shape, F32)) for st in sts))
    return [sel.T for _, sel in res]


def _head_rows(tile, src_half, dst_half, lane):
    if src_half != dst_half:
        tile = pltpu.roll(tile, HEAD_DIM, 1)
    return jnp.where((lane >= dst_half * HEAD_DIM) & (lane < (dst_half + 1) * HEAD_DIM), tile, 0.0)


def _nsa_prompt_kernel(q_ref, gn_ref, ck_ref, cv_ref, ks_ref, vs_ref, kw_ref, vw_ref, eneg_ref, o_ref,
                       qa_scr, s_scr, p_scr, sw_scr, pw_scr, *, ck_keys):
    i = pl.program_id(1)
    qb = Q_BLOCK
    rows = GQA * qb
    scale = HEAD_DIM ** -0.5
    n_cmp = ck_ref.shape[1]
    n_slc = n_cmp // 2
    lane = lax.broadcasted_iota(jnp.int32, (qb, LANES), 1)

    def qpos(shape, row0=0):
        return i * qb + ((row0 + lax.broadcasted_iota(jnp.int32, shape, 0)) & (qb - 1))

    sig = _sigmoid(gn_ref[0])
    n_wchunks = WINDOW // qb + 1
    out_tiles = [None] * (N_HEADS // 2)
    qks, q_augs, o_cs, scores_s = [], [], [], []
    cur = qpos(lane.shape) >> L_SLC_SHIFT
    valid_s = lane <= cur
    forced = (lane == 0) | (lane == cur) | (lane == cur - 1)
    for k in range(N_KV):
        qh = []
        for g in range(GQA):
            h = k * GQA + g
            tile = q_ref[0, :, (h // 2) * LANES:(h // 2 + 1) * LANES].astype(F32) * scale
            qh.append(_head_rows(tile, h % 2, k, lane))
        qk = jnp.concatenate(qh, axis=0).astype(BF16)

        s_c = _dot_nt(qk, ck_ref[0])
        cl = lax.broadcasted_iota(jnp.int32, (rows, n_cmp), 1)
        c_end = jnp.where(cl < n_slc, cl * L_SLC + L_CMP - 1, (cl - n_slc) * L_SLC + L_SLC - 1)
        p_c = _masked_softmax(s_c, c_end <= qpos(s_c.shape))
        o_c = jnp.dot(p_c.astype(BF16), cv_ref[0], preferred_element_type=F32)
        ps = p_c[0:qb]
        for g in range(1, GQA):
            ps = ps + p_c[g * qb:(g + 1) * qb]
        imp = ps[:, :n_slc] + ps[:, n_slc:]
        if n_slc < LANES:
            imp = jnp.concatenate([imp, jnp.zeros((qb, LANES - n_slc), F32)], axis=1)

        score = jnp.where(valid_s, jnp.where(forced, FORCE_SCORE, imp), -1.0)
        scores_s.append(jnp.where(lane < n_slc, score, -jnp.inf))
        qks.append(qk)
        o_cs.append(o_c)

    for k, sel in enumerate(_topk_masks(scores_s, min(N_SEL, n_slc))):
        notsel = (1.0 - jnp.where(valid_s, sel, 0.0)).astype(BF16)
        q_augs.append(jnp.concatenate([qks[k], jnp.concatenate([notsel] * GQA, axis=0)], axis=1))

    for k in range(N_KV):
        qa_scr[k] = q_augs[k]
    own = lax.broadcasted_iota(jnp.int32, (ck_keys, LANES), 1) < HEAD_DIM

    def scores(c, buf):
        start = pl.multiple_of(c * ck_keys, ck_keys)
        k_aug = jnp.concatenate([ks_ref[0, pl.ds(start, ck_keys), :], eneg_ref[pl.ds(start, ck_keys), :]], axis=1)
        for k in range(N_KV):
            s_scr[buf, k] = _dot_nt(qa_scr[k], k_aug)

    def probs(c, buf, ms, causal):
        new_m, alphas = [], []
        for k in range(N_KV):
            m_tiles = []
            for r in range(0, rows, NSA_ROW_TILE):
                rs = slice(r, r + NSA_ROW_TILE)
                s = s_scr[buf, k, rs]
                if causal:
                    kpos = c * ck_keys + lax.broadcasted_iota(jnp.int32, s.shape, 1)
                    s = jnp.where(kpos <= qpos(s.shape, r), s, NEG)
                m_new = jnp.maximum(ms[k][rs], jnp.max(s, axis=-1, keepdims=True))
                p_scr[buf, k, rs] = jnp.exp(s - m_new).astype(BF16)
                m_tiles.append(m_new)
            m_new = jnp.concatenate(m_tiles, axis=0)
            alphas.append(jnp.exp(ms[k] - m_new))
            new_m.append(m_new)
        return new_m, alphas

    def values(c, buf, alphas, accs):
        start = pl.multiple_of(c * ck_keys, ck_keys)
        v = vs_ref[0, pl.ds(start, ck_keys), :]
        v_one = [jnp.where(own, v, 1.0), jnp.where(own, 1.0, v)]
        return [alphas[k] * accs[k] + jnp.dot(p_scr[buf, k], v_one[k], preferred_element_type=F32)
                for k in range(N_KV)]

    def pair(c, carry, causal, last):
        ms, alphas, accs = carry
        accs = values(jnp.maximum(c - 1, 0), 1, alphas, accs)
        ms, alphas = probs(c, 0, ms, causal)
        scores(c + 1, 1)
        accs = values(c, 0, alphas, accs)
        ms, alphas = probs(c + 1, 1, ms, causal)
        if last:
            accs = values(c + 1, 1, alphas, accs)
        else:
            scores(c + 2, 0)
        return ms, alphas, accs

    n_pairs = (i * qb + qb + 2 * ck_keys - 1) // (2 * ck_keys)
    p_scr[1] = jnp.zeros(p_scr.shape[1:], BF16)
    scores(0, 0)
    carry = ([jnp.full((rows, 1), -jnp.inf, F32)] * N_KV, [jnp.ones((rows, 1), F32)] * N_KV,
             [jnp.zeros((rows, LANES), F32)] * N_KV)
    carry = lax.fori_loop(0, n_pairs - 1, lambda j, cr: pair(2 * j, cr, False, False), carry)
    _, _, accs = pair(2 * (n_pairs - 1), carry, True, True)
    o_ss = [acc / pltpu.roll(acc, HEAD_DIM, 1) for acc in accs]

    kws, vws = [], []
    for dj in range(n_wchunks):
        st = pl.multiple_of(jnp.maximum(i - (n_wchunks - 1) + dj, 0) * qb, qb)
        kws.append(kw_ref[0, pl.ds(st, qb), :])
        vws.append(vw_ref[0, pl.ds(st, qb), :])
    kw_blk, vw_blk = jnp.concatenate(kws, axis=0), jnp.concatenate(vws, axis=0)
    wk = kw_blk.shape[0]
    kwpos = (i - (n_wchunks - 1)) * qb + lax.broadcasted_iota(jnp.int32, (qb, wk), 1)
    qp = qpos((qb, wk))
    wbias = jnp.where((kwpos <= qp) & (kwpos > qp - WINDOW) & (kwpos >= 0), 0.0, NEG)
    own_w = lax.broadcasted_iota(jnp.int32, (wk, LANES), 1) < HEAD_DIM
    vw_one = [jnp.where(own_w, vw_blk, 1.0), jnp.where(own_w, 1.0, vw_blk)]
    for k in range(N_KV):
        sw_scr[k] = _dot_nt(qks[k], kw_blk)
    o_ws = []
    for k in range(N_KV):
        for r in range(0, rows, NSA_ROW_TILE):
            rs = slice(r, r + NSA_ROW_TILE)
            s = sw_scr[k, rs] + wbias[r % qb:r % qb + NSA_ROW_TILE]
            pw_scr[k, rs] = jnp.exp(s - jnp.max(s, axis=-1, keepdims=True)).astype(BF16)
        acc = jnp.dot(pw_scr[k], vw_one[k], preferred_element_type=F32)
        o_ws.append(acc / pltpu.roll(acc, HEAD_DIM, 1))

    for k in range(N_KV):
        o_c, o_s, o_w = o_cs[k], o_ss[k], o_ws[k]
        for g in range(GQA):
            h = k * GQA + g
            col = 3 * h
            r = slice(g * qb, (g + 1) * qb)
            o = sig[:, col:col + 1] * o_c[r] + sig[:, col + 1:col + 2] * o_s[r] + sig[:, col + 2:col + 3] * o_w[r]
            o = _head_rows(o, k, h % 2, lane)
            out_tiles[h // 2] = o if out_tiles[h // 2] is None else out_tiles[h // 2] + o
    for j, tile in enumerate(out_tiles):
        o_ref[0, :, j * LANES:(j + 1) * LANES] = tile.astype(o_ref.dtype)


def _nsa_prompt(q, gn, ckp, cvp, ks, vs, kw, vw, eneg):
    b, t, _ = q.shape
    assert t % Q_BLOCK == 0 and t // L_SLC <= LANES
    ck_keys = min(NSA_CHUNK_KEYS, t // 2)
    assert t % (2 * ck_keys) == 0
    rows = GQA * Q_BLOCK
    qspec = lambda w: pl.BlockSpec((1, Q_BLOCK, w), lambda bi, i: (bi, i, 0))
    seq = lambda n: pl.BlockSpec((1, n, KV_W), lambda bi, i: (bi, 0, 0))
    return pl.pallas_call(
        functools.partial(_nsa_prompt_kernel, ck_keys=ck_keys),
        scratch_shapes=[pltpu.VMEM((N_KV, rows, 2 * LANES), BF16), pltpu.VMEM((2, N_KV, rows, ck_keys), F32),
                        pltpu.VMEM((2, N_KV, rows, ck_keys), BF16),
                        pltpu.VMEM((N_KV, rows, WINDOW + Q_BLOCK), F32),
                        pltpu.VMEM((N_KV, rows, WINDOW + Q_BLOCK), BF16)],
        grid=(b, t // Q_BLOCK),
        in_specs=[qspec(N_HEADS * HEAD_DIM), qspec(LANES), seq(t // L_CMP), seq(t // L_CMP),
                  seq(t), seq(t), seq(t), seq(t), pl.BlockSpec((t, LANES), lambda bi, i: (0, 0))],
        out_specs=qspec(N_HEADS * HEAD_DIM),
        out_shape=jax.ShapeDtypeStruct((b, t, N_HEADS * HEAD_DIM), BF16),
        compiler_params=_params("parallel", "parallel"),
        name="nsa_prompt",
    )(q, gn, ckp, cvp, ks, vs, kw, vw, eneg)


def _nsa_sample_kernel(pt_ref, q_ref, gn_ref, kcn_ref, vcn_ref, ksn_ref, vsn_ref, kwn_ref, vwn_ref, wk_ref, wv_ref,
                       selk_ref, selv_ref, eneg_ref, *rest, n_pages, past_len, ts, page0):
    caches = rest[2:6]
    o_ref, nwk_ref, nwv_ref = rest[6:9]
    (past_buf, sems, tail_s, q_s, sc_s, g_s, o_s) = rest[9:]
    b = pl.program_id(0)
    slot = b & 1
    scale = HEAD_DIM ** -0.5
    wl = wk_ref.shape[2]
    n_keys = past_len + LANES
    n_slc = -(-(past_len + ts) // L_SLC)
    rows = GQA * N_KV * ts
    lane_t = lax.broadcasted_iota(jnp.int32, (ts, LANES), 1)

    def page_copy(seq, dst_slot, j, p):
        src = caches[j].at[page0 + pt_ref[seq * n_pages + p]]
        dst = past_buf.at[dst_slot, j, :, pl.ds(p * PAGE_SIZE, PAGE_SIZE)]
        return pltpu.make_async_copy(src, dst, sems.at[dst_slot])

    def all_pages(seq, dst_slot, op):
        for j in range(4):
            for p in range(n_pages):
                op(page_copy(seq, dst_slot, j, p))

    @pl.when(b == 0)
    def _():
        all_pages(0, 0, lambda cp: cp.start())

    @pl.when(b + 1 < pl.num_programs(0))
    def _():
        all_pages(b + 1, 1 - slot, lambda cp: cp.start())

    all_pages(b, slot, lambda cp: cp.wait())

    def new_cols(j, new_ref):
        tail_s[j] = jnp.zeros(tail_s.shape[1:], F32)
        tail_s[j, 0:ts] = new_ref[0]
        return tail_s[j].T

    kc_p, vc_p, ks_p, vs_p = (past_buf[slot, j].astype(BF16) for j in range(4))
    kc_n, vc_n, ks_n, vs_n = (new_cols(j, r).astype(BF16)
                              for j, r in enumerate((kcn_ref, vcn_ref, ksn_ref, vsn_ref)))

    kw_all = jnp.concatenate([wk_ref[0], new_cols(4, kwn_ref)], axis=1)
    vw_all = jnp.concatenate([wv_ref[0], new_cols(5, vwn_ref)], axis=1)
    nwk_ref[0] = kw_all[:, ts:ts + wl]
    nwv_ref[0] = vw_all[:, ts:ts + wl]

    for g in range(GQA):
        for k in range(N_KV):
            h = k * GQA + g
            tile = q_ref[0, :, (h // 2) * LANES:(h // 2 + 1) * LANES] * scale
            q_s[(g * N_KV + k) * ts:(g * N_KV + k + 1) * ts] = _head_rows(tile, h % 2, k, lane_t)
    qa = q_s[...].astype(BF16)

    def qpos(shape):
        return past_len + (lax.broadcasted_iota(jnp.int32, shape, 0) & (ts - 1))

    def compress(past, new, sel_ref):
        return (jnp.dot(past, sel_ref[0:past_len], preferred_element_type=F32)
                + jnp.dot(new, sel_ref[past_len:n_keys], preferred_element_type=F32)).astype(BF16)

    ck_t = compress(kc_p, kc_n, selk_ref)
    cv_t = compress(vc_p, vc_n, selv_ref)
    s_c = jnp.dot(qa, ck_t, preferred_element_type=F32)
    cl = lax.broadcasted_iota(jnp.int32, s_c.shape, 1)
    p_c = _masked_softmax(s_c, (cl + 1) * L_CMP - 1 <= qpos(s_c.shape))
    o_c = _dot_nt(p_c.astype(BF16), cv_t)
    grp = N_KV * ts
    ps = p_c[0:grp]
    for g in range(1, GQA):
        ps = ps + p_c[g * grp:(g + 1) * grp]
    imp = ps + pltpu.roll(ps, LANES - 1, 1)

    lane_g = lax.broadcasted_iota(jnp.int32, (grp, LANES), 1)
    blk = lane_g >> 1
    is_blk = ((lane_g & 1) == 0) & (blk < n_slc)
    cur = qpos(lane_g.shape) >> L_SLC_SHIFT
    valid_s = blk <= cur
    forced = (blk == 0) | (blk == cur) | (blk == cur - 1)
    score = jnp.where(valid_s, jnp.where(forced, FORCE_SCORE, imp), -1.0)
    sc_s[...] = jnp.zeros(sc_s.shape, F32)
    sc_s[0:grp] = jnp.where(is_blk, score, -jnp.inf)
    sel = _topk_masks([sc_s[...]], min(N_SEL, n_slc))[0][0:grp]
    sel = jnp.where(is_blk & valid_s, sel, 0.0)
    notsel = (1.0 - sel).astype(BF16)
    q_aug = jnp.concatenate([qa, jnp.concatenate([notsel] * GQA, axis=0)], axis=1)

    s_s = jnp.concatenate(
        [jnp.dot(q_aug, jnp.concatenate([ks_p, eneg_ref[:, 0:past_len]], axis=0), preferred_element_type=F32),
         jnp.dot(q_aug, jnp.concatenate([ks_n, eneg_ref[:, past_len:n_keys]], axis=0), preferred_element_type=F32)],
        axis=1)
    kpos = lax.broadcasted_iota(jnp.int32, s_s.shape, 1)
    s_s = jnp.where(kpos <= qpos(s_s.shape), s_s, NEG)
    e = jnp.exp(s_s - jnp.max(s_s, axis=-1, keepdims=True))
    p_s = (e / jnp.sum(e, axis=-1, keepdims=True)).astype(BF16)
    o_sel = _dot_nt(p_s[:, 0:past_len], vs_p) + _dot_nt(p_s[:, past_len:n_keys], vs_n)

    s_w = jnp.dot(qa, kw_all.astype(BF16), preferred_element_type=F32)
    kwpos = past_len - wl + lax.broadcasted_iota(jnp.int32, s_w.shape, 1)
    qp = qpos(s_w.shape)
    okw = (kwpos <= qp) & (kwpos > qp - WINDOW) & (kwpos >= 0)
    p_w = _masked_softmax(s_w, okw)
    o_w = _dot_nt(p_w.astype(BF16), vw_all.astype(BF16))

    sig = _sigmoid(gn_ref[0])
    o = None
    for j, ob in enumerate((o_c, o_sel, o_w)):
        for g in range(GQA):
            for k in range(N_KV):
                col = 3 * (k * GQA + g) + j
                g_s[(g * N_KV + k) * ts:(g * N_KV + k + 1) * ts] = jnp.broadcast_to(sig[:, col:col + 1], (ts, LANES))
        o = g_s[...] * ob if o is None else o + g_s[...] * ob
    o_s[...] = o
    for j in range(N_HEADS // 2):
        tile = None
        for h in (2 * j, 2 * j + 1):
            k, g = h // GQA, h % GQA
            piece = _head_rows(o_s[(g * N_KV + k) * ts:(g * N_KV + k + 1) * ts], k, h % 2, lane_t)
            tile = piece if tile is None else tile + piece
        o_ref[0, :, j * LANES:(j + 1) * LANES] = tile


def _nsa_sample(page_table, q, gn, new_rows, win_k, win_v, caches, selk, selv, eneg, l, new_win, *, n_pool, depth):
    bd, ts, _ = q.shape
    n_pages = page_table.shape[1]
    past_len = n_pages * PAGE_SIZE
    wl = win_k.shape[2]
    n_keys = past_len + LANES
    assert ts <= LANES and -(-(past_len + ts) // L_SLC) * 2 <= LANES and eneg.shape == (LANES, n_keys)
    rows = GQA * N_KV * ts
    tok = lambda w: pl.BlockSpec((1, ts, w), lambda b, pt: (b, 0, 0))
    win = pl.BlockSpec((1, KV_W, wl), lambda b, pt: (l * bd + b, 0, 0))
    sel = pl.BlockSpec((None, n_keys, LANES), lambda b, pt: (l, 0, 0))

    hbm = pl.BlockSpec(memory_space=pl.ANY)
    in_specs = ([tok(N_HEADS * HEAD_DIM), tok(LANES)] + [tok(KV_W)] * 6 + [win, win, sel, sel,
                pl.BlockSpec((LANES, n_keys), lambda b, pt: (0, 0))] + [hbm] * 6)
    grid_spec = pltpu.PrefetchScalarGridSpec(
        num_scalar_prefetch=1,
        grid=(bd,),
        in_specs=in_specs,
        out_specs=[tok(N_HEADS * HEAD_DIM), win, win],
        scratch_shapes=[
            pltpu.VMEM((2, 4, KV_W, past_len), F32),
            pltpu.SemaphoreType.DMA((2,)),
            pltpu.VMEM((6, LANES, KV_W), F32),
            pltpu.VMEM((rows, LANES), F32), pltpu.VMEM((LANES, LANES), F32),
            pltpu.VMEM((rows, LANES), F32), pltpu.VMEM((rows, LANES), F32),
        ],
    )
    win_shape = jax.ShapeDtypeStruct((depth * bd, KV_W, wl), F32)
    n_fixed = 1 + 2 + 6 + 5
    return pl.pallas_call(
        functools.partial(_nsa_sample_kernel, n_pages=n_pages, past_len=past_len, ts=ts, page0=l * n_pool),
        grid_spec=grid_spec,
        out_shape=[jax.ShapeDtypeStruct((bd, ts, N_HEADS * HEAD_DIM), F32), win_shape, win_shape],
        input_output_aliases={n_fixed: 1, n_fixed + 1: 2},
        compiler_params=_params("arbitrary"),
        name="nsa_sample",
    )(page_table.reshape(-1), q, gn, *new_rows, win_k, win_v, selk, selv, eneg, *new_win, *caches)


def _merge_kernel(x_ref, sc_ref, sh_ref, gt_ref, g_ref, a_ref, b_ref, c_ref, wgm_ref, wp_ref, wc_ref, wa_ref,
                  wo_ref, o_ref):
    x = x_ref[0]
    d = x.shape[-1]
    h = _modnorm(x, g_ref[...], sc_ref[0], sh_ref[0]).astype(BF16)
    branches = (
        jnp.dot(a_ref[0].astype(BF16), wp_ref[...], preferred_element_type=F32),
        jnp.dot(b_ref[0].astype(BF16), wc_ref[...], preferred_element_type=F32),
        jnp.dot(c_ref[0].astype(BF16), wa_ref[...], preferred_element_type=F32),
    )
    merged = None
    for j, br in enumerate(branches):
        gm = _sigmoid(jnp.dot(h, wgm_ref[:, j * d:(j + 1) * d], preferred_element_type=F32))
        merged = gm * br if merged is None else merged + gm * br
    out = jnp.dot(merged.astype(BF16), wo_ref[...], preferred_element_type=F32)
    o_ref[0] = x + gt_ref[0] * out


def _mod_spec(tmod, tm, d):
    if tmod == 1:
        return pl.BlockSpec((1, 1, d), lambda bi, i: (bi, 0, 0))
    return pl.BlockSpec((1, tm, d), lambda bi, i: (bi, i, 0))


def _merge(x, sc, sh, gt, g, a, bo, c, wgm, wp, wc, wa, wo, l, *, tm):
    b, t, d = x.shape
    tm = min(tm, t)
    assert t % tm == 0
    ms = _mod_spec(sc.shape[1], tm, d)
    row = lambda w: pl.BlockSpec((1, tm, w), lambda bi, i: (bi, i, 0))
    wsp = lambda w: pl.BlockSpec((None,) + w.shape[1:], lambda bi, i: (l, 0, 0))
    return pl.pallas_call(
        _merge_kernel,
        grid=(b, t // tm),
        in_specs=[row(d), ms, ms, ms, pl.BlockSpec((None, 1, d), lambda bi, i: (l, 0, 0)),
                  row(a.shape[-1]), row(bo.shape[-1]), row(c.shape[-1]),
                  wsp(wgm), wsp(wp), wsp(wc), wsp(wa), wsp(wo)],
        out_specs=row(d),
        out_shape=jax.ShapeDtypeStruct((b, t, d), F32),
        compiler_params=_params("parallel", "parallel"),
        name="merge_out",
    )(x, sc, sh, gt, g, a, bo, c, wgm, wp, wc, wa, wo)


def _ffn_kernel(x_ref, sc_ref, sh_ref, gt_ref, g_ref, wgu_ref, wd_ref, gf_ref, *outs, d_ff, n_parts, final):
    x = x_ref[0]
    h = _modnorm(x, g_ref[...], sc_ref[0], sh_ref[0]).astype(BF16)
    part = d_ff // n_parts
    acc = None
    for c in range(n_parts):
        gp = jnp.dot(h, wgu_ref[:, c * part:(c + 1) * part], preferred_element_type=F32)
        up = jnp.dot(h, wgu_ref[:, d_ff + c * part:d_ff + (c + 1) * part], preferred_element_type=F32)
        act = (gp * _sigmoid(gp) * up).astype(BF16)
        dn = jnp.dot(act, wd_ref[c * part:(c + 1) * part, :], preferred_element_type=F32)
        acc = dn if acc is None else acc + dn
    y = x + gt_ref[0] * acc
    outs[0][0] = y
    if final:
        ms = jnp.mean(y * y, axis=-1, keepdims=True)
        outs[1][0] = y * lax.rsqrt(ms + EPS) * gf_ref[...]


def _ffn(x, sc, sh, gt, g, wgu, wd, gf, l, *, tm, final):
    b, t, d = x.shape
    d_ff = wd.shape[1]
    tm = min(tm, t)
    n_parts = 2
    assert t % tm == 0 and d_ff % (n_parts * LANES) == 0
    ms = _mod_spec(sc.shape[1], tm, d)
    row = pl.BlockSpec((1, tm, d), lambda bi, i: (bi, i, 0))
    n_out = 2 if final else 1
    return pl.pallas_call(
        functools.partial(_ffn_kernel, d_ff=d_ff, n_parts=n_parts, final=final),
        grid=(b, t // tm),
        in_specs=[row, ms, ms, ms, pl.BlockSpec((None, 1, d), lambda bi, i: (l, 0, 0)),
                  pl.BlockSpec((None, d, 2 * d_ff), lambda bi, i: (l, 0, 0)),
                  pl.BlockSpec((None, d_ff, d), lambda bi, i: (l, 0, 0)),
                  pl.BlockSpec((1, d), lambda bi, i: (0, 0))],
        out_specs=[row] * n_out,
        out_shape=[jax.ShapeDtypeStruct((b, t, d), F32)] * n_out,
        compiler_params=_params("parallel", "parallel"),
        name="ffn",
    )(x, sc, sh, gt, g, wgu, wd, gf)


def _rope_tables(pos):
    half = ROPE_DIM // 2
    inv = ROPE_THETA ** (-jnp.arange(half, dtype=F32) * 2.0 / ROPE_DIM)
    ang = pos.astype(F32)[:, None] * inv[None, :]
    cos, sin = jnp.cos(ang), jnp.sin(ang)
    n = pos.shape[0]
    rest = HEAD_DIM - ROPE_DIM
    c = jnp.concatenate([cos, cos, jnp.ones((n, rest), F32)], axis=1)
    sa = jnp.concatenate([-sin, jnp.zeros((n, half + rest), F32)], axis=1)
    sb = jnp.concatenate([jnp.zeros((n, half), F32), sin, jnp.zeros((n, rest), F32)], axis=1)
    rep = LANES // HEAD_DIM
    return tuple(jnp.tile(a, (1, rep)) for a in (c, sa, sb))


def _block_bias(n_keys, lane_stride):
    key_blk = (jnp.arange(n_keys) // L_SLC)[:, None] * lane_stride
    return jnp.where(key_blk == jnp.arange(LANES)[None, :], NEG, 0.0).astype(BF16)


def kernel(x_prompt, x_sample, cache_cmp_k, cache_cmp_v, cache_slc_k, cache_slc_v, state_win_k, state_win_v,
           state_pool, state_conv, page_table, c_prompt, c_sample, w_ada, b_ada, g_norm_mix, g_norm_ffn, w_in,
           w_pool, s_pool, w_pool_out, w_dw, b_dw, g_conv_ln, b_conv_ln, w_conv_out, w_cmp_k, w_cmp_v,
           w_attn_out, w_out, w_gu, w_down, g_final):
    depth, d, _ = w_in.shape
    bp, tp, _ = x_prompt.shape
    bd, ts, _ = x_sample.shape
    wp_dim = w_pool_out.shape[1]
    wc_dim = w_conv_out.shape[1]
    n_pool = cache_cmp_k.shape[1]
    n_pages = page_table.shape[1]
    past_len = n_pages * PAGE_SIZE
    wl = state_win_k.shape[2]
    n_groups, gw = w_pool.shape[1], w_pool.shape[2]
    assert gw == POOL_WINDOWS_GW and n_groups == len(POOL_WINDOWS) and wp_dim == n_groups * gw

    n_a = wp_dim + 2 * wc_dim + N_HEADS * HEAD_DIM + 6 * KV_W + 3 * N_HEADS
    n_a_pad = n_a - 3 * N_HEADS + LANES
    w_in_a = jnp.pad(w_in[:, :, :n_a], ((0, 0), (0, 0), (0, n_a_pad - n_a))).astype(BF16)
    w_gm = w_in[:, :, n_a:].astype(BF16)
    w_ada_b = w_ada.astype(BF16)
    eye = jnp.eye(n_groups, dtype=F32)
    wpool_bd = (w_pool[:, :, :, None, :] * eye[None, :, None, :, None]).reshape(depth, wp_dim, wp_dim).astype(BF16)
    wpo, wco, wao, wo = (w.astype(BF16) for w in (w_pool_out, w_conv_out, w_attn_out, w_out))
    wgu, wdn = w_gu.astype(BF16), w_down.astype(BF16)
    wck = jnp.broadcast_to(w_cmp_k[:, :, None], (depth, L_CMP, KV_W))
    wcv = jnp.broadcast_to(w_cmp_v[:, :, None], (depth, L_CMP, KV_W))
    vec3 = lambda a: a.reshape(depth, 1, -1)
    g_mix, g_ffn = vec3(g_norm_mix), vec3(g_norm_ffn)
    spool, bdw, gln, bln = vec3(s_pool), vec3(b_dw), vec3(g_conv_ln), vec3(b_conv_ln)
    gf = g_final.reshape(1, d)

    tabs_p = _rope_tables(jnp.arange(tp))
    tabs_s = _rope_tables(jnp.tile(past_len + jnp.arange(ts), bd))
    eneg_p = _block_bias(tp, 1)
    n_keys_s = past_len + LANES
    eneg_s = _block_bias(n_keys_s, 2).T
    key = jnp.arange(n_keys_s)
    blk_of_key = (key[:, None] // L_CMP == jnp.arange(LANES)[None, :]).astype(F32)
    selk = (w_cmp_k[:, key % L_CMP, None] * blk_of_key[None]).astype(BF16)
    selv = (w_cmp_v[:, key % L_CMP, None] * blk_of_key[None]).astype(BF16)

    ada = _ada_all(jnp.concatenate([c_prompt, c_sample], axis=0), w_ada_b, b_ada)
    pos_minor = lambda a: a.transpose(0, 1, 3, 4, 2).reshape(a.shape[0] * a.shape[1], KV_W, a.shape[2])
    caches = tuple(pos_minor(c) for c in (cache_cmp_k, cache_cmp_v, cache_slc_k, cache_slc_v))
    win_k_all, win_v_all = pos_minor(state_win_k), pos_minor(state_win_v)
    new_win = [jnp.zeros(win_k_all.shape, F32), jnp.zeros(win_v_all.shape, F32)]

    xp = x_prompt
    xs = x_sample.reshape(1, bd * ts, d)
    st_p, st_s = [], []
    yp = ys = None
    for l in range(depth):
        final = l == depth - 1
        mp = [m[:, None, :] for m in jnp.split(ada[l, :bp], 6, axis=-1)]
        ms = [jnp.repeat(m, ts, axis=0)[None] for m in jnp.split(ada[l, bp:], 6, axis=-1)]

        (up, uc, q, kc, vc, ks, vs, kw, vw, gn, ck, cv, ks_b, vs_b, kw_b, vw_b) = _inproj(
            xp, mp[1], mp[0], g_mix, w_in_a, l, tabs_p, wck, wcv, w_pool=wp_dim, w_conv=wc_dim, with_cmp=True,
            q_dtype=BF16, tm=512)
        a_o, b_o, cst = _mix_prompt(up, uc, wpool_bd, spool, w_dw, bdw, gln, bln, l, tt=512)
        n_cmp = tp // L_CMP
        perm = lambda a: a.reshape(bp, n_cmp // 2, 2, KV_W).transpose(0, 2, 1, 3).reshape(bp, n_cmp, KV_W).astype(BF16)
        c_o = _nsa_prompt(q, gn, perm(ck), perm(cv), ks_b, vs_b, kw_b, vw_b, eneg_p)
        xp = _merge(xp, mp[1], mp[0], mp[2], g_mix, a_o, b_o, c_o, w_gm, wpo, wco, wao, wo, l, tm=512)
        res = _ffn(xp, mp[4], mp[3], mp[5], g_ffn, wgu, wdn, gf, l, tm=512, final=final)
        xp = res[0]
        if final:
            yp = res[1]
        kv4 = lambda a: a.reshape(a.shape[0], a.shape[1], N_KV, HEAD_DIM)
        wlp = min(WINDOW, tp)
        st_p.append((kv4(kc), kv4(vc), kv4(ks), kv4(vs), kv4(kw[:, tp - wlp:]), kv4(vw[:, tp - wlp:]),
                     up[:, tp - (max(POOL_WINDOWS) - 1):], cst[:, HALO - (CONV_K - 1):]))

        (up, uc, q, kc, vc, ks, vs, kw, vw, gn) = _inproj(
            xs, ms[1], ms[0], g_mix, w_in_a, l, tabs_s, wck, wcv, w_pool=wp_dim, w_conv=wc_dim, with_cmp=False,
            q_dtype=F32, tm=512)
        tmaj = lambda a: a.reshape(bd, ts, a.shape[-1]).transpose(1, 0, 2)
        a_t, b_t, glu_t = _mix_sample(state_pool[l].transpose(1, 0, 2), tmaj(up), state_conv[l].transpose(1, 0, 2),
                                      tmaj(uc), wpool_bd, spool, w_dw, bdw, gln, bln, l)
        bmaj = lambda a: a.transpose(1, 0, 2).reshape(1, bd * ts, a.shape[-1])
        per_seq = lambda a: a.reshape(bd, ts, a.shape[-1])
        c_o, *new_win = _nsa_sample(page_table, per_seq(q), per_seq(gn),
                                    [per_seq(a) for a in (kc, vc, ks, vs, kw, vw)], win_k_all, win_v_all, caches,
                                    selk, selv, eneg_s, l, new_win, n_pool=n_pool, depth=depth)
        xs = _merge(xs, ms[1], ms[0], ms[2], g_mix, bmaj(a_t), bmaj(b_t), c_o.reshape(1, bd * ts, -1), w_gm, wpo,
                    wco, wao, wo, l, tm=512)
        res = _ffn(xs, ms[4], ms[3], ms[5], g_ffn, wgu, wdn, gf, l, tm=512, final=final)
        xs = res[0]
        if final:
            ys = res[1]
        kv4s = lambda a: a.reshape(bd, -1, N_KV, HEAD_DIM)
        new_pool = jnp.concatenate([state_pool[l][:, ts:], per_seq(up)], axis=1)
        new_conv = jnp.concatenate([state_conv[l][:, ts:], glu_t.transpose(1, 0, 2)], axis=1)
        st_s.append((kv4s(kc), kv4s(vc), kv4s(ks), kv4s(vs), None, None, new_pool, new_conv))

    win_out = [w.reshape(depth, bd, N_KV, HEAD_DIM, wl).transpose(0, 1, 4, 2, 3) for w in new_win]
    outs = [yp, ys.reshape(bd, ts, d)]
    for i in range(8):
        outs.append(jnp.stack([s[i] for s in st_p]))
        outs.append(win_out[i - 4] if i in (4, 5) else jnp.stack([s[i] for s in st_s]))
    return tuple(outs)
```

```python
import functools

import jax
import jax.numpy as jnp
from jax import lax
from jax.experimental import pallas as pl
from jax.experimental.pallas import tpu as pltpu

F32 = jnp.float32
BF16 = jnp.bfloat16

HEAD_DIM = 64
N_KV = 2
GQA = 4
N_HEADS = N_KV * GQA
ROPE_DIM = 16
ROPE_THETA = 500000.0
L_CMP = 32
L_SLC = 64
L_SLC_SHIFT = 6
N_SEL = 16
WINDOW = 512
Q_BLOCK = 128
PAGE_SIZE = 128
POOL_WINDOWS = (2, 4, 8, 16)
CONV_K = 31
EPS = 1e-6
NEG = -1e30
FORCE_SCORE = 1e4

LANES = 128
SUBLANES = 8
VMEM_LIMIT_BYTES = 56 * 1024 * 1024

NSA_CHUNK_KEYS = 512
NSA_SAMPLE_SEQS = 4
NSA_ROW_TILE = 64
HALO = 32
KV_W = N_KV * HEAD_DIM


def _params(*sem):
    return pltpu.CompilerParams(dimension_semantics=sem, vmem_limit_bytes=VMEM_LIMIT_BYTES)


def _modnorm(x, g, sc, sh):
    ms = jnp.mean(x * x, axis=-1, keepdims=True)
    return (x * lax.rsqrt(ms + EPS) * g) * (1.0 + sc) + sh


def _sigmoid(x):
    return 1.0 / (1.0 + jnp.exp(-x))


def _dot_nt(a, b):
    return lax.dot_general(a, b, (((1,), (1,)), ((), ())), preferred_element_type=F32)


def _masked_softmax(s, mask):
    sm = jnp.where(mask, s, NEG)
    e = jnp.exp(sm - jnp.max(sm, axis=-1, keepdims=True))
    return jnp.where(mask, e / jnp.sum(e, axis=-1, keepdims=True), 0.0)


def _ada_kernel(c_ref, w_ref, b_ref, o_ref):
    c = c_ref[...]
    a = (c * _sigmoid(c)).astype(BF16)
    o_ref[...] = jnp.dot(a, w_ref[...], preferred_element_type=F32) + b_ref[...]


def _ada_all(c_all, w_ada, b_ada):
    depth, d, n = w_ada.shape
    tn = 1536
    assert n % tn == 0
    nb = c_all.shape[0]
    return pl.pallas_call(
        _ada_kernel,
        grid=(depth, n // tn),
        in_specs=[
            pl.BlockSpec((nb, d), lambda l, j: (0, 0)),
            pl.BlockSpec((None, d, tn), lambda l, j: (l, 0, j)),
            pl.BlockSpec((None, 1, tn), lambda l, j: (l, 0, j)),
        ],
        out_specs=pl.BlockSpec((None, nb, tn), lambda l, j: (l, 0, j)),
        out_shape=jax.ShapeDtypeStruct((depth, nb, n), F32),
        compiler_params=_params("parallel", "parallel"),
        name="ada_mod",
    )(c_all, w_ada, b_ada.reshape(depth, 1, n))


def _inproj_kernel(x_ref, sc_ref, sh_ref, g_ref, w_ref, cos_ref, sa_ref, sb_ref, wck_ref, wcv_ref, *outs,
                   w_pool, w_conv, with_cmp):
    (up_ref, uc_ref, q_ref, kc_ref, vc_ref, ks_ref, vs_ref, kw_ref, vw_ref, gn_ref) = outs[:10]
    h = _modnorm(x_ref[0], g_ref[...], sc_ref[0], sh_ref[0]).astype(BF16)
    y = jnp.dot(h, w_ref[...], preferred_element_type=F32)
    cos, sa, sb = cos_ref[...], sa_ref[...], sb_ref[...]

    def rope(t):
        return t * cos + pltpu.roll(t, LANES - ROPE_DIM // 2, 1) * sa + pltpu.roll(t, ROPE_DIM // 2, 1) * sb

    o = 0
    up_ref[0] = y[:, o:o + w_pool]
    o += w_pool
    uc_ref[0] = y[:, o:o + 2 * w_conv]
    o += 2 * w_conv
    for j in range(N_HEADS * HEAD_DIM // LANES):
        q_ref[0, :, j * LANES:(j + 1) * LANES] = rope(y[:, o:o + LANES]).astype(q_ref.dtype)
        o += LANES
    kc = rope(y[:, o:o + KV_W])
    vc = y[:, o + KV_W:o + 2 * KV_W]
    ks = rope(y[:, o + 2 * KV_W:o + 3 * KV_W])
    vs = y[:, o + 3 * KV_W:o + 4 * KV_W]
    kw = rope(y[:, o + 4 * KV_W:o + 5 * KV_W])
    vw = y[:, o + 5 * KV_W:o + 6 * KV_W]
    o += 6 * KV_W
    kc_ref[0], vc_ref[0], ks_ref[0], vs_ref[0], kw_ref[0], vw_ref[0] = kc, vc, ks, vs, kw, vw
    gn_ref[0] = y[:, o:o + LANES]
    if with_cmp:
        ck_ref, cv_ref = outs[10:12]
        tm = kc.shape[0]
        ck_ref[0] = jnp.sum(kc.reshape(tm // L_CMP, L_CMP, KV_W) * wck_ref[...][None], axis=1)
        cv_ref[0] = jnp.sum(vc.reshape(tm // L_CMP, L_CMP, KV_W) * wcv_ref[...][None], axis=1)
        for ref, val in zip(outs[12:], (ks, vs, kw, vw)):
            ref[0] = val.astype(BF16)


def _inproj(x, sc, sh, g, w_a, l, tabs, wck, wcv, *, w_pool, w_conv, with_cmp, q_dtype, tm):
    b, t, d = x.shape
    n = w_a.shape[-1]
    tm = min(tm, t)
    assert t % tm == 0 and (tm % L_CMP == 0 or not with_cmp)
    tmod = sc.shape[1]
    mod_spec = (pl.BlockSpec((1, 1, d), lambda bi, i: (bi, 0, 0)) if tmod == 1
                else pl.BlockSpec((1, tm, d), lambda bi, i: (bi, i, 0)))
    tab_spec = pl.BlockSpec((tm, LANES), lambda bi, i: (i, 0))
    row = lambda w: pl.BlockSpec((1, tm, w), lambda bi, i: (bi, i, 0))
    widths = [w_pool, 2 * w_conv, N_HEADS * HEAD_DIM] + [KV_W] * 6 + [LANES]
    dtypes = [F32, F32, q_dtype] + [F32] * 7
    out_specs = [row(w) for w in widths]
    out_shape = [jax.ShapeDtypeStruct((b, t, w), dt) for w, dt in zip(widths, dtypes)]
    if with_cmp:
        out_specs += [pl.BlockSpec((1, tm // L_CMP, KV_W), lambda bi, i: (bi, i, 0))] * 2 + [row(KV_W)] * 4
        out_shape += ([jax.ShapeDtypeStruct((b, t // L_CMP, KV_W), F32)] * 2
                      + [jax.ShapeDtypeStruct((b, t, KV_W), BF16)] * 4)
    return pl.pallas_call(
        functools.partial(_inproj_kernel, w_pool=w_pool, w_conv=w_conv, with_cmp=with_cmp),
        grid=(b, t // tm),
        in_specs=[
            row(d), mod_spec, mod_spec,
            pl.BlockSpec((None, 1, d), lambda bi, i: (l, 0, 0)),
            pl.BlockSpec((None, d, n), lambda bi, i: (l, 0, 0)),
            tab_spec, tab_spec, tab_spec,
            pl.BlockSpec((None, L_CMP, KV_W), lambda bi, i: (l, 0, 0)),
            pl.BlockSpec((None, L_CMP, KV_W), lambda bi, i: (l, 0, 0)),
        ],
        out_specs=out_specs,
        out_shape=out_shape,
        compiler_params=_params("parallel", "parallel"),
        name="in_proj",
    )(x, sc, sh, g, w_a, *tabs, wck, wcv)


def _pool_means(ext_ref, lvl_ref, base, rows, pos0):
    assert POOL_WINDOWS == (2, 4, 8, 16) and base == 4 * SUBLANES
    n = base + rows
    lane = lax.broadcasted_iota(jnp.int32, (rows, LANES), 1)
    pos = lax.broadcasted_iota(jnp.int32, (rows, LANES), 0) + pos0
    low = lane < POOL_WINDOWS_GW
    means = []
    for tile in range(len(POOL_WINDOWS) // 2):
        cols = slice(tile * LANES, (tile + 1) * LANES)
        w_small, w_big = POOL_WINDOWS[2 * tile], POOL_WINDOWS[2 * tile + 1]
        sums = {}
        src, w, lo = None, 1, 0
        while w < w_big:
            lo += SUBLANES
            if src is None:
                cur = ext_ref[lo:n, cols] + ext_ref[lo - w:n - w, cols]
            else:
                cur = lvl_ref[src, lo:n] + lvl_ref[src, lo - w:n - w]
            w *= 2
            sums[w] = cur[base - lo:]
            if w < w_big:
                src = 0 if src != 0 else 1
                lvl_ref[src, lo:n] = cur
        cnt = jnp.minimum(jnp.where(low, w_small, w_big), pos + 1).astype(F32)
        means.append(jnp.where(low, sums[w_small], sums[w_big]) / cnt)
    return jnp.concatenate(means, axis=1)


POOL_WINDOWS_GW = 64


def _layernorm_silu(y, g, b):
    mu = jnp.mean(y, axis=-1, keepdims=True)
    yc = y - mu
    var = jnp.mean(yc * yc, axis=-1, keepdims=True)
    z = yc * lax.rsqrt(var + EPS) * g + b
    return z * _sigmoid(z)


def _mix_prompt_kernel(up_ref, uph_ref, uc_ref, uch_ref, wpool_ref, spool_ref, wdw_ref, bdw_ref, gln_ref, bln_ref,
                       a_ref, b_ref, cst_ref, pext_ref, cext_ref, part_ref, lvl_ref, *, tt, w_conv):
    i = pl.program_id(1)
    first = i == 0
    u = up_ref[0]
    pext_ref[0:HALO] = jnp.where(first, 0.0, uph_ref[0])
    pext_ref[HALO:HALO + tt] = u
    d = _pool_means(pext_ref, lvl_ref, HALO, tt, i * tt) - u
    z = jnp.dot(d.astype(BF16), wpool_ref[...], preferred_element_type=F32) * spool_ref[...]
    a_ref[0] = z.astype(a_ref.dtype)
    hc = uch_ref[0]
    cext_ref[0:HALO] = jnp.where(first, 0.0, hc[:, :w_conv] * _sigmoid(hc[:, w_conv:]))
    uc = uc_ref[0]
    cext_ref[HALO:HALO + tt] = uc[:, :w_conv] * _sigmoid(uc[:, w_conv:])
    off = HALO - (CONV_K - 1)
    acc = None
    for m in range(SUBLANES):
        n_rows = tt + SUBLANES * (-(-(off + m) // SUBLANES))
        part = None
        for k in range(m, CONV_K, SUBLANES):
            assert k - m + n_rows <= HALO + tt
            term = wdw_ref[k:k + 1, :] * cext_ref[k - m:k - m + n_rows]
            part = term if part is None else part + term
        part_ref[m, 0:n_rows] = part
        shifted = part_ref[m, off + m:off + m + tt]
        acc = shifted if acc is None else acc + shifted
    b_ref[0] = _layernorm_silu(acc + bdw_ref[...], gln_ref[...], bln_ref[...]).astype(b_ref.dtype)
    cst_ref[0] = cext_ref[tt:tt + HALO]


def _mix_prompt(up, uc, wpool_bd, spool, wdw, bdw, gln, bln, l, *, tt):
    b, t, w_pool = up.shape
    w_conv = uc.shape[-1] // 2
    tt = min(tt, t)
    assert t % tt == 0 and tt % HALO == 0
    r = tt // HALO
    halo = lambda w: pl.BlockSpec((1, HALO, w), lambda bi, i: (bi, jnp.maximum(i * r - 1, 0), 0))
    vec = lambda w: pl.BlockSpec((None, 1, w), lambda bi, i: (l, 0, 0))
    return pl.pallas_call(
        functools.partial(_mix_prompt_kernel, tt=tt, w_conv=w_conv),
        grid=(b, t // tt),
        in_specs=[
            pl.BlockSpec((1, tt, w_pool), lambda bi, i: (bi, i, 0)), halo(w_pool),
            pl.BlockSpec((1, tt, 2 * w_conv), lambda bi, i: (bi, i, 0)), halo(2 * w_conv),
            pl.BlockSpec((None, w_pool, w_pool), lambda bi, i: (l, 0, 0)), vec(w_pool),
            pl.BlockSpec((None, CONV_K, w_conv), lambda bi, i: (l, 0, 0)), vec(w_conv), vec(w_conv), vec(w_conv),
        ],
        out_specs=[
            pl.BlockSpec((1, tt, w_pool), lambda bi, i: (bi, i, 0)),
            pl.BlockSpec((1, tt, w_conv), lambda bi, i: (bi, i, 0)),
            pl.BlockSpec((1, HALO, w_conv), lambda bi, i: (bi, 0, 0)),
        ],
        out_shape=[
            jax.ShapeDtypeStruct((b, t, w_pool), BF16),
            jax.ShapeDtypeStruct((b, t, w_conv), BF16),
            jax.ShapeDtypeStruct((b, HALO, w_conv), F32),
        ],
        scratch_shapes=[pltpu.VMEM((HALO + tt, w_pool), F32), pltpu.VMEM((HALO + tt, w_conv), F32),
                        pltpu.VMEM((SUBLANES, tt + 2 * SUBLANES, w_conv), F32),
                        pltpu.VMEM((2, HALO + tt, LANES), F32)],
        compiler_params=_params("parallel", "arbitrary"),
        name="mix_prompt",
    )(up, up, uc, uc, wpool_bd, spool, wdw, bdw, gln, bln)


def _mix_sample_kernel(pst_ref, up_ref, cst_ref, uc_ref, wpool_ref, spool_ref, wdw_ref, bdw_ref, gln_ref, bln_ref,
                       a_ref, b_ref, glu_ref, *, w_conv):
    n_p, n_c, ts = pst_ref.shape[0], cst_ref.shape[0], up_ref.shape[0]
    prow = lambda r: pst_ref[r] if r < n_p else up_ref[r - n_p]
    low = lax.broadcasted_iota(jnp.int32, (up_ref.shape[1], LANES), 1) < POOL_WINDOWS_GW
    glu = []
    for t in range(ts):
        uc = uc_ref[t]
        glu.append(uc[:, :w_conv] * _sigmoid(uc[:, w_conv:]))
        glu_ref[t] = glu[t]
    crow = lambda r: cst_ref[r] if r < n_c else glu[r - n_c]
    for t in range(ts):
        u = up_ref[t]
        means = None
        acc = u
        sums = {1: acc}
        for s in range(1, max(POOL_WINDOWS)):
            acc = acc + prow(n_p + t - s)
            sums[s + 1] = acc
        tiles = []
        for tile in range(len(POOL_WINDOWS) // 2):
            cols = slice(tile * LANES, (tile + 1) * LANES)
            w_small, w_big = POOL_WINDOWS[2 * tile], POOL_WINDOWS[2 * tile + 1]
            tiles.append(jnp.where(low, sums[w_small][:, cols] / float(w_small),
                                   sums[w_big][:, cols] / float(w_big)))
        means = jnp.concatenate(tiles, axis=1)
        z = jnp.dot((means - u).astype(BF16), wpool_ref[...], preferred_element_type=F32) * spool_ref[...]
        a_ref[t] = z.astype(a_ref.dtype)
        acc = wdw_ref[0:1, :] * crow(t + n_c - (CONV_K - 1))
        for k in range(1, CONV_K):
            acc = acc + wdw_ref[k:k + 1, :] * crow(t + n_c - (CONV_K - 1) + k)
        b_ref[t] = _layernorm_silu(acc + bdw_ref[...], gln_ref[...], bln_ref[...]).astype(b_ref.dtype)


def _mix_sample(pst_t, up_t, cst_t, uc_t, wpool_bd, spool, wdw, bdw, gln, bln, l):
    ts, bd, w_pool = up_t.shape
    w_conv = uc_t.shape[-1] // 2
    full = lambda a: pl.BlockSpec(a.shape, lambda i: (0,) * a.ndim)
    vec = lambda w: pl.BlockSpec((None, 1, w), lambda i: (l, 0, 0))
    return pl.pallas_call(
        functools.partial(_mix_sample_kernel, w_conv=w_conv),
        grid=(1,),
        in_specs=[
            full(pst_t), full(up_t), full(cst_t), full(uc_t),
            pl.BlockSpec((None, w_pool, w_pool), lambda i: (l, 0, 0)), vec(w_pool),
            pl.BlockSpec((None, CONV_K, w_conv), lambda i: (l, 0, 0)), vec(w_conv), vec(w_conv), vec(w_conv),
        ],
        out_specs=[pl.BlockSpec((ts, bd, w_pool), lambda i: (0, 0, 0)),
                   pl.BlockSpec((ts, bd, w_conv), lambda i: (0, 0, 0)),
                   pl.BlockSpec((ts, bd, w_conv), lambda i: (0, 0, 0))],
        out_shape=[jax.ShapeDtypeStruct((ts, bd, w_pool), BF16),
                   jax.ShapeDtypeStruct((ts, bd, w_conv), BF16),
                   jax.ShapeDtypeStruct((ts, bd, w_conv), F32)],
        compiler_params=_params("arbitrary"),
        name="mix_sample",
    )(pst_t, up_t, cst_t, uc_t, wpool_bd, spool, wdw, bdw, gln, bln)


def _topk_masks(scores, n_sel):
    sts = tuple(s.T for s in scores)
    cand = lax.broadcasted_iota(jnp.int32, sts[0].shape, 0).astype(F32)

    def body(_, carry):
        out = []
        for st, sel in carry:
            m = jnp.max(st, axis=0, keepdims=True)
            first = jnp.min(jnp.where(st == m, cand, float(LANES)), axis=0, keepdims=True)
            hit = cand == first
            out.append((jnp.where(hit, -jnp.inf, st), jnp.where(hit, 1.0, sel)))
        return tuple(out)

    res = lax.fori_loop(0, n_sel, body, tuple((st, jnp.zeros(st.shape, F32)) for st in sts))
    return [sel.T for _, sel in res]


def _head_rows(tile, src_half, dst_half, lane):
    if src_half != dst_half:
        tile = pltpu.roll(tile, HEAD_DIM, 1)
    return jnp.where((lane >= dst_half * HEAD_DIM) & (lane < (dst_half + 1) * HEAD_DIM), tile, 0.0)


def _nsa_prompt_kernel(q_ref, gn_ref, ck_ref, cv_ref, ks_ref, vs_ref, kw_ref, vw_ref, eneg_ref, o_ref,
                       qa_scr, s_scr, p_scr, sw_scr, pw_scr, *, ck_keys):
    i = pl.program_id(1)
    qb = Q_BLOCK
    rows = GQA * qb
    scale = HEAD_DIM ** -0.5
    n_cmp = ck_ref.shape[1]
    n_slc = n_cmp // 2
    lane = lax.broadcasted_iota(jnp.int32, (qb, LANES), 1)

    def qpos(shape, row0=0):
        return i * qb + ((row0 + lax.broadcasted_iota(jnp.int32, shape, 0)) & (qb - 1))

    sig = _sigmoid(gn_ref[0])
    n_wchunks = WINDOW // qb + 1
    out_tiles = [None] * (N_HEADS // 2)
    qks, q_augs, o_cs, scores_s = [], [], [], []
    cur = qpos(lane.shape) >> L_SLC_SHIFT
    valid_s = lane <= cur
    forced = (lane == 0) | (lane == cur) | (lane == cur - 1)
    for k in range(N_KV):
        qh = []
        for g in range(GQA):
            h = k * GQA + g
            tile = q_ref[0, :, (h // 2) * LANES:(h // 2 + 1) * LANES].astype(F32) * scale
            qh.append(_head_rows(tile, h % 2, k, lane))
        qk = jnp.concatenate(qh, axis=0).astype(BF16)

        s_c = _dot_nt(qk, ck_ref[0])
        cl = lax.broadcasted_iota(jnp.int32, (rows, n_cmp), 1)
        c_end = jnp.where(cl < n_slc, cl * L_SLC + L_CMP - 1, (cl - n_slc) * L_SLC + L_SLC - 1)
        p_c = _masked_softmax(s_c, c_end <= qpos(s_c.shape))
        o_c = jnp.dot(p_c.astype(BF16), cv_ref[0], preferred_element_type=F32)
        ps = p_c[0:qb]
        for g in range(1, GQA):
            ps = ps + p_c[g * qb:(g + 1) * qb]
        imp = ps[:, :n_slc] + ps[:, n_slc:]
        if n_slc < LANES:
            imp = jnp.concatenate([imp, jnp.zeros((qb, LANES - n_slc), F32)], axis=1)

        score = jnp.where(valid_s, jnp.where(forced, FORCE_SCORE, imp), -1.0)
        scores_s.append(jnp.where(lane < n_slc, score, -jnp.inf))
        qks.append(qk)
        o_cs.append(o_c)

    for k, sel in enumerate(_topk_masks(scores_s, min(N_SEL, n_slc))):
        notsel = (1.0 - jnp.where(valid_s, sel, 0.0)).astype(BF16)
        q_augs.append(jnp.concatenate([qks[k], jnp.concatenate([notsel] * GQA, axis=0)], axis=1))

    for k in range(N_KV):
        qa_scr[k] = q_augs[k]
    own = lax.broadcasted_iota(jnp.int32, (ck_keys, LANES), 1) < HEAD_DIM

    def scores(c, buf):
        start = pl.multiple_of(c * ck_keys, ck_keys)
        k_aug = jnp.concatenate([ks_ref[0, pl.ds(start, ck_keys), :], eneg_ref[pl.ds(start, ck_keys), :]], axis=1)
        for k in range(N_KV):
            s_scr[buf, k] = _dot_nt(qa_scr[k], k_aug)

    def probs(c, buf, ms, causal):
        new_m, alphas = [], []
        for k in range(N_KV):
            m_tiles = []
            for r in range(0, rows, NSA_ROW_TILE):
                rs = slice(r, r + NSA_ROW_TILE)
                s = s_scr[buf, k, rs]
                if causal:
                    kpos = c * ck_keys + lax.broadcasted_iota(jnp.int32, s.shape, 1)
                    s = jnp.where(kpos <= qpos(s.shape, r), s, NEG)
                m_new = jnp.maximum(ms[k][rs], jnp.max(s, axis=-1, keepdims=True))
                p_scr[buf, k, rs] = jnp.exp(s - m_new).astype(BF16)
                m_tiles.append(m_new)
            m_new = jnp.concatenate(m_tiles, axis=0)
            alphas.append(jnp.exp(ms[k] - m_new))
            new_m.append(m_new)
        return new_m, alphas

    def values(c, buf, alphas, accs):
        start = pl.multiple_of(c * ck_keys, ck_keys)
        v = vs_ref[0, pl.ds(start, ck_keys), :]
        v_one = [jnp.where(own, v, 1.0), jnp.where(own, 1.0, v)]
        return [alphas[k] * accs[k] + jnp.dot(p_scr[buf, k], v_one[k], preferred_element_type=F32)
                for k in range(N_KV)]

    def pair(c, carry, causal, last):
        ms, alphas, accs = carry
        accs = values(jnp.maximum(c - 1, 0), 1, alphas, accs)
        ms, alphas = probs(c, 0, ms, causal)
        scores(c + 1, 1)
        accs = values(c, 0, alphas, accs)
        ms, alphas = probs(c + 1, 1, ms, causal)
        if last:
            accs = values(c + 1, 1, alphas, accs)
        else:
            scores(c + 2, 0)
        return ms, alphas, accs

    n_pairs = (i * qb + qb + 2 * ck_keys - 1) // (2 * ck_keys)
    p_scr[1] = jnp.zeros(p_scr.shape[1:], BF16)
    scores(0, 0)
    carry = ([jnp.full((rows, 1), -jnp.inf, F32)] * N_KV, [jnp.ones((rows, 1), F32)] * N_KV,
             [jnp.zeros((rows, LANES), F32)] * N_KV)
    carry = lax.fori_loop(0, n_pairs - 1, lambda j, cr: pair(2 * j, cr, False, False), carry)
    _, _, accs = pair(2 * (n_pairs - 1), carry, True, True)
    o_ss = [acc / pltpu.roll(acc, HEAD_DIM, 1) for acc in accs]

    kws, vws = [], []
    for dj in range(n_wchunks):
        st = pl.multiple_of(jnp.maximum(i - (n_wchunks - 1) + dj, 0) * qb, qb)
        kws.append(kw_ref[0, pl.ds(st, qb), :])
        vws.append(vw_ref[0, pl.ds(st, qb), :])
    kw_blk, vw_blk = jnp.concatenate(kws, axis=0), jnp.concatenate(vws, axis=0)
    wk = kw_blk.shape[0]
    kwpos = (i - (n_wchunks - 1)) * qb + lax.broadcasted_iota(jnp.int32, (qb, wk), 1)
    qp = qpos((qb, wk))
    wbias = jnp.where((kwpos <= qp) & (kwpos > qp - WINDOW) & (kwpos >= 0), 0.0, NEG)
    own_w = lax.broadcasted_iota(jnp.int32, (wk, LANES), 1) < HEAD_DIM
    vw_one = [jnp.where(own_w, vw_blk, 1.0), jnp.where(own_w, 1.0, vw_blk)]
    for k in range(N_KV):
        sw_scr[k] = _dot_nt(qks[k], kw_blk)
    o_ws = []
    for k in range(N_KV):
        for r in range(0, rows, NSA_ROW_TILE):
            rs = slice(r, r + NSA_ROW_TILE)
            s = sw_scr[k, rs] + wbias[r % qb:r % qb + NSA_ROW_TILE]
            pw_scr[k, rs] = jnp.exp(s - jnp.max(s, axis=-1, keepdims=True)).astype(BF16)
        acc = jnp.dot(pw_scr[k], vw_one[k], preferred_element_type=F32)
        o_ws.append(acc / pltpu.roll(acc, HEAD_DIM, 1))

    for k in range(N_KV):
        o_c, o_s, o_w = o_cs[k], o_ss[k], o_ws[k]
        for g in range(GQA):
            h = k * GQA + g
            col = 3 * h
            r = slice(g * qb, (g + 1) * qb)
            o = sig[:, col:col + 1] * o_c[r] + sig[:, col + 1:col + 2] * o_s[r] + sig[:, col + 2:col + 3] * o_w[r]
            o = _head_rows(o, k, h % 2, lane)
            out_tiles[h // 2] = o if out_tiles[h // 2] is None else out_tiles[h // 2] + o
    for j, tile in enumerate(out_tiles):
        o_ref[0, :, j * LANES:(j + 1) * LANES] = tile.astype(o_ref.dtype)


def _nsa_prompt(q, gn, ckp, cvp, ks, vs, kw, vw, eneg):
    b, t, _ = q.shape
    assert t % Q_BLOCK == 0 and t // L_SLC <= LANES
    ck_keys = min(NSA_CHUNK_KEYS, t // 2)
    assert t % (2 * ck_keys) == 0
    rows = GQA * Q_BLOCK
    qspec = lambda w: pl.BlockSpec((1, Q_BLOCK, w), lambda bi, i: (bi, i, 0))
    seq = lambda n: pl.BlockSpec((1, n, KV_W), lambda bi, i: (bi, 0, 0))
    return pl.pallas_call(
        functools.partial(_nsa_prompt_kernel, ck_keys=ck_keys),
        scratch_shapes=[pltpu.VMEM((N_KV, rows, 2 * LANES), BF16), pltpu.VMEM((2, N_KV, rows, ck_keys), F32),
                        pltpu.VMEM((2, N_KV, rows, ck_keys), BF16),
                        pltpu.VMEM((N_KV, rows, WINDOW + Q_BLOCK), F32),
                        pltpu.VMEM((N_KV, rows, WINDOW + Q_BLOCK), BF16)],
        grid=(b, t // Q_BLOCK),
        in_specs=[qspec(N_HEADS * HEAD_DIM), qspec(LANES), seq(t // L_CMP), seq(t // L_CMP),
                  seq(t), seq(t), seq(t), seq(t), pl.BlockSpec((t, LANES), lambda bi, i: (0, 0))],
        out_specs=qspec(N_HEADS * HEAD_DIM),
        out_shape=jax.ShapeDtypeStruct((b, t, N_HEADS * HEAD_DIM), BF16),
        compiler_params=_params("parallel", "parallel"),
        name="nsa_prompt",
    )(q, gn, ckp, cvp, ks, vs, kw, vw, eneg)


def _nsa_sample_kernel(pt_ref, q_ref, gn_ref, kcn_ref, vcn_ref, ksn_ref, vsn_ref, kwn_ref, vwn_ref, wk_ref, wv_ref,
                       selk_ref, selv_ref, eneg_ref, *rest, n_pages, past_len, ts, page0):
    caches = rest[2:6]
    o_ref, nwk_ref, nwv_ref = rest[6:9]
    (past_buf, sems, tail_s, q_s, sc_s, g_s, o_s) = rest[9:]
    n_seq = q_ref.shape[0]
    b = pl.program_id(0)
    slot = b & 1
    scale = HEAD_DIM ** -0.5
    wl = wk_ref.shape[2]
    n_keys = past_len + LANES
    n_slc = -(-(past_len + ts) // L_SLC)
    rows = GQA * N_KV * ts
    grp = N_KV * ts
    lane_t = lax.broadcasted_iota(jnp.int32, (ts, LANES), 1)

    def page_copy(step, dst_slot, u, j, p):
        src = caches[j].at[page0 + pt_ref[(step * n_seq + u) * n_pages + p]]
        dst = past_buf.at[dst_slot, u, j, :, pl.ds(p * PAGE_SIZE, PAGE_SIZE)]
        return pltpu.make_async_copy(src, dst, sems.at[dst_slot])

    def all_pages(step, dst_slot, op):
        for u in range(n_seq):
            for j in range(4):
                for p in range(n_pages):
                    op(page_copy(step, dst_slot, u, j, p))

    @pl.when(b == 0)
    def _():
        all_pages(0, 0, lambda cp: cp.start())

    @pl.when(b + 1 < pl.num_programs(0))
    def _():
        all_pages(b + 1, 1 - slot, lambda cp: cp.start())

    all_pages(b, slot, lambda cp: cp.wait())

    def qpos(shape):
        return past_len + (lax.broadcasted_iota(jnp.int32, shape, 0) & (ts - 1))

    lane_g = lax.broadcasted_iota(jnp.int32, (grp, LANES), 1)
    blk = lane_g >> 1
    is_blk = ((lane_g & 1) == 0) & (blk < n_slc)
    cur = qpos(lane_g.shape) >> L_SLC_SHIFT
    valid_s = blk <= cur
    forced = (blk == 0) | (blk == cur) | (blk == cur - 1)

    def front(u, out):
        def new_cols(j, new_ref):
            tail_s[u, j] = jnp.zeros(tail_s.shape[2:], F32)
            tail_s[u, j, 0:ts] = new_ref[u]
            return tail_s[u, j].T

        kc_p, vc_p, ks_p, vs_p = (past_buf[slot, u, j].astype(BF16) for j in range(4))
        kc_n, vc_n, ks_n, vs_n = (new_cols(j, r).astype(BF16)
                                  for j, r in enumerate((kcn_ref, vcn_ref, ksn_ref, vsn_ref)))

        kw_all = jnp.concatenate([wk_ref[u], new_cols(4, kwn_ref)], axis=1)
        vw_all = jnp.concatenate([wv_ref[u], new_cols(5, vwn_ref)], axis=1)
        nwk_ref[u] = kw_all[:, ts:ts + wl]
        nwv_ref[u] = vw_all[:, ts:ts + wl]
        yield

        for g in range(GQA):
            for k in range(N_KV):
                h = k * GQA + g
                tile = q_ref[u, :, (h // 2) * LANES:(h // 2 + 1) * LANES] * scale
                q_s[u, (g * N_KV + k) * ts:(g * N_KV + k + 1) * ts] = _head_rows(tile, h % 2, k, lane_t)
        qa = q_s[u].astype(BF16)

        def compress(past, new, sel_ref):
            return (jnp.dot(past, sel_ref[0:past_len], preferred_element_type=F32)
                    + jnp.dot(new, sel_ref[past_len:n_keys], preferred_element_type=F32)).astype(BF16)

        ck_t = compress(kc_p, kc_n, selk_ref)
        yield
        cv_t = compress(vc_p, vc_n, selv_ref)
        yield
        s_c = jnp.dot(qa, ck_t, preferred_element_type=F32)
        yield
        cl = lax.broadcasted_iota(jnp.int32, s_c.shape, 1)
        p_c = _masked_softmax(s_c, (cl + 1) * L_CMP - 1 <= qpos(s_c.shape))
        yield
        o_c = _dot_nt(p_c.astype(BF16), cv_t)
        ps = p_c[0:grp]
        for g in range(1, GQA):
            ps = ps + p_c[g * grp:(g + 1) * grp]
        imp = ps + pltpu.roll(ps, LANES - 1, 1)

        score = jnp.where(valid_s, jnp.where(forced, FORCE_SCORE, imp), -1.0)
        sc_s[u * grp:(u + 1) * grp] = jnp.where(is_blk, score, -jnp.inf)
        out[u] = (qa, o_c, ks_p, ks_n, vs_p, vs_n, kw_all, vw_all)

    sc_s[...] = jnp.zeros(sc_s.shape, F32)
    fronts = {}
    _lockstep(front(u, fronts) for u in range(n_seq))
    sel = _topk_masks([sc_s[...]], min(N_SEL, n_slc))[0]
    _lockstep(_nsa_sample_back(u, fronts[u], sel[u * grp:(u + 1) * grp], is_blk & valid_s, qpos, gn_ref, eneg_ref,
                               o_ref, g_s, o_s, past_len=past_len, ts=ts, wl=wl) for u in range(n_seq))


def _lockstep(stages):
    stages = list(stages)
    while stages:
        for gen in list(stages):
            try:
                next(gen)
            except StopIteration:
                stages.remove(gen)


def _nsa_sample_back(u, front, sel, selectable, qpos, gn_ref, eneg_ref, o_ref, g_s, o_s, *, past_len, ts, wl):
    qa, o_c, ks_p, ks_n, vs_p, vs_n, kw_all, vw_all = front
    n_keys = past_len + LANES
    lane_t = lax.broadcasted_iota(jnp.int32, (ts, LANES), 1)
    notsel = (1.0 - jnp.where(selectable, sel, 0.0)).astype(BF16)
    q_aug = jnp.concatenate([qa, jnp.concatenate([notsel] * GQA, axis=0)], axis=1)

    s_s = jnp.concatenate(
        [jnp.dot(q_aug, jnp.concatenate([ks_p, eneg_ref[:, 0:past_len]], axis=0), preferred_element_type=F32),
         jnp.dot(q_aug, jnp.concatenate([ks_n, eneg_ref[:, past_len:n_keys]], axis=0), preferred_element_type=F32)],
        axis=1)
    s_w = jnp.dot(qa, kw_all.astype(BF16), preferred_element_type=F32)
    yield
    kpos = lax.broadcasted_iota(jnp.int32, s_s.shape, 1)
    s_s = jnp.where(kpos <= qpos(s_s.shape), s_s, NEG)
    e = jnp.exp(s_s - jnp.max(s_s, axis=-1, keepdims=True))
    yield
    p_s = (e / jnp.sum(e, axis=-1, keepdims=True)).astype(BF16)
    yield
    o_sel = _dot_nt(p_s[:, 0:past_len], vs_p) + _dot_nt(p_s[:, past_len:n_keys], vs_n)
    yield
    kwpos = past_len - wl + lax.broadcasted_iota(jnp.int32, s_w.shape, 1)
    qp = qpos(s_w.shape)
    okw = (kwpos <= qp) & (kwpos > qp - WINDOW) & (kwpos >= 0)
    p_w = _masked_softmax(s_w, okw)
    yield
    o_w = _dot_nt(p_w.astype(BF16), vw_all.astype(BF16))
    yield

    sig = _sigmoid(gn_ref[u])
    o = None
    for j, ob in enumerate((o_c, o_sel, o_w)):
        for g in range(GQA):
            for k in range(N_KV):
                col = 3 * (k * GQA + g) + j
                g_s[u, j, (g * N_KV + k) * ts:(g * N_KV + k + 1) * ts] = jnp.broadcast_to(sig[:, col:col + 1],
                                                                                        (ts, LANES))
        o = g_s[u, j] * ob if o is None else o + g_s[u, j] * ob
    o_s[u] = o
    for j in range(N_HEADS // 2):
        tile = None
        for h in (2 * j, 2 * j + 1):
            k, g = h // GQA, h % GQA
            piece = _head_rows(o_s[u, (g * N_KV + k) * ts:(g * N_KV + k + 1) * ts], k, h % 2, lane_t)
            tile = piece if tile is None else tile + piece
        o_ref[u, :, j * LANES:(j + 1) * LANES] = tile


def _nsa_sample(page_table, q, gn, new_rows, win_k, win_v, caches, selk, selv, eneg, l, new_win, *, n_pool, depth):
    bd, ts, _ = q.shape
    n_pages = page_table.shape[1]
    past_len = n_pages * PAGE_SIZE
    wl = win_k.shape[2]
    n_keys = past_len + LANES
    assert ts <= LANES and -(-(past_len + ts) // L_SLC) * 2 <= LANES and eneg.shape == (LANES, n_keys)
    rows = GQA * N_KV * ts
    n_seq = NSA_SAMPLE_SEQS
    assert bd % n_seq == 0 and n_seq * N_KV * ts <= LANES
    tok = lambda w: pl.BlockSpec((n_seq, ts, w), lambda b, pt: (b, 0, 0))
    win = pl.BlockSpec((n_seq, KV_W, wl), lambda b, pt: (l * (bd // n_seq) + b, 0, 0))
    sel = pl.BlockSpec((None, n_keys, LANES), lambda b, pt: (l, 0, 0))

    hbm = pl.BlockSpec(memory_space=pl.ANY)
    in_specs = ([tok(N_HEADS * HEAD_DIM), tok(LANES)] + [tok(KV_W)] * 6 + [win, win, sel, sel,
                pl.BlockSpec((LANES, n_keys), lambda b, pt: (0, 0))] + [hbm] * 6)
    grid_spec = pltpu.PrefetchScalarGridSpec(
        num_scalar_prefetch=1,
        grid=(bd // n_seq,),
        in_specs=in_specs,
        out_specs=[tok(N_HEADS * HEAD_DIM), win, win],
        scratch_shapes=[
            pltpu.VMEM((2, n_seq, 4, KV_W, past_len), F32),
            pltpu.SemaphoreType.DMA((2,)),
            pltpu.VMEM((n_seq, 6, LANES, KV_W), F32),
            pltpu.VMEM((n_seq, rows, LANES), F32), pltpu.VMEM((LANES, LANES), F32),
            pltpu.VMEM((n_seq, 3, rows, LANES), F32), pltpu.VMEM((n_seq, rows, LANES), F32),
        ],
    )
    win_shape = jax.ShapeDtypeStruct((depth * bd, KV_W, wl), F32)
    n_fixed = 1 + 2 + 6 + 5
    return pl.pallas_call(
        functools.partial(_nsa_sample_kernel, n_pages=n_pages, past_len=past_len, ts=ts, page0=l * n_pool),
        grid_spec=grid_spec,
        out_shape=[jax.ShapeDtypeStruct((bd, ts, N_HEADS * HEAD_DIM), F32), win_shape, win_shape],
        input_output_aliases={n_fixed: 1, n_fixed + 1: 2},
        compiler_params=_params("arbitrary"),
        name="nsa_sample",
    )(page_table.reshape(-1), q, gn, *new_rows, win_k, win_v, selk, selv, eneg, *new_win, *caches)


def _merge_kernel(x_ref, sc_ref, sh_ref, gt_ref, g_ref, a_ref, b_ref, c_ref, wgm_ref, wp_ref, wc_ref, wa_ref,
                  wo_ref, o_ref):
    x = x_ref[0]
    d = x.shape[-1]
    h = _modnorm(x, g_ref[...], sc_ref[0], sh_ref[0]).astype(BF16)
    branches = (
        jnp.dot(a_ref[0].astype(BF16), wp_ref[...], preferred_element_type=F32),
        jnp.dot(b_ref[0].astype(BF16), wc_ref[...], preferred_element_type=F32),
        jnp.dot(c_ref[0].astype(BF16), wa_ref[...], preferred_element_type=F32),
    )
    merged = None
    for j, br in enumerate(branches):
        gm = _sigmoid(jnp.dot(h, wgm_ref[:, j * d:(j + 1) * d], preferred_element_type=F32))
        merged = gm * br if merged is None else merged + gm * br
    out = jnp.dot(merged.astype(BF16), wo_ref[...], preferred_element_type=F32)
    o_ref[0] = x + gt_ref[0] * out


def _mod_spec(tmod, tm, d):
    if tmod == 1:
        return pl.BlockSpec((1, 1, d), lambda bi, i: (bi, 0, 0))
    return pl.BlockSpec((1, tm, d), lambda bi, i: (bi, i, 0))


def _merge(x, sc, sh, gt, g, a, bo, c, wgm, wp, wc, wa, wo, l, *, tm):
    b, t, d = x.shape
    tm = min(tm, t)
    assert t % tm == 0
    ms = _mod_spec(sc.shape[1], tm, d)
    row = lambda w: pl.BlockSpec((1, tm, w), lambda bi, i: (bi, i, 0))
    wsp = lambda w: pl.BlockSpec((None,) + w.shape[1:], lambda bi, i: (l, 0, 0))
    return pl.pallas_call(
        _merge_kernel,
        grid=(b, t // tm),
        in_specs=[row(d), ms, ms, ms, pl.BlockSpec((None, 1, d), lambda bi, i: (l, 0, 0)),
                  row(a.shape[-1]), row(bo.shape[-1]), row(c.shape[-1]),
                  wsp(wgm), wsp(wp), wsp(wc), wsp(wa), wsp(wo)],
        out_specs=row(d),
        out_shape=jax.ShapeDtypeStruct((b, t, d), F32),
        compiler_params=_params("parallel", "parallel"),
        name="merge_out",
    )(x, sc, sh, gt, g, a, bo, c, wgm, wp, wc, wa, wo)


def _ffn_kernel(x_ref, sc_ref, sh_ref, gt_ref, g_ref, wgu_ref, wd_ref, gf_ref, *outs, d_ff, n_parts, final):
    x = x_ref[0]
    h = _modnorm(x, g_ref[...], sc_ref[0], sh_ref[0]).astype(BF16)
    part = d_ff // n_parts
    acc = None
    for c in range(n_parts):
        gp = jnp.dot(h, wgu_ref[:, c * part:(c + 1) * part], preferred_element_type=F32)
        up = jnp.dot(h, wgu_ref[:, d_ff + c * part:d_ff + (c + 1) * part], preferred_element_type=F32)
        act = (gp * _sigmoid(gp) * up).astype(BF16)
        dn = jnp.dot(act, wd_ref[c * part:(c + 1) * part, :], preferred_element_type=F32)
        acc = dn if acc is None else acc + dn
    y = x + gt_ref[0] * acc
    outs[0][0] = y
    if final:
        ms = jnp.mean(y * y, axis=-1, keepdims=True)
        outs[1][0] = y * lax.rsqrt(ms + EPS) * gf_ref[...]


def _ffn(x, sc, sh, gt, g, wgu, wd, gf, l, *, tm, final):
    b, t, d = x.shape
    d_ff = wd.shape[1]
    tm = min(tm, t)
    n_parts = 2
    assert t % tm == 0 and d_ff % (n_parts * LANES) == 0
    ms = _mod_spec(sc.shape[1], tm, d)
    row = pl.BlockSpec((1, tm, d), lambda bi, i: (bi, i, 0))
    n_out = 2 if final else 1
    return pl.pallas_call(
        functools.partial(_ffn_kernel, d_ff=d_ff, n_parts=n_parts, final=final),
        grid=(b, t // tm),
        in_specs=[row, ms, ms, ms, pl.BlockSpec((None, 1, d), lambda bi, i: (l, 0, 0)),
                  pl.BlockSpec((None, d, 2 * d_ff), lambda bi, i: (l, 0, 0)),
                  pl.BlockSpec((None, d_ff, d), lambda bi, i: (l, 0, 0)),
                  pl.BlockSpec((1, d), lambda bi, i: (0, 0))],
        out_specs=[row] * n_out,
        out_shape=[jax.ShapeDtypeStruct((b, t, d), F32)] * n_out,
        compiler_params=_params("parallel", "parallel"),
        name="ffn",
    )(x, sc, sh, gt, g, wgu, wd, gf)


def _rope_tables(pos):
    half = ROPE_DIM // 2
    inv = ROPE_THETA ** (-jnp.arange(half, dtype=F32) * 2.0 / ROPE_DIM)
    ang = pos.astype(F32)[:, None] * inv[None, :]
    cos, sin = jnp.cos(ang), jnp.sin(ang)
    n = pos.shape[0]
    rest = HEAD_DIM - ROPE_DIM
    c = jnp.concatenate([cos, cos, jnp.ones((n, rest), F32)], axis=1)
    sa = jnp.concatenate([-sin, jnp.zeros((n, half + rest), F32)], axis=1)
    sb = jnp.concatenate([jnp.zeros((n, half), F32), sin, jnp.zeros((n, rest), F32)], axis=1)
    rep = LANES // HEAD_DIM
    return tuple(jnp.tile(a, (1, rep)) for a in (c, sa, sb))


def _block_bias(n_keys, lane_stride):
    key_blk = (jnp.arange(n_keys) // L_SLC)[:, None] * lane_stride
    return jnp.where(key_blk == jnp.arange(LANES)[None, :], NEG, 0.0).astype(BF16)


def kernel(x_prompt, x_sample, cache_cmp_k, cache_cmp_v, cache_slc_k, cache_slc_v, state_win_k, state_win_v,
           state_pool, state_conv, page_table, c_prompt, c_sample, w_ada, b_ada, g_norm_mix, g_norm_ffn, w_in,
           w_pool, s_pool, w_pool_out, w_dw, b_dw, g_conv_ln, b_conv_ln, w_conv_out, w_cmp_k, w_cmp_v,
           w_attn_out, w_out, w_gu, w_down, g_final):
    depth, d, _ = w_in.shape
    bp, tp, _ = x_prompt.shape
    bd, ts, _ = x_sample.shape
    wp_dim = w_pool_out.shape[1]
    wc_dim = w_conv_out.shape[1]
    n_pool = cache_cmp_k.shape[1]
    n_pages = page_table.shape[1]
    past_len = n_pages * PAGE_SIZE
    wl = state_win_k.shape[2]
    n_groups, gw = w_pool.shape[1], w_pool.shape[2]
    assert gw == POOL_WINDOWS_GW and n_groups == len(POOL_WINDOWS) and wp_dim == n_groups * gw

    n_a = wp_dim + 2 * wc_dim + N_HEADS * HEAD_DIM + 6 * KV_W + 3 * N_HEADS
    n_a_pad = n_a - 3 * N_HEADS + LANES
    w_in_a = jnp.pad(w_in[:, :, :n_a], ((0, 0), (0, 0), (0, n_a_pad - n_a))).astype(BF16)
    w_gm = w_in[:, :, n_a:].astype(BF16)
    w_ada_b = w_ada.astype(BF16)
    eye = jnp.eye(n_groups, dtype=F32)
    wpool_bd = (w_pool[:, :, :, None, :] * eye[None, :, None, :, None]).reshape(depth, wp_dim, wp_dim).astype(BF16)
    wpo, wco, wao, wo = (w.astype(BF16) for w in (w_pool_out, w_conv_out, w_attn_out, w_out))
    wgu, wdn = w_gu.astype(BF16), w_down.astype(BF16)
    wck = jnp.broadcast_to(w_cmp_k[:, :, None], (depth, L_CMP, KV_W))
    wcv = jnp.broadcast_to(w_cmp_v[:, :, None], (depth, L_CMP, KV_W))
    vec3 = lambda a: a.reshape(depth, 1, -1)
    g_mix, g_ffn = vec3(g_norm_mix), vec3(g_norm_ffn)
    spool, bdw, gln, bln = vec3(s_pool), vec3(b_dw), vec3(g_conv_ln), vec3(b_conv_ln)
    gf = g_final.reshape(1, d)

    tabs_p = _rope_tables(jnp.arange(tp))
    tabs_s = _rope_tables(jnp.tile(past_len + jnp.arange(ts), bd))
    eneg_p = _block_bias(tp, 1)
    n_keys_s = past_len + LANES
    eneg_s = _block_bias(n_keys_s, 2).T
    key = jnp.arange(n_keys_s)
    blk_of_key = (key[:, None] // L_CMP == jnp.arange(LANES)[None, :]).astype(F32)
    selk = (w_cmp_k[:, key % L_CMP, None] * blk_of_key[None]).astype(BF16)
    selv = (w_cmp_v[:, key % L_CMP, None] * blk_of_key[None]).astype(BF16)

    ada = _ada_all(jnp.concatenate([c_prompt, c_sample], axis=0), w_ada_b, b_ada)
    pos_minor = lambda a: a.transpose(0, 1, 3, 4, 2).reshape(a.shape[0] * a.shape[1], KV_W, a.shape[2])
    caches = tuple(pos_minor(c) for c in (cache_cmp_k, cache_cmp_v, cache_slc_k, cache_slc_v))
    win_k_all, win_v_all = pos_minor(state_win_k), pos_minor(state_win_v)
    new_win = [jnp.zeros(win_k_all.shape, F32), jnp.zeros(win_v_all.shape, F32)]

    xp = x_prompt
    xs = x_sample.reshape(1, bd * ts, d)
    st_p, st_s = [], []
    yp = ys = None
    for l in range(depth):
        final = l == depth - 1
        mp = [m[:, None, :] for m in jnp.split(ada[l, :bp], 6, axis=-1)]
        ms = [jnp.repeat(m, ts, axis=0)[None] for m in jnp.split(ada[l, bp:], 6, axis=-1)]

        (up, uc, q, kc, vc, ks, vs, kw, vw, gn, ck, cv, ks_b, vs_b, kw_b, vw_b) = _inproj(
            xp, mp[1], mp[0], g_mix, w_in_a, l, tabs_p, wck, wcv, w_pool=wp_dim, w_conv=wc_dim, with_cmp=True,
            q_dtype=BF16, tm=512)
        a_o, b_o, cst = _mix_prompt(up, uc, wpool_bd, spool, w_dw, bdw, gln, bln, l, tt=512)
        n_cmp = tp // L_CMP
        perm = lambda a: a.reshape(bp, n_cmp // 2, 2, KV_W).transpose(0, 2, 1, 3).reshape(bp, n_cmp, KV_W).astype(BF16)
        c_o = _nsa_prompt(q, gn, perm(ck), perm(cv), ks_b, vs_b, kw_b, vw_b, eneg_p)
        xp = _merge(xp, mp[1], mp[0], mp[2], g_mix, a_o, b_o, c_o, w_gm, wpo, wco, wao, wo, l, tm=512)
        res = _ffn(xp, mp[4], mp[3], mp[5], g_ffn, wgu, wdn, gf, l, tm=512, final=final)
        xp = res[0]
        if final:
            yp = res[1]
        kv4 = lambda a: a.reshape(a.shape[0], a.shape[1], N_KV, HEAD_DIM)
        wlp = min(WINDOW, tp)
        st_p.append((kv4(kc), kv4(vc), kv4(ks), kv4(vs), kv4(kw[:, tp - wlp:]), kv4(vw[:, tp - wlp:]),
                     up[:, tp - (max(POOL_WINDOWS) - 1):], cst[:, HALO - (CONV_K - 1):]))

        (up, uc, q, kc, vc, ks, vs, kw, vw, gn) = _inproj(
            xs, ms[1], ms[0], g_mix, w_in_a, l, tabs_s, wck, wcv, w_pool=wp_dim, w_conv=wc_dim, with_cmp=False,
            q_dtype=F32, tm=512)
        tmaj = lambda a: a.reshape(bd, ts, a.shape[-1]).transpose(1, 0, 2)
        a_t, b_t, glu_t = _mix_sample(state_pool[l].transpose(1, 0, 2), tmaj(up), state_conv[l].transpose(1, 0, 2),
                                      tmaj(uc), wpool_bd, spool, w_dw, bdw, gln, bln, l)
        bmaj = lambda a: a.transpose(1, 0, 2).reshape(1, bd * ts, a.shape[-1])
        per_seq = lambda a: a.reshape(bd, ts, a.shape[-1])
        c_o, *new_win = _nsa_sample(page_table, per_seq(q), per_seq(gn),
                                    [per_seq(a) for a in (kc, vc, ks, vs, kw, vw)], win_k_all, win_v_all, caches,
                                    selk, selv, eneg_s, l, new_win, n_pool=n_pool, depth=depth)
        xs = _merge(xs, ms[1], ms[0], ms[2], g_mix, bmaj(a_t), bmaj(b_t), c_o.reshape(1, bd * ts, -1), w_gm, wpo,
                    wco, wao, wo, l, tm=512)
        res = _ffn(xs, ms[4], ms[3], ms[5], g_ffn, wgu, wdn, gf, l, tm=512, final=final)
        xs = res[0]
        if final:
            ys = res[1]
        kv4s = lambda a: a.reshape(bd, -1, N_KV, HEAD_DIM)
        new_pool = jnp.concatenate([state_pool[l][:, ts:], per_seq(up)], axis=1)
        new_conv = jnp.concatenate([state_conv[l][:, ts:], glu_t.transpose(1, 0, 2)], axis=1)
        st_s.append((kv4s(kc), kv4s(vc), kv4s(ks), kv4s(vs), None, None, new_pool, new_conv))

    win_out = [w.reshape(depth, bd, N_KV, HEAD_DIM, wl).transpose(0, 1, 4, 2, 3) for w in new_win]
    outs = [yp, ys.reshape(bd, ts, d)]
    for i in range(8):
        outs.append(jnp.stack([s[i] for s in st_p]))
        outs.append(win_out[i - 4] if i in (4, 5) else jnp.stack([s[i] for s in st_s]))
    return tuple(outs)
```

```python
import functools

import jax
import jax.numpy as jnp
from jax import lax
from jax.experimental import pallas as pl
from jax.experimental.pallas import tpu as pltpu

F32 = jnp.float32
BF16 = jnp.bfloat16

HEAD_DIM = 64
N_KV = 2
GQA = 4
N_HEADS = N_KV * GQA
ROPE_DIM = 16
ROPE_THETA = 500000.0
L_CMP = 32
L_SLC = 64
L_SLC_SHIFT = 6
N_SEL = 16
WINDOW = 512
Q_BLOCK = 128
PAGE_SIZE = 128
POOL_WINDOWS = (2, 4, 8, 16)
CONV_K = 31
EPS = 1e-6
NEG = -1e30
FORCE_SCORE = 1e4

LANES = 128
SUBLANES = 8
VMEM_LIMIT_BYTES = 56 * 1024 * 1024

NSA_CHUNK_KEYS = 512
NSA_PROMPT_BLOCKS = 2
NSA_SAMPLE_SEQS = 4
NSA_ROW_TILE = 64
HALO = 32
KV_W = N_KV * HEAD_DIM


def _params(*sem):
    return pltpu.CompilerParams(dimension_semantics=sem, vmem_limit_bytes=VMEM_LIMIT_BYTES)


def _modnorm(x, g, sc, sh):
    ms = jnp.mean(x * x, axis=-1, keepdims=True)
    return (x * lax.rsqrt(ms + EPS) * g) * (1.0 + sc) + sh


def _sigmoid(x):
    return 1.0 / (1.0 + jnp.exp(-x))


def _dot_nt(a, b):
    return lax.dot_general(a, b, (((1,), (1,)), ((), ())), preferred_element_type=F32)


def _masked_softmax(s, mask):
    sm = jnp.where(mask, s, NEG)
    e = jnp.exp(sm - jnp.max(sm, axis=-1, keepdims=True))
    return jnp.where(mask, e / jnp.sum(e, axis=-1, keepdims=True), 0.0)


def _ada_kernel(c_ref, w_ref, b_ref, o_ref):
    c = c_ref[...]
    a = (c * _sigmoid(c)).astype(BF16)
    o_ref[...] = jnp.dot(a, w_ref[...].astype(BF16), preferred_element_type=F32) + b_ref[...]


def _ada_all(c_all, w_ada, b_ada):
    depth, d, n = w_ada.shape
    tn = 1536
    assert n % tn == 0
    nb = c_all.shape[0]
    return pl.pallas_call(
        _ada_kernel,
        grid=(depth, n // tn),
        in_specs=[
            pl.BlockSpec((nb, d), lambda l, j: (0, 0)),
            pl.BlockSpec((None, d, tn), lambda l, j: (l, 0, j)),
            pl.BlockSpec((None, 1, tn), lambda l, j: (l, 0, j)),
        ],
        out_specs=pl.BlockSpec((None, nb, tn), lambda l, j: (l, 0, j)),
        out_shape=jax.ShapeDtypeStruct((depth, nb, n), F32),
        compiler_params=_params("parallel", "parallel"),
        name="ada_mod",
    )(c_all, w_ada, b_ada.reshape(depth, 1, n))


def _inproj_kernel(x_ref, sc_ref, sh_ref, g_ref, w_ref, cos_ref, sa_ref, sb_ref, wck_ref, wcv_ref, *outs,
                   w_pool, w_conv, with_cmp):
    (up_ref, uc_ref, q_ref, kc_ref, vc_ref, ks_ref, vs_ref, kw_ref, vw_ref, gn_ref) = outs[:10]
    h = _modnorm(x_ref[0], g_ref[...], sc_ref[0], sh_ref[0]).astype(BF16)
    y = jnp.dot(h, w_ref[...], preferred_element_type=F32)
    cos, sa, sb = cos_ref[...], sa_ref[...], sb_ref[...]

    def rope(t):
        return t * cos + pltpu.roll(t, LANES - ROPE_DIM // 2, 1) * sa + pltpu.roll(t, ROPE_DIM // 2, 1) * sb

    o = 0
    up_ref[0] = y[:, o:o + w_pool]
    o += w_pool
    uc_ref[0] = y[:, o:o + 2 * w_conv]
    o += 2 * w_conv
    for j in range(N_HEADS * HEAD_DIM // LANES):
        q_ref[0, :, j * LANES:(j + 1) * LANES] = rope(y[:, o:o + LANES]).astype(q_ref.dtype)
        o += LANES
    kc = rope(y[:, o:o + KV_W])
    vc = y[:, o + KV_W:o + 2 * KV_W]
    ks = rope(y[:, o + 2 * KV_W:o + 3 * KV_W])
    vs = y[:, o + 3 * KV_W:o + 4 * KV_W]
    kw = rope(y[:, o + 4 * KV_W:o + 5 * KV_W])
    vw = y[:, o + 5 * KV_W:o + 6 * KV_W]
    o += 6 * KV_W
    kc_ref[0], vc_ref[0], ks_ref[0], vs_ref[0], kw_ref[0], vw_ref[0] = kc, vc, ks, vs, kw, vw
    gn_ref[0] = y[:, o:o + LANES]
    if with_cmp:
        ck_ref, cv_ref = outs[10:12]
        tm = kc.shape[0]
        ck_ref[0] = jnp.sum(kc.reshape(tm // L_CMP, L_CMP, KV_W) * wck_ref[...][None], axis=1)
        cv_ref[0] = jnp.sum(vc.reshape(tm // L_CMP, L_CMP, KV_W) * wcv_ref[...][None], axis=1)
        for ref, val in zip(outs[12:], (ks, vs, kw, vw)):
            ref[0] = val.astype(BF16)


def _mod_spec(mods, col, tm, d):
    if mods.shape[1] == 1:
        return pl.BlockSpec((1, 1, d), lambda bi, i: (bi, 0, col))
    return pl.BlockSpec((1, tm, d), lambda bi, i: (bi, i, col))


def _inproj(x, mods, g, w_a, l, tabs, wck, wcv, *, w_pool, w_conv, with_cmp, q_dtype, tm):
    b, t, d = x.shape
    n = w_a.shape[-1]
    tm = min(tm, t)
    assert t % tm == 0 and (tm % L_CMP == 0 or not with_cmp)
    tab_spec = pl.BlockSpec((tm, LANES), lambda bi, i: (i, 0))
    row = lambda w: pl.BlockSpec((1, tm, w), lambda bi, i: (bi, i, 0))
    widths = [w_pool, 2 * w_conv, N_HEADS * HEAD_DIM] + [KV_W] * 6 + [LANES]
    dtypes = [F32, F32, q_dtype] + [F32] * 7
    out_specs = [row(w) for w in widths]
    out_shape = [jax.ShapeDtypeStruct((b, t, w), dt) for w, dt in zip(widths, dtypes)]
    if with_cmp:
        out_specs += [pl.BlockSpec((1, tm // L_CMP, KV_W), lambda bi, i: (bi, i, 0))] * 2 + [row(KV_W)] * 4
        out_shape += ([jax.ShapeDtypeStruct((b, t // L_CMP, KV_W), F32)] * 2
                      + [jax.ShapeDtypeStruct((b, t, KV_W), BF16)] * 4)
    return pl.pallas_call(
        functools.partial(_inproj_kernel, w_pool=w_pool, w_conv=w_conv, with_cmp=with_cmp),
        grid=(b, t // tm),
        in_specs=[
            row(d), _mod_spec(mods, 1, tm, d), _mod_spec(mods, 0, tm, d),
            pl.BlockSpec((None, 1, d), lambda bi, i: (l, 0, 0)),
            pl.BlockSpec((None, d, n), lambda bi, i: (l, 0, 0)),
            tab_spec, tab_spec, tab_spec,
            pl.BlockSpec((None, L_CMP, KV_W), lambda bi, i: (l, 0, 0)),
            pl.BlockSpec((None, L_CMP, KV_W), lambda bi, i: (l, 0, 0)),
        ],
        out_specs=out_specs,
        out_shape=out_shape,
        compiler_params=_params("parallel", "parallel"),
        name="in_proj",
    )(x, mods, mods, g, w_a, *tabs, wck, wcv)


def _pool_means(ext_ref, lvl_ref, base, rows, pos0):
    assert POOL_WINDOWS == (2, 4, 8, 16) and base == 4 * SUBLANES
    n = base + rows
    lane = lax.broadcasted_iota(jnp.int32, (rows, LANES), 1)
    pos = lax.broadcasted_iota(jnp.int32, (rows, LANES), 0) + pos0
    low = lane < POOL_WINDOWS_GW
    means = []
    for tile in range(len(POOL_WINDOWS) // 2):
        cols = slice(tile * LANES, (tile + 1) * LANES)
        w_small, w_big = POOL_WINDOWS[2 * tile], POOL_WINDOWS[2 * tile + 1]
        sums = {}
        src, w, lo = None, 1, 0
        while w < w_big:
            lo += SUBLANES
            if src is None:
                cur = ext_ref[lo:n, cols] + ext_ref[lo - w:n - w, cols]
            else:
                cur = lvl_ref[src, lo:n] + lvl_ref[src, lo - w:n - w]
            w *= 2
            sums[w] = cur[base - lo:]
            if w < w_big:
                src = 0 if src != 0 else 1
                lvl_ref[src, lo:n] = cur
        cnt = jnp.minimum(jnp.where(low, w_small, w_big), pos + 1).astype(F32)
        means.append(jnp.where(low, sums[w_small], sums[w_big]) / cnt)
    return jnp.concatenate(means, axis=1)


POOL_WINDOWS_GW = 64


def _layernorm_silu(y, g, b):
    mu = jnp.mean(y, axis=-1, keepdims=True)
    yc = y - mu
    var = jnp.mean(yc * yc, axis=-1, keepdims=True)
    z = yc * lax.rsqrt(var + EPS) * g + b
    return z * _sigmoid(z)


def _mix_prompt_kernel(up_ref, uph_ref, uc_ref, uch_ref, wpool_ref, spool_ref, wdw_ref, bdw_ref, gln_ref, bln_ref,
                       a_ref, b_ref, cst_ref, pext_ref, cext_ref, part_ref, lvl_ref, *, tt, w_conv):
    i = pl.program_id(1)
    first = i == 0
    u = up_ref[0]
    pext_ref[0:HALO] = jnp.where(first, 0.0, uph_ref[0])
    pext_ref[HALO:HALO + tt] = u
    d = _pool_means(pext_ref, lvl_ref, HALO, tt, i * tt) - u
    z = jnp.dot(d.astype(BF16), wpool_ref[...], preferred_element_type=F32) * spool_ref[...]
    a_ref[0] = z.astype(a_ref.dtype)
    hc = uch_ref[0]
    cext_ref[0:HALO] = jnp.where(first, 0.0, hc[:, :w_conv] * _sigmoid(hc[:, w_conv:]))
    uc = uc_ref[0]
    cext_ref[HALO:HALO + tt] = uc[:, :w_conv] * _sigmoid(uc[:, w_conv:])
    off = HALO - (CONV_K - 1)
    acc = None
    for m in range(SUBLANES):
        n_rows = tt + SUBLANES * (-(-(off + m) // SUBLANES))
        part = None
        for k in range(m, CONV_K, SUBLANES):
            assert k - m + n_rows <= HALO + tt
            term = wdw_ref[k:k + 1, :] * cext_ref[k - m:k - m + n_rows]
            part = term if part is None else part + term
        part_ref[m, 0:n_rows] = part
        shifted = part_ref[m, off + m:off + m + tt]
        acc = shifted if acc is None else acc + shifted
    b_ref[0] = _layernorm_silu(acc + bdw_ref[...], gln_ref[...], bln_ref[...]).astype(b_ref.dtype)
    cst_ref[0] = cext_ref[tt:tt + HALO]


def _mix_prompt(up, uc, wpool_bd, spool, wdw, bdw, gln, bln, l, *, tt):
    b, t, w_pool = up.shape
    w_conv = uc.shape[-1] // 2
    tt = min(tt, t)
    assert t % tt == 0 and tt % HALO == 0
    r = tt // HALO
    halo = lambda w: pl.BlockSpec((1, HALO, w), lambda bi, i: (bi, jnp.maximum(i * r - 1, 0), 0))
    vec = lambda w: pl.BlockSpec((None, 1, w), lambda bi, i: (l, 0, 0))
    return pl.pallas_call(
        functools.partial(_mix_prompt_kernel, tt=tt, w_conv=w_conv),
        grid=(b, t // tt),
        in_specs=[
            pl.BlockSpec((1, tt, w_pool), lambda bi, i: (bi, i, 0)), halo(w_pool),
            pl.BlockSpec((1, tt, 2 * w_conv), lambda bi, i: (bi, i, 0)), halo(2 * w_conv),
            pl.BlockSpec((None, w_pool, w_pool), lambda bi, i: (l, 0, 0)), vec(w_pool),
            pl.BlockSpec((None, CONV_K, w_conv), lambda bi, i: (l, 0, 0)), vec(w_conv), vec(w_conv), vec(w_conv),
        ],
        out_specs=[
            pl.BlockSpec((1, tt, w_pool), lambda bi, i: (bi, i, 0)),
            pl.BlockSpec((1, tt, w_conv), lambda bi, i: (bi, i, 0)),
            pl.BlockSpec((1, HALO, w_conv), lambda bi, i: (bi, 0, 0)),
        ],
        out_shape=[
            jax.ShapeDtypeStruct((b, t, w_pool), BF16),
            jax.ShapeDtypeStruct((b, t, w_conv), BF16),
            jax.ShapeDtypeStruct((b, HALO, w_conv), F32),
        ],
        scratch_shapes=[pltpu.VMEM((HALO + tt, w_pool), F32), pltpu.VMEM((HALO + tt, w_conv), F32),
                        pltpu.VMEM((SUBLANES, tt + 2 * SUBLANES, w_conv), F32),
                        pltpu.VMEM((2, HALO + tt, LANES), F32)],
        compiler_params=_params("parallel", "arbitrary"),
        name="mix_prompt",
    )(up, up, uc, uc, wpool_bd, spool, wdw, bdw, gln, bln)


def _mix_sample_kernel(pst_ref, up_ref, cst_ref, uc_ref, wpool_ref, spool_ref, wdw_ref, bdw_ref, gln_ref, bln_ref,
                       a_ref, b_ref, glu_ref, *, w_conv):
    n_p, n_c, ts = pst_ref.shape[0], cst_ref.shape[0], up_ref.shape[0]
    prow = lambda r: pst_ref[r] if r < n_p else up_ref[r - n_p]
    low = lax.broadcasted_iota(jnp.int32, (up_ref.shape[1], LANES), 1) < POOL_WINDOWS_GW
    glu = []
    for t in range(ts):
        uc = uc_ref[t]
        glu.append(uc[:, :w_conv] * _sigmoid(uc[:, w_conv:]))
        glu_ref[t] = glu[t]
    crow = lambda r: cst_ref[r] if r < n_c else glu[r - n_c]
    for t in range(ts):
        u = up_ref[t]
        means = None
        acc = u
        sums = {1: acc}
        for s in range(1, max(POOL_WINDOWS)):
            acc = acc + prow(n_p + t - s)
            sums[s + 1] = acc
        tiles = []
        for tile in range(len(POOL_WINDOWS) // 2):
            cols = slice(tile * LANES, (tile + 1) * LANES)
            w_small, w_big = POOL_WINDOWS[2 * tile], POOL_WINDOWS[2 * tile + 1]
            tiles.append(jnp.where(low, sums[w_small][:, cols] / float(w_small),
                                   sums[w_big][:, cols] / float(w_big)))
        means = jnp.concatenate(tiles, axis=1)
        z = jnp.dot((means - u).astype(BF16), wpool_ref[...], preferred_element_type=F32) * spool_ref[...]
        a_ref[t] = z.astype(a_ref.dtype)
        acc = wdw_ref[0:1, :] * crow(t + n_c - (CONV_K - 1))
        for k in range(1, CONV_K):
            acc = acc + wdw_ref[k:k + 1, :] * crow(t + n_c - (CONV_K - 1) + k)
        b_ref[t] = _layernorm_silu(acc + bdw_ref[...], gln_ref[...], bln_ref[...]).astype(b_ref.dtype)


def _mix_sample(pst_t, up_t, cst_t, uc_t, wpool_bd, spool, wdw, bdw, gln, bln, l):
    ts, bd, w_pool = up_t.shape
    w_conv = uc_t.shape[-1] // 2
    full = lambda a: pl.BlockSpec(a.shape, lambda i: (0,) * a.ndim)
    vec = lambda w: pl.BlockSpec((None, 1, w), lambda i: (l, 0, 0))
    return pl.pallas_call(
        functools.partial(_mix_sample_kernel, w_conv=w_conv),
        grid=(1,),
        in_specs=[
            full(pst_t), full(up_t), full(cst_t), full(uc_t),
            pl.BlockSpec((None, w_pool, w_pool), lambda i: (l, 0, 0)), vec(w_pool),
            pl.BlockSpec((None, CONV_K, w_conv), lambda i: (l, 0, 0)), vec(w_conv), vec(w_conv), vec(w_conv),
        ],
        out_specs=[pl.BlockSpec((ts, bd, w_pool), lambda i: (0, 0, 0)),
                   pl.BlockSpec((ts, bd, w_conv), lambda i: (0, 0, 0)),
                   pl.BlockSpec((ts, bd, w_conv), lambda i: (0, 0, 0))],
        out_shape=[jax.ShapeDtypeStruct((ts, bd, w_pool), BF16),
                   jax.ShapeDtypeStruct((ts, bd, w_conv), BF16),
                   jax.ShapeDtypeStruct((ts, bd, w_conv), F32)],
        compiler_params=_params("arbitrary"),
        name="mix_sample",
    )(pst_t, up_t, cst_t, uc_t, wpool_bd, spool, wdw, bdw, gln, bln)


def _topk_masks(scores, n_sel):
    sts = tuple(s.T for s in scores)
    cand = lax.broadcasted_iota(jnp.int32, sts[0].shape, 0).astype(F32)

    def body(_, carry):
        out = []
        for st, sel in carry:
            m = jnp.max(st, axis=0, keepdims=True)
            first = jnp.min(jnp.where(st == m, cand, float(LANES)), axis=0, keepdims=True)
            hit = cand == first
            out.append((jnp.where(hit, -jnp.inf, st), jnp.where(hit, 1.0, sel)))
        return tuple(out)

    res = lax.fori_loop(0, n_sel, body, tuple((st, jnp.zeros(st.shape, F32)) for st in sts))
    return [sel.T for _, sel in res]


def _head_rows(tile, src_half, dst_half, lane):
    if src_half != dst_half:
        tile = pltpu.roll(tile, HEAD_DIM, 1)
    return jnp.where((lane >= dst_half * HEAD_DIM) & (lane < (dst_half + 1) * HEAD_DIM), tile, 0.0)


def _nsa_prompt_multi_kernel(q_ref, gn_ref, ck_ref, cv_ref, ks_ref, vs_ref, kw_ref, vw_ref, eneg_ref, o_ref,
                             qa_scr, s_scr, p_scr, sw_scr, pw_scr, *, ck_keys, n_sub):
    step = pl.program_id(1)
    qb = Q_BLOCK
    rows = GQA * qb
    scale = HEAD_DIM ** -0.5
    n_cmp = ck_ref.shape[1]
    n_slc = n_cmp // 2
    n_wchunks = WINDOW // qb + 1
    wk = n_wchunks * qb
    lane = lax.broadcasted_iota(jnp.int32, (qb, LANES), 1)
    chains = [(u, k) for u in range(n_sub) for k in range(N_KV)]

    def qblk(u):
        return step * n_sub + u

    def qpos(u, shape, row0=0):
        return qblk(u) * qb + ((row0 + lax.broadcasted_iota(jnp.int32, shape, 0)) & (qb - 1))

    def qrows(u, ref, lo, hi):
        return ref[0, u * qb:(u + 1) * qb, lo:hi]

    st = {}

    def phase1(u):
        cur = qpos(u, lane.shape) >> L_SLC_SHIFT
        valid_s = lane <= cur
        forced = (lane == 0) | (lane == cur) | (lane == cur - 1)
        st[u, "valid"] = valid_s
        for k in range(N_KV):
            qh = []
            for g in range(GQA):
                h = k * GQA + g
                tile = qrows(u, q_ref, (h // 2) * LANES, (h // 2 + 1) * LANES).astype(F32) * scale
                qh.append(_head_rows(tile, h % 2, k, lane))
            qk = jnp.concatenate(qh, axis=0).astype(BF16)
            st[u, k, "qk"] = qk
            s_c = _dot_nt(qk, ck_ref[0])
            yield
            cl = lax.broadcasted_iota(jnp.int32, (rows, n_cmp), 1)
            c_end = jnp.where(cl < n_slc, cl * L_SLC + L_CMP - 1, (cl - n_slc) * L_SLC + L_SLC - 1)
            p_c = _masked_softmax(s_c, c_end <= qpos(u, s_c.shape))
            yield
            st[u, k, "o_c"] = jnp.dot(p_c.astype(BF16), cv_ref[0], preferred_element_type=F32)
            ps = p_c[0:qb]
            for g in range(1, GQA):
                ps = ps + p_c[g * qb:(g + 1) * qb]
            imp = ps[:, :n_slc] + ps[:, n_slc:]
            if n_slc < LANES:
                imp = jnp.concatenate([imp, jnp.zeros((qb, LANES - n_slc), F32)], axis=1)
            score = jnp.where(valid_s, jnp.where(forced, FORCE_SCORE, imp), -1.0)
            st[u, k, "score"] = jnp.where(lane < n_slc, score, -jnp.inf)
            yield

    _lockstep(phase1(u) for u in range(n_sub))

    for u in range(n_sub):
        for k, sel in enumerate(_topk_masks([st[u, k, "score"] for k in range(N_KV)], min(N_SEL, n_slc))):
            notsel = (1.0 - jnp.where(st[u, "valid"], sel, 0.0)).astype(BF16)
            qa_scr[u * N_KV + k] = jnp.concatenate([st[u, k, "qk"], jnp.concatenate([notsel] * GQA, axis=0)], axis=1)

    own = lax.broadcasted_iota(jnp.int32, (ck_keys, LANES), 1) < HEAD_DIM

    def scores(c, buf):
        start = pl.multiple_of(c * ck_keys, ck_keys)
        k_aug = jnp.concatenate([ks_ref[0, pl.ds(start, ck_keys), :], eneg_ref[pl.ds(start, ck_keys), :]], axis=1)
        for n in range(len(chains)):
            s_scr[buf, n] = _dot_nt(qa_scr[n], k_aug)

    def probs(c, buf, ms, causal):
        new_m, alphas = [], []
        for n, (u, k) in enumerate(chains):
            m_tiles = []
            for r in range(0, rows, NSA_ROW_TILE):
                rs = slice(r, r + NSA_ROW_TILE)
                s = s_scr[buf, n, rs]
                if causal:
                    kpos = c * ck_keys + lax.broadcasted_iota(jnp.int32, s.shape, 1)
                    s = jnp.where(kpos <= qpos(u, s.shape, r), s, NEG)
                m_new = jnp.maximum(ms[n][rs], jnp.max(s, axis=-1, keepdims=True))
                p_scr[buf, n, rs] = jnp.exp(s - m_new).astype(BF16)
                m_tiles.append(m_new)
            m_new = jnp.concatenate(m_tiles, axis=0)
            alphas.append(jnp.exp(ms[n] - m_new))
            new_m.append(m_new)
        return new_m, alphas

    def values(c, buf, alphas, accs):
        start = pl.multiple_of(c * ck_keys, ck_keys)
        v = vs_ref[0, pl.ds(start, ck_keys), :]
        v_one = [jnp.where(own, v, 1.0), jnp.where(own, 1.0, v)]
        return [alphas[n] * accs[n] + jnp.dot(p_scr[buf, n], v_one[k], preferred_element_type=F32)
                for n, (u, k) in enumerate(chains)]

    def pair(c, carry, causal, last):
        ms, alphas, accs = carry
        accs = values(jnp.maximum(c - 1, 0), 1, alphas, accs)
        ms, alphas = probs(c, 0, ms, causal)
        scores(c + 1, 1)
        accs = values(c, 0, alphas, accs)
        ms, alphas = probs(c + 1, 1, ms, causal)
        if last:
            accs = values(c + 1, 1, alphas, accs)
        else:
            scores(c + 2, 0)
        return ms, alphas, accs

    n_pairs = (qblk(n_sub - 1) * qb + qb + 2 * ck_keys - 1) // (2 * ck_keys)
    p_scr[1] = jnp.zeros(p_scr.shape[1:], BF16)
    scores(0, 0)
    nc = len(chains)
    carry = ([jnp.full((rows, 1), -jnp.inf, F32)] * nc, [jnp.ones((rows, 1), F32)] * nc,
             [jnp.zeros((rows, LANES), F32)] * nc)
    carry = lax.fori_loop(0, n_pairs - 1, lambda j, cr: pair(2 * j, cr, False, False), carry)
    c_last = 2 * (n_pairs - 1)

    def last_single(cr):
        ms, alphas, accs = cr
        accs = values(jnp.maximum(c_last - 1, 0), 1, alphas, accs)
        ms, alphas = probs(c_last, 0, ms, True)
        return values(c_last, 0, alphas, accs)

    n_chunks = (qblk(n_sub - 1) * qb + qb + ck_keys - 1) // ck_keys
    accs = lax.cond(n_chunks == c_last + 1, last_single, lambda cr: pair(c_last, cr, True, True)[2], carry)
    o_ss = [acc / pltpu.roll(acc, HEAD_DIM, 1) for acc in accs]

    def tail(u):
        i = qblk(u)
        kws, vws = [], []
        for dj in range(n_wchunks):
            st0 = pl.multiple_of(jnp.maximum(i - (n_wchunks - 1) + dj, 0) * qb, qb)
            kws.append(kw_ref[0, pl.ds(st0, qb), :])
            vws.append(vw_ref[0, pl.ds(st0, qb), :])
        kw_blk, vw_blk = jnp.concatenate(kws, axis=0), jnp.concatenate(vws, axis=0)
        kwpos = (i - (n_wchunks - 1)) * qb + lax.broadcasted_iota(jnp.int32, (qb, wk), 1)
        qp = qpos(u, (qb, wk))
        wbias = jnp.where((kwpos <= qp) & (kwpos > qp - WINDOW) & (kwpos >= 0), 0.0, NEG)
        own_w = lax.broadcasted_iota(jnp.int32, (wk, LANES), 1) < HEAD_DIM
        vw_one = [jnp.where(own_w, vw_blk, 1.0), jnp.where(own_w, 1.0, vw_blk)]
        for k in range(N_KV):
            sw_scr[u * N_KV + k] = _dot_nt(st[u, k, "qk"], kw_blk)
        yield
        o_ws = []
        for k in range(N_KV):
            n = u * N_KV + k
            for r in range(0, rows, NSA_ROW_TILE):
                rs = slice(r, r + NSA_ROW_TILE)
                s = sw_scr[n, rs] + wbias[r % qb:r % qb + NSA_ROW_TILE]
                pw_scr[n, rs] = jnp.exp(s - jnp.max(s, axis=-1, keepdims=True)).astype(BF16)
            yield
            acc = jnp.dot(pw_scr[n], vw_one[k], preferred_element_type=F32)
            o_ws.append(acc / pltpu.roll(acc, HEAD_DIM, 1))
            yield
        sig = _sigmoid(qrows(u, gn_ref, 0, LANES))
        out_tiles = [None] * (N_HEADS // 2)
        for k in range(N_KV):
            o_c, o_s, o_w = st[u, k, "o_c"], o_ss[u * N_KV + k], o_ws[k]
            for g in range(GQA):
                h = k * GQA + g
                col = 3 * h
                r = slice(g * qb, (g + 1) * qb)
                o = (sig[:, col:col + 1] * o_c[r] + sig[:, col + 1:col + 2] * o_s[r]
                     + sig[:, col + 2:col + 3] * o_w[r])
                o = _head_rows(o, k, h % 2, lane)
                out_tiles[h // 2] = o if out_tiles[h // 2] is None else out_tiles[h // 2] + o
            yield
        for j, tile in enumerate(out_tiles):
            o_ref[0, u * qb:(u + 1) * qb, j * LANES:(j + 1) * LANES] = tile.astype(o_ref.dtype)

    _lockstep(tail(u) for u in range(n_sub))


def _nsa_prompt(q, gn, ckp, cvp, ks, vs, kw, vw, eneg):
    b, t, _ = q.shape
    assert t % Q_BLOCK == 0 and t // L_SLC <= LANES
    ck_keys = min(NSA_CHUNK_KEYS, t // 2)
    assert t % (2 * ck_keys) == 0
    rows = GQA * Q_BLOCK
    n_sub = NSA_PROMPT_BLOCKS
    assert (2 * ck_keys) % (n_sub * Q_BLOCK) == 0 and t % (n_sub * Q_BLOCK) == 0
    nc = n_sub * N_KV
    qspec = lambda w: pl.BlockSpec((1, n_sub * Q_BLOCK, w), lambda bi, i: (bi, i, 0))
    seq = lambda n: pl.BlockSpec((1, n, KV_W), lambda bi, i: (bi, 0, 0))
    return pl.pallas_call(
        functools.partial(_nsa_prompt_multi_kernel, ck_keys=ck_keys, n_sub=n_sub),
        scratch_shapes=[pltpu.VMEM((nc, rows, 2 * LANES), BF16), pltpu.VMEM((2, nc, rows, ck_keys), F32),
                        pltpu.VMEM((2, nc, rows, ck_keys), BF16),
                        pltpu.VMEM((nc, rows, WINDOW + Q_BLOCK), F32),
                        pltpu.VMEM((nc, rows, WINDOW + Q_BLOCK), BF16)],
        grid=(b, t // (n_sub * Q_BLOCK)),
        in_specs=[qspec(N_HEADS * HEAD_DIM), qspec(LANES), seq(t // L_CMP), seq(t // L_CMP),
                  seq(t), seq(t), seq(t), seq(t), pl.BlockSpec((t, LANES), lambda bi, i: (0, 0))],
        out_specs=qspec(N_HEADS * HEAD_DIM),
        out_shape=jax.ShapeDtypeStruct((b, t, N_HEADS * HEAD_DIM), BF16),
        compiler_params=_params("parallel", "parallel"),
        name="nsa_prompt",
    )(q, gn, ckp, cvp, ks, vs, kw, vw, eneg)


def _nsa_sample_kernel(pt_ref, q_ref, gn_ref, kcn_ref, vcn_ref, ksn_ref, vsn_ref, kwn_ref, vwn_ref, wk_ref, wv_ref,
                       selk_ref, selv_ref, eneg_ref, *rest, n_pages, past_len, ts, page0):
    caches = rest[2:6]
    o_ref, nwk_ref, nwv_ref = rest[6:9]
    (past_buf, sems, tail_s, q_s, sc_s, g_s, o_s) = rest[9:]
    n_seq = q_ref.shape[0]
    b = pl.program_id(0)
    slot = b & 1
    scale = HEAD_DIM ** -0.5
    wl = wk_ref.shape[2]
    n_keys = past_len + LANES
    n_slc = -(-(past_len + ts) // L_SLC)
    rows = GQA * N_KV * ts
    grp = N_KV * ts
    lane_t = lax.broadcasted_iota(jnp.int32, (ts, LANES), 1)

    def page_copy(step, dst_slot, u, j, p):
        src = caches[j].at[page0 + pt_ref[(step * n_seq + u) * n_pages + p]]
        dst = past_buf.at[dst_slot, u, j, :, pl.ds(p * PAGE_SIZE, PAGE_SIZE)]
        return pltpu.make_async_copy(src, dst, sems.at[dst_slot])

    def all_pages(step, dst_slot, op):
        for u in range(n_seq):
            for j in range(4):
                for p in range(n_pages):
                    op(page_copy(step, dst_slot, u, j, p))

    @pl.when(b == 0)
    def _():
        all_pages(0, 0, lambda cp: cp.start())

    @pl.when(b + 1 < pl.num_programs(0))
    def _():
        all_pages(b + 1, 1 - slot, lambda cp: cp.start())

    all_pages(b, slot, lambda cp: cp.wait())

    def qpos(shape):
        return past_len + (lax.broadcasted_iota(jnp.int32, shape, 0) & (ts - 1))

    lane_g = lax.broadcasted_iota(jnp.int32, (grp, LANES), 1)
    blk = lane_g >> 1
    is_blk = ((lane_g & 1) == 0) & (blk < n_slc)
    cur = qpos(lane_g.shape) >> L_SLC_SHIFT
    valid_s = blk <= cur
    forced = (blk == 0) | (blk == cur) | (blk == cur - 1)

    def front(u, out):
        def new_cols(j, new_ref):
            tail_s[u, j] = jnp.zeros(tail_s.shape[2:], F32)
            tail_s[u, j, 0:ts] = new_ref[u]
            return tail_s[u, j].T

        kc_p, vc_p, ks_p, vs_p = (past_buf[slot, u, j].astype(BF16) for j in range(4))
        kc_n, vc_n, ks_n, vs_n = (new_cols(j, r).astype(BF16)
                                  for j, r in enumerate((kcn_ref, vcn_ref, ksn_ref, vsn_ref)))

        kw_all = jnp.concatenate([wk_ref[u], new_cols(4, kwn_ref)], axis=1)
        vw_all = jnp.concatenate([wv_ref[u], new_cols(5, vwn_ref)], axis=1)
        nwk_ref[u] = kw_all[:, ts:ts + wl]
        nwv_ref[u] = vw_all[:, ts:ts + wl]
        yield

        for g in range(GQA):
            for k in range(N_KV):
                h = k * GQA + g
                tile = q_ref[u, :, (h // 2) * LANES:(h // 2 + 1) * LANES] * scale
                q_s[u, (g * N_KV + k) * ts:(g * N_KV + k + 1) * ts] = _head_rows(tile, h % 2, k, lane_t)
        qa = q_s[u].astype(BF16)

        def compress(past, new, sel_ref):
            return (jnp.dot(past, sel_ref[0:past_len], preferred_element_type=F32)
                    + jnp.dot(new, sel_ref[past_len:n_keys], preferred_element_type=F32)).astype(BF16)

        ck_t = compress(kc_p, kc_n, selk_ref)
        yield
        cv_t = compress(vc_p, vc_n, selv_ref)
        yield
        s_c = jnp.dot(qa, ck_t, preferred_element_type=F32)
        yield
        cl = lax.broadcasted_iota(jnp.int32, s_c.shape, 1)
        p_c = _masked_softmax(s_c, (cl + 1) * L_CMP - 1 <= qpos(s_c.shape))
        yield
        o_c = _dot_nt(p_c.astype(BF16), cv_t)
        ps = p_c[0:grp]
        for g in range(1, GQA):
            ps = ps + p_c[g * grp:(g + 1) * grp]
        imp = ps + pltpu.roll(ps, LANES - 1, 1)

        score = jnp.where(valid_s, jnp.where(forced, FORCE_SCORE, imp), -1.0)
        sc_s[u * grp:(u + 1) * grp] = jnp.where(is_blk, score, -jnp.inf)
        out[u] = (qa, o_c, ks_p, ks_n, vs_p, vs_n, kw_all, vw_all)

    sc_s[...] = jnp.zeros(sc_s.shape, F32)
    fronts = {}
    _lockstep(front(u, fronts) for u in range(n_seq))
    sel = _topk_masks([sc_s[...]], min(N_SEL, n_slc))[0]
    _lockstep(_nsa_sample_back(u, fronts[u], sel[u * grp:(u + 1) * grp], is_blk & valid_s, qpos, gn_ref, eneg_ref,
                               o_ref, g_s, o_s, past_len=past_len, ts=ts, wl=wl) for u in range(n_seq))


def _lockstep(stages):
    stages = list(stages)
    while stages:
        for gen in list(stages):
            try:
                next(gen)
            except StopIteration:
                stages.remove(gen)


def _nsa_sample_back(u, front, sel, selectable, qpos, gn_ref, eneg_ref, o_ref, g_s, o_s, *, past_len, ts, wl):
    qa, o_c, ks_p, ks_n, vs_p, vs_n, kw_all, vw_all = front
    n_keys = past_len + LANES
    lane_t = lax.broadcasted_iota(jnp.int32, (ts, LANES), 1)
    notsel = (1.0 - jnp.where(selectable, sel, 0.0)).astype(BF16)
    q_aug = jnp.concatenate([qa, jnp.concatenate([notsel] * GQA, axis=0)], axis=1)

    s_s = jnp.concatenate(
        [jnp.dot(q_aug, jnp.concatenate([ks_p, eneg_ref[:, 0:past_len]], axis=0), preferred_element_type=F32),
         jnp.dot(q_aug, jnp.concatenate([ks_n, eneg_ref[:, past_len:n_keys]], axis=0), preferred_element_type=F32)],
        axis=1)
    s_w = jnp.dot(qa, kw_all.astype(BF16), preferred_element_type=F32)
    yield
    kpos = lax.broadcasted_iota(jnp.int32, s_s.shape, 1)
    s_s = jnp.where(kpos <= qpos(s_s.shape), s_s, NEG)
    e = jnp.exp(s_s - jnp.max(s_s, axis=-1, keepdims=True))
    yield
    p_s = (e / jnp.sum(e, axis=-1, keepdims=True)).astype(BF16)
    yield
    o_sel = _dot_nt(p_s[:, 0:past_len], vs_p) + _dot_nt(p_s[:, past_len:n_keys], vs_n)
    yield
    kwpos = past_len - wl + lax.broadcasted_iota(jnp.int32, s_w.shape, 1)
    qp = qpos(s_w.shape)
    okw = (kwpos <= qp) & (kwpos > qp - WINDOW) & (kwpos >= 0)
    p_w = _masked_softmax(s_w, okw)
    yield
    o_w = _dot_nt(p_w.astype(BF16), vw_all.astype(BF16))
    yield

    sig = _sigmoid(gn_ref[u])
    o = None
    for j, ob in enumerate((o_c, o_sel, o_w)):
        for g in range(GQA):
            for k in range(N_KV):
                col = 3 * (k * GQA + g) + j
                g_s[u, j, (g * N_KV + k) * ts:(g * N_KV + k + 1) * ts] = jnp.broadcast_to(sig[:, col:col + 1],
                                                                                        (ts, LANES))
        o = g_s[u, j] * ob if o is None else o + g_s[u, j] * ob
    o_s[u] = o
    for j in range(N_HEADS // 2):
        tile = None
        for h in (2 * j, 2 * j + 1):
            k, g = h // GQA, h % GQA
            piece = _head_rows(o_s[u, (g * N_KV + k) * ts:(g * N_KV + k + 1) * ts], k, h % 2, lane_t)
            tile = piece if tile is None else tile + piece
        o_ref[u, :, j * LANES:(j + 1) * LANES] = tile


def _nsa_sample(page_table, q, gn, new_rows, win_k, win_v, caches, selk, selv, eneg, l, new_win, *, n_pool, depth):
    bd, ts, _ = q.shape
    n_pages = page_table.shape[1]
    past_len = n_pages * PAGE_SIZE
    wl = win_k.shape[2]
    n_keys = past_len + LANES
    assert ts <= LANES and -(-(past_len + ts) // L_SLC) * 2 <= LANES and eneg.shape == (LANES, n_keys)
    rows = GQA * N_KV * ts
    n_seq = NSA_SAMPLE_SEQS
    assert bd % n_seq == 0 and n_seq * N_KV * ts <= LANES
    tok = lambda w: pl.BlockSpec((n_seq, ts, w), lambda b, pt: (b, 0, 0))
    win = pl.BlockSpec((n_seq, KV_W, wl), lambda b, pt: (l * (bd // n_seq) + b, 0, 0))
    sel = pl.BlockSpec((None, n_keys, LANES), lambda b, pt: (l, 0, 0))

    hbm = pl.BlockSpec(memory_space=pl.ANY)
    in_specs = ([tok(N_HEADS * HEAD_DIM), tok(LANES)] + [tok(KV_W)] * 6 + [win, win, sel, sel,
                pl.BlockSpec((LANES, n_keys), lambda b, pt: (0, 0))] + [hbm] * 6)
    grid_spec = pltpu.PrefetchScalarGridSpec(
        num_scalar_prefetch=1,
        grid=(bd // n_seq,),
        in_specs=in_specs,
        out_specs=[tok(N_HEADS * HEAD_DIM), win, win],
        scratch_shapes=[
            pltpu.VMEM((2, n_seq, 4, KV_W, past_len), F32),
            pltpu.SemaphoreType.DMA((2,)),
            pltpu.VMEM((n_seq, 6, LANES, KV_W), F32),
            pltpu.VMEM((n_seq, rows, LANES), F32), pltpu.VMEM((LANES, LANES), F32),
            pltpu.VMEM((n_seq, 3, rows, LANES), F32), pltpu.VMEM((n_seq, rows, LANES), F32),
        ],
    )
    win_shape = jax.ShapeDtypeStruct((depth * bd, KV_W, wl), F32)
    n_fixed = 1 + 2 + 6 + 5
    return pl.pallas_call(
        functools.partial(_nsa_sample_kernel, n_pages=n_pages, past_len=past_len, ts=ts, page0=l * n_pool),
        grid_spec=grid_spec,
        out_shape=[jax.ShapeDtypeStruct((bd, ts, N_HEADS * HEAD_DIM), F32), win_shape, win_shape],
        input_output_aliases={n_fixed: 1, n_fixed + 1: 2},
        compiler_params=_params("arbitrary"),
        name="nsa_sample",
    )(page_table.reshape(-1), q, gn, *new_rows, win_k, win_v, selk, selv, eneg, *new_win, *caches)


def _merge_kernel(x_ref, sc_ref, sh_ref, gt_ref, g_ref, a_ref, b_ref, c_ref, wgm_ref, wp_ref, wc_ref, wa_ref,
                  wo_ref, o_ref):
    x = x_ref[0]
    d = x.shape[-1]
    h = _modnorm(x, g_ref[...], sc_ref[0], sh_ref[0]).astype(BF16)
    branches = (
        jnp.dot(a_ref[0].astype(BF16), wp_ref[...], preferred_element_type=F32),
        jnp.dot(b_ref[0].astype(BF16), wc_ref[...], preferred_element_type=F32),
        jnp.dot(c_ref[0].astype(BF16), wa_ref[...], preferred_element_type=F32),
    )
    merged = None
    for j, br in enumerate(branches):
        gm = _sigmoid(jnp.dot(h, wgm_ref[:, j * d:(j + 1) * d], preferred_element_type=F32))
        merged = gm * br if merged is None else merged + gm * br
    out = jnp.dot(merged.astype(BF16), wo_ref[...], preferred_element_type=F32)
    o_ref[0] = x + gt_ref[0] * out


def _merge(x, mods, g, a, bo, c, wgm, wp, wc, wa, wo, l, *, tm):
    b, t, d = x.shape
    tm = min(tm, t)
    assert t % tm == 0
    row = lambda w: pl.BlockSpec((1, tm, w), lambda bi, i: (bi, i, 0))
    wsp = lambda w: pl.BlockSpec((None,) + w.shape[1:], lambda bi, i: (l, 0, 0))
    return pl.pallas_call(
        _merge_kernel,
        grid=(b, t // tm),
        in_specs=[row(d), _mod_spec(mods, 1, tm, d), _mod_spec(mods, 0, tm, d), _mod_spec(mods, 2, tm, d),
                  pl.BlockSpec((None, 1, d), lambda bi, i: (l, 0, 0)),
                  row(a.shape[-1]), row(bo.shape[-1]), row(c.shape[-1]),
                  wsp(wgm), wsp(wp), wsp(wc), wsp(wa), wsp(wo)],
        out_specs=row(d),
        out_shape=jax.ShapeDtypeStruct((b, t, d), F32),
        compiler_params=_params("parallel", "parallel"),
        name="merge_out",
    )(x, mods, mods, mods, g, a, bo, c, wgm, wp, wc, wa, wo)


def _ffn_kernel(x_ref, sc_ref, sh_ref, gt_ref, g_ref, wgu_ref, wd_ref, gf_ref, *outs, d_ff, n_parts, final):
    x = x_ref[0]
    h = _modnorm(x, g_ref[...], sc_ref[0], sh_ref[0]).astype(BF16)
    part = d_ff // n_parts
    acc = None
    for c in range(n_parts):
        gp = jnp.dot(h, wgu_ref[:, c * part:(c + 1) * part], preferred_element_type=F32)
        up = jnp.dot(h, wgu_ref[:, d_ff + c * part:d_ff + (c + 1) * part], preferred_element_type=F32)
        act = (gp * _sigmoid(gp) * up).astype(BF16)
        dn = jnp.dot(act, wd_ref[c * part:(c + 1) * part, :], preferred_element_type=F32)
        acc = dn if acc is None else acc + dn
    y = x + gt_ref[0] * acc
    outs[0][0] = y
    if final:
        ms = jnp.mean(y * y, axis=-1, keepdims=True)
        outs[1][0] = y * lax.rsqrt(ms + EPS) * gf_ref[...]


def _ffn(x, mods, g, wgu, wd, gf, l, *, tm, final):
    b, t, d = x.shape
    d_ff = wd.shape[1]
    tm = min(tm, t)
    n_parts = 2
    assert t % tm == 0 and d_ff % (n_parts * LANES) == 0
    row = pl.BlockSpec((1, tm, d), lambda bi, i: (bi, i, 0))
    n_out = 2 if final else 1
    return pl.pallas_call(
        functools.partial(_ffn_kernel, d_ff=d_ff, n_parts=n_parts, final=final),
        grid=(b, t // tm),
        in_specs=[row, _mod_spec(mods, 4, tm, d), _mod_spec(mods, 3, tm, d), _mod_spec(mods, 5, tm, d),
                  pl.BlockSpec((None, 1, d), lambda bi, i: (l, 0, 0)),
                  pl.BlockSpec((None, d, 2 * d_ff), lambda bi, i: (l, 0, 0)),
                  pl.BlockSpec((None, d_ff, d), lambda bi, i: (l, 0, 0)),
                  pl.BlockSpec((1, d), lambda bi, i: (0, 0))],
        out_specs=[row] * n_out,
        out_shape=[jax.ShapeDtypeStruct((b, t, d), F32)] * n_out,
        compiler_params=_params("parallel", "parallel"),
        name="ffn",
    )(x, mods, mods, mods, g, wgu, wd, gf)


def _rope_tables(pos):
    half = ROPE_DIM // 2
    inv = ROPE_THETA ** (-jnp.arange(half, dtype=F32) * 2.0 / ROPE_DIM)
    ang = pos.astype(F32)[:, None] * inv[None, :]
    cos, sin = jnp.cos(ang), jnp.sin(ang)
    n = pos.shape[0]
    rest = HEAD_DIM - ROPE_DIM
    c = jnp.concatenate([cos, cos, jnp.ones((n, rest), F32)], axis=1)
    sa = jnp.concatenate([-sin, jnp.zeros((n, half + rest), F32)], axis=1)
    sb = jnp.concatenate([jnp.zeros((n, half), F32), sin, jnp.zeros((n, rest), F32)], axis=1)
    rep = LANES // HEAD_DIM
    return tuple(jnp.tile(a, (1, rep)) for a in (c, sa, sb))


def _block_bias(n_keys, lane_stride):
    key_blk = (jnp.arange(n_keys) // L_SLC)[:, None] * lane_stride
    return jnp.where(key_blk == jnp.arange(LANES)[None, :], NEG, 0.0).astype(BF16)


def kernel(x_prompt, x_sample, cache_cmp_k, cache_cmp_v, cache_slc_k, cache_slc_v, state_win_k, state_win_v,
           state_pool, state_conv, page_table, c_prompt, c_sample, w_ada, b_ada, g_norm_mix, g_norm_ffn, w_in,
           w_pool, s_pool, w_pool_out, w_dw, b_dw, g_conv_ln, b_conv_ln, w_conv_out, w_cmp_k, w_cmp_v,
           w_attn_out, w_out, w_gu, w_down, g_final):
    depth, d, _ = w_in.shape
    bp, tp, _ = x_prompt.shape
    bd, ts, _ = x_sample.shape
    wp_dim = w_pool_out.shape[1]
    wc_dim = w_conv_out.shape[1]
    n_pool = cache_cmp_k.shape[1]
    n_pages = page_table.shape[1]
    past_len = n_pages * PAGE_SIZE
    wl = state_win_k.shape[2]
    n_groups, gw = w_pool.shape[1], w_pool.shape[2]
    assert gw == POOL_WINDOWS_GW and n_groups == len(POOL_WINDOWS) and wp_dim == n_groups * gw

    n_a = wp_dim + 2 * wc_dim + N_HEADS * HEAD_DIM + 6 * KV_W + 3 * N_HEADS
    n_a_pad = n_a - 3 * N_HEADS + LANES
    w_in_a = jnp.pad(w_in[:, :, :n_a], ((0, 0), (0, 0), (0, n_a_pad - n_a))).astype(BF16)
    w_gm = w_in[:, :, n_a:].astype(BF16)
    eye = jnp.eye(n_groups, dtype=F32)
    wpool_bd = (w_pool[:, :, :, None, :] * eye[None, :, None, :, None]).reshape(depth, wp_dim, wp_dim).astype(BF16)
    wpo, wco, wao, wo = (w.astype(BF16) for w in (w_pool_out, w_conv_out, w_attn_out, w_out))
    wgu, wdn = w_gu.astype(BF16), w_down.astype(BF16)
    wck = jnp.broadcast_to(w_cmp_k[:, :, None], (depth, L_CMP, KV_W))
    wcv = jnp.broadcast_to(w_cmp_v[:, :, None], (depth, L_CMP, KV_W))
    vec3 = lambda a: a.reshape(depth, 1, -1)
    g_mix, g_ffn = vec3(g_norm_mix), vec3(g_norm_ffn)
    spool, bdw, gln, bln = vec3(s_pool), vec3(b_dw), vec3(g_conv_ln), vec3(b_conv_ln)
    gf = g_final.reshape(1, d)

    tabs_p = _rope_tables(jnp.arange(tp))
    tabs_s = _rope_tables(jnp.tile(past_len + jnp.arange(ts), bd))
    eneg_p = _block_bias(tp, 1)
    n_keys_s = past_len + LANES
    eneg_s = _block_bias(n_keys_s, 2).T
    key = jnp.arange(n_keys_s)
    blk_of_key = (key[:, None] // L_CMP == jnp.arange(LANES)[None, :]).astype(F32)
    selk = (w_cmp_k[:, key % L_CMP, None] * blk_of_key[None]).astype(BF16)
    selv = (w_cmp_v[:, key % L_CMP, None] * blk_of_key[None]).astype(BF16)

    ada = _ada_all(jnp.concatenate([c_prompt, c_sample], axis=0), w_ada, b_ada)
    pos_minor = lambda a: a.transpose(0, 1, 3, 4, 2).reshape(a.shape[0] * a.shape[1], KV_W, a.shape[2])
    caches = tuple(pos_minor(c) for c in (cache_cmp_k, cache_cmp_v, cache_slc_k, cache_slc_v))
    win_k_all, win_v_all = pos_minor(state_win_k), pos_minor(state_win_v)
    new_win = [jnp.zeros(win_k_all.shape, F32), jnp.zeros(win_v_all.shape, F32)]

    xp = x_prompt
    xs = x_sample.reshape(1, bd * ts, d)
    st_p, st_s = [], []
    yp = ys = None
    for l in range(depth):
        final = l == depth - 1
        mp = ada[l, :bp][:, None, :]
        ms = jnp.repeat(ada[l, bp:], ts, axis=0)[None]

        (up, uc, q, kc, vc, ks, vs, kw, vw, gn, ck, cv, ks_b, vs_b, kw_b, vw_b) = _inproj(
            xp, mp, g_mix, w_in_a, l, tabs_p, wck, wcv, w_pool=wp_dim, w_conv=wc_dim, with_cmp=True,
            q_dtype=BF16, tm=512)
        a_o, b_o, cst = _mix_prompt(up, uc, wpool_bd, spool, w_dw, bdw, gln, bln, l, tt=512)
        n_cmp = tp // L_CMP
        perm = lambda a: a.reshape(bp, n_cmp // 2, 2, KV_W).transpose(0, 2, 1, 3).reshape(bp, n_cmp, KV_W).astype(BF16)
        c_o = _nsa_prompt(q, gn, perm(ck), perm(cv), ks_b, vs_b, kw_b, vw_b, eneg_p)
        xp = _merge(xp, mp, g_mix, a_o, b_o, c_o, w_gm, wpo, wco, wao, wo, l, tm=512)
        res = _ffn(xp, mp, g_ffn, wgu, wdn, gf, l, tm=512, final=final)
        xp = res[0]
        if final:
            yp = res[1]
        kv4 = lambda a: a.reshape(a.shape[0], a.shape[1], N_KV, HEAD_DIM)
        wlp = min(WINDOW, tp)
        st_p.append((kv4(kc), kv4(vc), kv4(ks), kv4(vs), kv4(kw[:, tp - wlp:]), kv4(vw[:, tp - wlp:]),
                     up[:, tp - (max(POOL_WINDOWS) - 1):], cst[:, HALO - (CONV_K - 1):]))

        (up, uc, q, kc, vc, ks, vs, kw, vw, gn) = _inproj(
            xs, ms, g_mix, w_in_a, l, tabs_s, wck, wcv, w_pool=wp_dim, w_conv=wc_dim, with_cmp=False,
            q_dtype=F32, tm=512)
        tmaj = lambda a: a.reshape(bd, ts, a.shape[-1]).transpose(1, 0, 2)
        a_t, b_t, glu_t = _mix_sample(state_pool[l].transpose(1, 0, 2), tmaj(up), state_conv[l].transpose(1, 0, 2),
                                      tmaj(uc), wpool_bd, spool, w_dw, bdw, gln, bln, l)
        bmaj = lambda a: a.transpose(1, 0, 2).reshape(1, bd * ts, a.shape[-1])
        per_seq = lambda a: a.reshape(bd, ts, a.shape[-1])
        c_o, *new_win = _nsa_sample(page_table, per_seq(q), per_seq(gn),
                                    [per_seq(a) for a in (kc, vc, ks, vs, kw, vw)], win_k_all, win_v_all, caches,
                                    selk, selv, eneg_s, l, new_win, n_pool=n_pool, depth=depth)
        xs = _merge(xs, ms, g_mix, bmaj(a_t), bmaj(b_t), c_o.reshape(1, bd * ts, -1), w_gm, wpo,
                    wco, wao, wo, l, tm=512)
        res = _ffn(xs, ms, g_ffn, wgu, wdn, gf, l, tm=512, final=final)
        xs = res[0]
        if final:
            ys = res[1]
        kv4s = lambda a: a.reshape(bd, -1, N_KV, HEAD_DIM)
        new_pool = jnp.concatenate([state_pool[l][:, ts:], per_seq(up)], axis=1)
        new_conv = jnp.concatenate([state_conv[l][:, ts:], glu_t.transpose(1, 0, 2)], axis=1)
        st_s.append((kv4s(kc), kv4s(vc), kv4s(ks), kv4s(vs), None, None, new_pool, new_conv))

    win_out = [w.reshape(depth, bd, N_KV, HEAD_DIM, wl).transpose(0, 1, 4, 2, 3) for w in new_win]
    outs = [yp, ys.reshape(bd, ts, d)]
    for i in range(8):
        outs.append(jnp.stack([s[i] for s in st_p]))
        outs.append(win_out[i - 4] if i in (4, 5) else jnp.stack([s[i] for s in st_s]))
    return tuple(outs)
```

```python
import functools

import jax
import jax.numpy as jnp
from jax import lax
from jax.experimental import pallas as pl
from jax.experimental.pallas import tpu as pltpu

F32 = jnp.float32
BF16 = jnp.bfloat16

HEAD_DIM = 64
N_KV = 2
GQA = 4
N_HEADS = N_KV * GQA
ROPE_DIM = 16
ROPE_THETA = 500000.0
L_CMP = 32
L_SLC = 64
L_SLC_SHIFT = 6
N_SEL = 16
WINDOW = 512
Q_BLOCK = 128
PAGE_SIZE = 128
POOL_WINDOWS = (2, 4, 8, 16)
CONV_K = 31
EPS = 1e-6
NEG = -1e30
FORCE_SCORE = 1e4

LANES = 128
SUBLANES = 8
VMEM_LIMIT_BYTES = 56 * 1024 * 1024

NSA_CHUNK_KEYS = 512
NSA_PROMPT_BLOCKS = 2
NSA_SAMPLE_SEQS = 4
NSA_ROW_TILE = 32
HALO = 32
KV_W = N_KV * HEAD_DIM


def _params(*sem):
    return pltpu.CompilerParams(dimension_semantics=sem, vmem_limit_bytes=VMEM_LIMIT_BYTES)


def _modnorm(x, g, sc, sh):
    ms = jnp.mean(x * x, axis=-1, keepdims=True)
    return (x * lax.rsqrt(ms + EPS) * g) * (1.0 + sc) + sh


def _sigmoid(x):
    return 1.0 / (1.0 + jnp.exp(-x))


def _dot_nt(a, b):
    return lax.dot_general(a, b, (((1,), (1,)), ((), ())), preferred_element_type=F32)


def _masked_softmax(s, mask):
    sm = jnp.where(mask, s, NEG)
    e = jnp.exp(sm - jnp.max(sm, axis=-1, keepdims=True))
    return jnp.where(mask, e / jnp.sum(e, axis=-1, keepdims=True), 0.0)


def _ada_kernel(c_ref, w_ref, b_ref, o_ref):
    c = c_ref[...]
    a = (c * _sigmoid(c)).astype(BF16)
    o_ref[...] = jnp.dot(a, w_ref[...].astype(BF16), preferred_element_type=F32) + b_ref[...]


def _ada_all(c_all, w_ada, b_ada):
    depth, d, n = w_ada.shape
    tn = 1536
    assert n % tn == 0
    nb = c_all.shape[0]
    return pl.pallas_call(
        _ada_kernel,
        grid=(depth, n // tn),
        in_specs=[
            pl.BlockSpec((nb, d), lambda l, j: (0, 0)),
            pl.BlockSpec((None, d, tn), lambda l, j: (l, 0, j)),
            pl.BlockSpec((None, 1, tn), lambda l, j: (l, 0, j)),
        ],
        out_specs=pl.BlockSpec((None, nb, tn), lambda l, j: (l, 0, j)),
        out_shape=jax.ShapeDtypeStruct((depth, nb, n), F32),
        compiler_params=_params("parallel", "parallel"),
        name="ada_mod",
    )(c_all, w_ada, b_ada.reshape(depth, 1, n))


def _inproj_kernel(x_ref, sc_ref, sh_ref, g_ref, w_ref, cos_ref, sa_ref, sb_ref, wck_ref, wcv_ref, *outs,
                   w_pool, w_conv, with_cmp):
    (up_ref, uc_ref, q_ref, kc_ref, vc_ref, ks_ref, vs_ref, kw_ref, vw_ref, gn_ref) = outs[:10]
    h = _modnorm(x_ref[0], g_ref[...], sc_ref[0], sh_ref[0]).astype(BF16)
    y = jnp.dot(h, w_ref[...], preferred_element_type=F32)
    cos, sa, sb = cos_ref[...], sa_ref[...], sb_ref[...]

    def rope(t):
        return t * cos + pltpu.roll(t, LANES - ROPE_DIM // 2, 1) * sa + pltpu.roll(t, ROPE_DIM // 2, 1) * sb

    o = 0
    up_ref[0] = y[:, o:o + w_pool]
    o += w_pool
    uc_ref[0] = y[:, o:o + 2 * w_conv]
    o += 2 * w_conv
    for j in range(N_HEADS * HEAD_DIM // LANES):
        q_ref[0, :, j * LANES:(j + 1) * LANES] = rope(y[:, o:o + LANES]).astype(q_ref.dtype)
        o += LANES
    kc = rope(y[:, o:o + KV_W])
    vc = y[:, o + KV_W:o + 2 * KV_W]
    ks = rope(y[:, o + 2 * KV_W:o + 3 * KV_W])
    vs = y[:, o + 3 * KV_W:o + 4 * KV_W]
    kw = rope(y[:, o + 4 * KV_W:o + 5 * KV_W])
    vw = y[:, o + 5 * KV_W:o + 6 * KV_W]
    o += 6 * KV_W
    kc_ref[0], vc_ref[0], ks_ref[0], vs_ref[0], kw_ref[0], vw_ref[0] = kc, vc, ks, vs, kw, vw
    gn_ref[0] = y[:, o:o + LANES]
    if with_cmp:
        ck_ref, cv_ref = outs[10:12]
        tm = kc.shape[0]
        ck_ref[0] = jnp.sum(kc.reshape(tm // L_CMP, L_CMP, KV_W) * wck_ref[...][None], axis=1)
        cv_ref[0] = jnp.sum(vc.reshape(tm // L_CMP, L_CMP, KV_W) * wcv_ref[...][None], axis=1)
        for ref, val in zip(outs[12:], (ks, vs, kw, vw)):
            ref[0] = val.astype(BF16)


def _mod_spec(mods, col, tm, d):
    if mods.shape[1] == 1:
        return pl.BlockSpec((1, 1, d), lambda bi, i: (bi, 0, col))
    return pl.BlockSpec((1, tm, d), lambda bi, i: (bi, i, col))


def _inproj(x, mods, g, w_a, l, tabs, wck, wcv, *, w_pool, w_conv, with_cmp, q_dtype, tm):
    b, t, d = x.shape
    n = w_a.shape[-1]
    tm = min(tm, t)
    assert t % tm == 0 and (tm % L_CMP == 0 or not with_cmp)
    tab_spec = pl.BlockSpec((tm, LANES), lambda bi, i: (i, 0))
    row = lambda w: pl.BlockSpec((1, tm, w), lambda bi, i: (bi, i, 0))
    widths = [w_pool, 2 * w_conv, N_HEADS * HEAD_DIM] + [KV_W] * 6 + [LANES]
    dtypes = [F32, F32, q_dtype] + [F32] * 7
    out_specs = [row(w) for w in widths]
    out_shape = [jax.ShapeDtypeStruct((b, t, w), dt) for w, dt in zip(widths, dtypes)]
    if with_cmp:
        out_specs += [pl.BlockSpec((1, tm // L_CMP, KV_W), lambda bi, i: (bi, i, 0))] * 2 + [row(KV_W)] * 4
        out_shape += ([jax.ShapeDtypeStruct((b, t // L_CMP, KV_W), F32)] * 2
                      + [jax.ShapeDtypeStruct((b, t, KV_W), BF16)] * 4)
    return pl.pallas_call(
        functools.partial(_inproj_kernel, w_pool=w_pool, w_conv=w_conv, with_cmp=with_cmp),
        grid=(b, t // tm),
        in_specs=[
            row(d), _mod_spec(mods, 1, tm, d), _mod_spec(mods, 0, tm, d),
            pl.BlockSpec((None, 1, d), lambda bi, i: (l, 0, 0)),
            pl.BlockSpec((None, d, n), lambda bi, i: (l, 0, 0)),
            tab_spec, tab_spec, tab_spec,
            pl.BlockSpec((None, L_CMP, KV_W), lambda bi, i: (l, 0, 0)),
            pl.BlockSpec((None, L_CMP, KV_W), lambda bi, i: (l, 0, 0)),
        ],
        out_specs=out_specs,
        out_shape=out_shape,
        compiler_params=_params("parallel", "parallel"),
        name="in_proj",
    )(x, mods, mods, g, w_a, *tabs, wck, wcv)


def _pool_means(ext_ref, lvl_ref, base, rows, pos0):
    assert POOL_WINDOWS == (2, 4, 8, 16) and base == 4 * SUBLANES
    n = base + rows
    lane = lax.broadcasted_iota(jnp.int32, (rows, LANES), 1)
    pos = lax.broadcasted_iota(jnp.int32, (rows, LANES), 0) + pos0
    low = lane < POOL_WINDOWS_GW
    means = []
    for tile in range(len(POOL_WINDOWS) // 2):
        cols = slice(tile * LANES, (tile + 1) * LANES)
        w_small, w_big = POOL_WINDOWS[2 * tile], POOL_WINDOWS[2 * tile + 1]
        sums = {}
        src, w, lo = None, 1, 0
        while w < w_big:
            lo += SUBLANES
            if src is None:
                cur = ext_ref[lo:n, cols] + ext_ref[lo - w:n - w, cols]
            else:
                cur = lvl_ref[src, lo:n] + lvl_ref[src, lo - w:n - w]
            w *= 2
            sums[w] = cur[base - lo:]
            if w < w_big:
                src = 0 if src != 0 else 1
                lvl_ref[src, lo:n] = cur
        cnt = jnp.minimum(jnp.where(low, w_small, w_big), pos + 1).astype(F32)
        means.append(jnp.where(low, sums[w_small], sums[w_big]) / cnt)
    return jnp.concatenate(means, axis=1)


POOL_WINDOWS_GW = 64


def _layernorm_silu(y, g, b):
    mu = jnp.mean(y, axis=-1, keepdims=True)
    yc = y - mu
    var = jnp.mean(yc * yc, axis=-1, keepdims=True)
    z = yc * lax.rsqrt(var + EPS) * g + b
    return z * _sigmoid(z)


def _mix_prompt_kernel(up_ref, uph_ref, uc_ref, uch_ref, wpool_ref, spool_ref, wdw_ref, bdw_ref, gln_ref, bln_ref,
                       a_ref, b_ref, cst_ref, pext_ref, cext_ref, part_ref, lvl_ref, *, tt, w_conv):
    i = pl.program_id(1)
    first = i == 0
    u = up_ref[0]
    pext_ref[0:HALO] = jnp.where(first, 0.0, uph_ref[0])
    pext_ref[HALO:HALO + tt] = u
    d = _pool_means(pext_ref, lvl_ref, HALO, tt, i * tt) - u
    z = jnp.dot(d.astype(BF16), wpool_ref[...], preferred_element_type=F32) * spool_ref[...]
    a_ref[0] = z.astype(a_ref.dtype)
    hc = uch_ref[0]
    cext_ref[0:HALO] = jnp.where(first, 0.0, hc[:, :w_conv] * _sigmoid(hc[:, w_conv:]))
    uc = uc_ref[0]
    cext_ref[HALO:HALO + tt] = uc[:, :w_conv] * _sigmoid(uc[:, w_conv:])
    off = HALO - (CONV_K - 1)
    acc = None
    for m in range(SUBLANES):
        n_rows = tt + SUBLANES * (-(-(off + m) // SUBLANES))
        part = None
        for k in range(m, CONV_K, SUBLANES):
            assert k - m + n_rows <= HALO + tt
            term = wdw_ref[k:k + 1, :] * cext_ref[k - m:k - m + n_rows]
            part = term if part is None else part + term
        part_ref[m, 0:n_rows] = part
        shifted = part_ref[m, off + m:off + m + tt]
        acc = shifted if acc is None else acc + shifted
    b_ref[0] = _layernorm_silu(acc + bdw_ref[...], gln_ref[...], bln_ref[...]).astype(b_ref.dtype)
    cst_ref[0] = cext_ref[tt:tt + HALO]


def _mix_prompt(up, uc, wpool_bd, spool, wdw, bdw, gln, bln, l, *, tt):
    b, t, w_pool = up.shape
    w_conv = uc.shape[-1] // 2
    tt = min(tt, t)
    assert t % tt == 0 and tt % HALO == 0
    r = tt // HALO
    halo = lambda w: pl.BlockSpec((1, HALO, w), lambda bi, i: (bi, jnp.maximum(i * r - 1, 0), 0))
    vec = lambda w: pl.BlockSpec((None, 1, w), lambda bi, i: (l, 0, 0))
    return pl.pallas_call(
        functools.partial(_mix_prompt_kernel, tt=tt, w_conv=w_conv),
        grid=(b, t // tt),
        in_specs=[
            pl.BlockSpec((1, tt, w_pool), lambda bi, i: (bi, i, 0)), halo(w_pool),
            pl.BlockSpec((1, tt, 2 * w_conv), lambda bi, i: (bi, i, 0)), halo(2 * w_conv),
            pl.BlockSpec((None, w_pool, w_pool), lambda bi, i: (l, 0, 0)), vec(w_pool),
            pl.BlockSpec((None, CONV_K, w_conv), lambda bi, i: (l, 0, 0)), vec(w_conv), vec(w_conv), vec(w_conv),
        ],
        out_specs=[
            pl.BlockSpec((1, tt, w_pool), lambda bi, i: (bi, i, 0)),
            pl.BlockSpec((1, tt, w_conv), lambda bi, i: (bi, i, 0)),
            pl.BlockSpec((1, HALO, w_conv), lambda bi, i: (bi, 0, 0)),
        ],
        out_shape=[
            jax.ShapeDtypeStruct((b, t, w_pool), BF16),
            jax.ShapeDtypeStruct((b, t, w_conv), BF16),
            jax.ShapeDtypeStruct((b, HALO, w_conv), F32),
        ],
        scratch_shapes=[pltpu.VMEM((HALO + tt, w_pool), F32), pltpu.VMEM((HALO + tt, w_conv), F32),
                        pltpu.VMEM((SUBLANES, tt + 2 * SUBLANES, w_conv), F32),
                        pltpu.VMEM((2, HALO + tt, LANES), F32)],
        compiler_params=_params("parallel", "arbitrary"),
        name="mix_prompt",
    )(up, up, uc, uc, wpool_bd, spool, wdw, bdw, gln, bln)


def _mix_sample_kernel(pst_ref, up_ref, cst_ref, uc_ref, wpool_ref, spool_ref, wdw_ref, bdw_ref, gln_ref, bln_ref,
                       a_ref, b_ref, glu_ref, *, w_conv):
    n_p, n_c, ts = pst_ref.shape[0], cst_ref.shape[0], up_ref.shape[0]
    prow = lambda r: pst_ref[r] if r < n_p else up_ref[r - n_p]
    low = lax.broadcasted_iota(jnp.int32, (up_ref.shape[1], LANES), 1) < POOL_WINDOWS_GW
    glu = []
    for t in range(ts):
        uc = uc_ref[t]
        glu.append(uc[:, :w_conv] * _sigmoid(uc[:, w_conv:]))
        glu_ref[t] = glu[t]
    crow = lambda r: cst_ref[r] if r < n_c else glu[r - n_c]
    for t in range(ts):
        u = up_ref[t]
        means = None
        acc = u
        sums = {1: acc}
        for s in range(1, max(POOL_WINDOWS)):
            acc = acc + prow(n_p + t - s)
            sums[s + 1] = acc
        tiles = []
        for tile in range(len(POOL_WINDOWS) // 2):
            cols = slice(tile * LANES, (tile + 1) * LANES)
            w_small, w_big = POOL_WINDOWS[2 * tile], POOL_WINDOWS[2 * tile + 1]
            tiles.append(jnp.where(low, sums[w_small][:, cols] / float(w_small),
                                   sums[w_big][:, cols] / float(w_big)))
        means = jnp.concatenate(tiles, axis=1)
        z = jnp.dot((means - u).astype(BF16), wpool_ref[...], preferred_element_type=F32) * spool_ref[...]
        a_ref[t] = z.astype(a_ref.dtype)
        acc = wdw_ref[0:1, :] * crow(t + n_c - (CONV_K - 1))
        for k in range(1, CONV_K):
            acc = acc + wdw_ref[k:k + 1, :] * crow(t + n_c - (CONV_K - 1) + k)
        b_ref[t] = _layernorm_silu(acc + bdw_ref[...], gln_ref[...], bln_ref[...]).astype(b_ref.dtype)


def _mix_sample(pst_t, up_t, cst_t, uc_t, wpool_bd, spool, wdw, bdw, gln, bln, l):
    ts, bd, w_pool = up_t.shape
    w_conv = uc_t.shape[-1] // 2
    full = lambda a: pl.BlockSpec(a.shape, lambda i: (0,) * a.ndim)
    vec = lambda w: pl.BlockSpec((None, 1, w), lambda i: (l, 0, 0))
    return pl.pallas_call(
        functools.partial(_mix_sample_kernel, w_conv=w_conv),
        grid=(1,),
        in_specs=[
            full(pst_t), full(up_t), full(cst_t), full(uc_t),
            pl.BlockSpec((None, w_pool, w_pool), lambda i: (l, 0, 0)), vec(w_pool),
            pl.BlockSpec((None, CONV_K, w_conv), lambda i: (l, 0, 0)), vec(w_conv), vec(w_conv), vec(w_conv),
        ],
        out_specs=[pl.BlockSpec((ts, bd, w_pool), lambda i: (0, 0, 0)),
                   pl.BlockSpec((ts, bd, w_conv), lambda i: (0, 0, 0)),
                   pl.BlockSpec((ts, bd, w_conv), lambda i: (0, 0, 0))],
        out_shape=[jax.ShapeDtypeStruct((ts, bd, w_pool), BF16),
                   jax.ShapeDtypeStruct((ts, bd, w_conv), BF16),
                   jax.ShapeDtypeStruct((ts, bd, w_conv), F32)],
        compiler_params=_params("arbitrary"),
        name="mix_sample",
    )(pst_t, up_t, cst_t, uc_t, wpool_bd, spool, wdw, bdw, gln, bln)


def _topk_masks(scores, n_sel):
    sts = tuple(s.T for s in scores)
    cand = lax.broadcasted_iota(jnp.int32, sts[0].shape, 0).astype(F32)

    def body(_, carry):
        out = []
        for st, sel in carry:
            m = jnp.max(st, axis=0, keepdims=True)
            first = jnp.min(jnp.where(st == m, cand, float(LANES)), axis=0, keepdims=True)
            hit = cand == first
            out.append((jnp.where(hit, -jnp.inf, st), jnp.where(hit, 1.0, sel)))
        return tuple(out)

    res = lax.fori_loop(0, n_sel, body, tuple((st, jnp.zeros(st.shape, F32)) for st in sts))
    return [sel.T for _, sel in res]


def _head_rows(tile, src_half, dst_half, lane):
    if src_half != dst_half:
        tile = pltpu.roll(tile, HEAD_DIM, 1)
    return jnp.where((lane >= dst_half * HEAD_DIM) & (lane < (dst_half + 1) * HEAD_DIM), tile, 0.0)


def _nsa_prompt_multi_kernel(q_ref, gn_ref, ck_ref, cv_ref, ks_ref, vs_ref, kw_ref, vw_ref, eneg_ref, o_ref,
                             qa_scr, s_scr, p_scr, sw_scr, pw_scr, *, ck_keys, n_sub):
    step = pl.program_id(1)
    qb = Q_BLOCK
    rows = GQA * qb
    scale = HEAD_DIM ** -0.5
    n_cmp = ck_ref.shape[1]
    n_slc = n_cmp // 2
    n_wchunks = WINDOW // qb + 1
    wk = n_wchunks * qb
    lane = lax.broadcasted_iota(jnp.int32, (qb, LANES), 1)
    chains = [(u, k) for u in range(n_sub) for k in range(N_KV)]

    def qblk(u):
        return step * n_sub + u

    def qpos(u, shape, row0=0):
        return qblk(u) * qb + ((row0 + lax.broadcasted_iota(jnp.int32, shape, 0)) & (qb - 1))

    def qrows(u, ref, lo, hi):
        return ref[0, u * qb:(u + 1) * qb, lo:hi]

    st = {}

    def phase1(u):
        cur = qpos(u, lane.shape) >> L_SLC_SHIFT
        valid_s = lane <= cur
        forced = (lane == 0) | (lane == cur) | (lane == cur - 1)
        st[u, "valid"] = valid_s
        for k in range(N_KV):
            qh = []
            for g in range(GQA):
                h = k * GQA + g
                tile = qrows(u, q_ref, (h // 2) * LANES, (h // 2 + 1) * LANES).astype(F32) * scale
                qh.append(_head_rows(tile, h % 2, k, lane))
            qk = jnp.concatenate(qh, axis=0).astype(BF16)
            st[u, k, "qk"] = qk
            s_c = _dot_nt(qk, ck_ref[0])
            yield
            cl = lax.broadcasted_iota(jnp.int32, (rows, n_cmp), 1)
            c_end = jnp.where(cl < n_slc, cl * L_SLC + L_CMP - 1, (cl - n_slc) * L_SLC + L_SLC - 1)
            p_c = _masked_softmax(s_c, c_end <= qpos(u, s_c.shape))
            yield
            st[u, k, "o_c"] = jnp.dot(p_c.astype(BF16), cv_ref[0], preferred_element_type=F32)
            ps = p_c[0:qb]
            for g in range(1, GQA):
                ps = ps + p_c[g * qb:(g + 1) * qb]
            imp = ps[:, :n_slc] + ps[:, n_slc:]
            if n_slc < LANES:
                imp = jnp.concatenate([imp, jnp.zeros((qb, LANES - n_slc), F32)], axis=1)
            score = jnp.where(valid_s, jnp.where(forced, FORCE_SCORE, imp), -1.0)
            st[u, k, "score"] = jnp.where(lane < n_slc, score, -jnp.inf)
            yield

    _lockstep(phase1(u) for u in range(n_sub))

    for u in range(n_sub):
        for k, sel in enumerate(_topk_masks([st[u, k, "score"] for k in range(N_KV)], min(N_SEL, n_slc))):
            notsel = (1.0 - jnp.where(st[u, "valid"], sel, 0.0)).astype(BF16)
            qa_scr[u * N_KV + k] = jnp.concatenate([st[u, k, "qk"], jnp.concatenate([notsel] * GQA, axis=0)], axis=1)

    own = lax.broadcasted_iota(jnp.int32, (ck_keys, LANES), 1) < HEAD_DIM

    def scores(c, buf):
        start = pl.multiple_of(c * ck_keys, ck_keys)
        k_aug = jnp.concatenate([ks_ref[0, pl.ds(start, ck_keys), :], eneg_ref[pl.ds(start, ck_keys), :]], axis=1)
        nc = len(chains)
        s_all = _dot_nt(qa_scr[...].reshape(nc * rows, 2 * LANES), k_aug)
        s_scr[buf] = s_all.reshape(nc, rows, ck_keys)

    def probs(c, buf, ms, causal):
        new_m, alphas = [], []
        for n, (u, k) in enumerate(chains):
            m_tiles = []
            for r in range(0, rows, NSA_ROW_TILE):
                rs = slice(r, r + NSA_ROW_TILE)
                s = s_scr[buf, n, rs]
                if causal:
                    kpos = c * ck_keys + lax.broadcasted_iota(jnp.int32, s.shape, 1)
                    s = jnp.where(kpos <= qpos(u, s.shape, r), s, NEG)
                m_new = jnp.maximum(ms[n][rs], jnp.max(s, axis=-1, keepdims=True))
                p_scr[buf, n, rs] = jnp.exp(s - m_new).astype(BF16)
                m_tiles.append(m_new)
            m_new = jnp.concatenate(m_tiles, axis=0)
            alphas.append(jnp.exp(ms[n] - m_new))
            new_m.append(m_new)
        return new_m, alphas

    def values(c, buf, alphas, accs):
        start = pl.multiple_of(c * ck_keys, ck_keys)
        v = vs_ref[0, pl.ds(start, ck_keys), :]
        v_one = [jnp.where(own, v, 1.0), jnp.where(own, 1.0, v)]
        return [alphas[n] * accs[n] + jnp.dot(p_scr[buf, n], v_one[k], preferred_element_type=F32)
                for n, (u, k) in enumerate(chains)]

    def pair(c, carry, causal, last):
        ms, alphas, accs = carry
        accs = values(jnp.maximum(c - 1, 0), 1, alphas, accs)
        ms, alphas = probs(c, 0, ms, causal)
        scores(c + 1, 1)
        accs = values(c, 0, alphas, accs)
        ms, alphas = probs(c + 1, 1, ms, causal)
        if last:
            accs = values(c + 1, 1, alphas, accs)
        else:
            scores(c + 2, 0)
        return ms, alphas, accs

    n_pairs = (qblk(n_sub - 1) * qb + qb + 2 * ck_keys - 1) // (2 * ck_keys)
    p_scr[1] = jnp.zeros(p_scr.shape[1:], BF16)
    scores(0, 0)
    nc = len(chains)
    carry = ([jnp.full((rows, 1), -jnp.inf, F32)] * nc, [jnp.ones((rows, 1), F32)] * nc,
             [jnp.zeros((rows, LANES), F32)] * nc)
    carry = lax.fori_loop(0, n_pairs - 1, lambda j, cr: pair(2 * j, cr, False, False), carry)
    c_last = 2 * (n_pairs - 1)

    def last_single(cr):
        ms, alphas, accs = cr
        accs = values(jnp.maximum(c_last - 1, 0), 1, alphas, accs)
        ms, alphas = probs(c_last, 0, ms, True)
        return values(c_last, 0, alphas, accs)

    n_chunks = (qblk(n_sub - 1) * qb + qb + ck_keys - 1) // ck_keys
    accs = lax.cond(n_chunks == c_last + 1, last_single, lambda cr: pair(c_last, cr, True, True)[2], carry)
    o_ss = [acc / pltpu.roll(acc, HEAD_DIM, 1) for acc in accs]

    def tail(u):
        i = qblk(u)
        kws, vws = [], []
        for dj in range(n_wchunks):
            st0 = pl.multiple_of(jnp.maximum(i - (n_wchunks - 1) + dj, 0) * qb, qb)
            kws.append(kw_ref[0, pl.ds(st0, qb), :])
            vws.append(vw_ref[0, pl.ds(st0, qb), :])
        kw_blk, vw_blk = jnp.concatenate(kws, axis=0), jnp.concatenate(vws, axis=0)
        kwpos = (i - (n_wchunks - 1)) * qb + lax.broadcasted_iota(jnp.int32, (qb, wk), 1)
        qp = qpos(u, (qb, wk))
        wbias = jnp.where((kwpos <= qp) & (kwpos > qp - WINDOW) & (kwpos >= 0), 0.0, NEG)
        own_w = lax.broadcasted_iota(jnp.int32, (wk, LANES), 1) < HEAD_DIM
        vw_one = [jnp.where(own_w, vw_blk, 1.0), jnp.where(own_w, 1.0, vw_blk)]
        for k in range(N_KV):
            sw_scr[u * N_KV + k] = _dot_nt(st[u, k, "qk"], kw_blk)
        yield
        o_ws = []
        for k in range(N_KV):
            n = u * N_KV + k
            for r in range(0, rows, NSA_ROW_TILE):
                rs = slice(r, r + NSA_ROW_TILE)
                s = sw_scr[n, rs] + wbias[r % qb:r % qb + NSA_ROW_TILE]
                pw_scr[n, rs] = jnp.exp(s - jnp.max(s, axis=-1, keepdims=True)).astype(BF16)
            yield
            acc = jnp.dot(pw_scr[n], vw_one[k], preferred_element_type=F32)
            o_ws.append(acc / pltpu.roll(acc, HEAD_DIM, 1))
            yield
        sig = _sigmoid(qrows(u, gn_ref, 0, LANES))
        out_tiles = [None] * (N_HEADS // 2)
        for k in range(N_KV):
            o_c, o_s, o_w = st[u, k, "o_c"], o_ss[u * N_KV + k], o_ws[k]
            for g in range(GQA):
                h = k * GQA + g
                col = 3 * h
                r = slice(g * qb, (g + 1) * qb)
                o = (sig[:, col:col + 1] * o_c[r] + sig[:, col + 1:col + 2] * o_s[r]
                     + sig[:, col + 2:col + 3] * o_w[r])
                o = _head_rows(o, k, h % 2, lane)
                out_tiles[h // 2] = o if out_tiles[h // 2] is None else out_tiles[h // 2] + o
            yield
        for j, tile in enumerate(out_tiles):
            o_ref[0, u * qb:(u + 1) * qb, j * LANES:(j + 1) * LANES] = tile.astype(o_ref.dtype)

    _lockstep(tail(u) for u in range(n_sub))


def _nsa_prompt(q, gn, ckp, cvp, ks, vs, kw, vw, eneg):
    b, t, _ = q.shape
    assert t % Q_BLOCK == 0 and t // L_SLC <= LANES
    ck_keys = min(NSA_CHUNK_KEYS, t // 2)
    assert t % (2 * ck_keys) == 0
    rows = GQA * Q_BLOCK
    n_sub = NSA_PROMPT_BLOCKS
    assert (2 * ck_keys) % (n_sub * Q_BLOCK) == 0 and t % (n_sub * Q_BLOCK) == 0
    nc = n_sub * N_KV
    qspec = lambda w: pl.BlockSpec((1, n_sub * Q_BLOCK, w), lambda bi, i: (bi, i, 0))
    seq = lambda n: pl.BlockSpec((1, n, KV_W), lambda bi, i: (bi, 0, 0))
    return pl.pallas_call(
        functools.partial(_nsa_prompt_multi_kernel, ck_keys=ck_keys, n_sub=n_sub),
        scratch_shapes=[pltpu.VMEM((nc, rows, 2 * LANES), BF16), pltpu.VMEM((2, nc, rows, ck_keys), F32),
                        pltpu.VMEM((2, nc, rows, ck_keys), BF16),
                        pltpu.VMEM((nc, rows, WINDOW + Q_BLOCK), F32),
                        pltpu.VMEM((nc, rows, WINDOW + Q_BLOCK), BF16)],
        grid=(b, t // (n_sub * Q_BLOCK)),
        in_specs=[qspec(N_HEADS * HEAD_DIM), qspec(LANES), seq(t // L_CMP), seq(t // L_CMP),
                  seq(t), seq(t), seq(t), seq(t), pl.BlockSpec((t, LANES), lambda bi, i: (0, 0))],
        out_specs=qspec(N_HEADS * HEAD_DIM),
        out_shape=jax.ShapeDtypeStruct((b, t, N_HEADS * HEAD_DIM), BF16),
        compiler_params=_params("parallel", "parallel"),
        name="nsa_prompt",
    )(q, gn, ckp, cvp, ks, vs, kw, vw, eneg)


def _nsa_sample_kernel(pt_ref, q_ref, gn_ref, kcn_ref, vcn_ref, ksn_ref, vsn_ref, kwn_ref, vwn_ref, wk_ref, wv_ref,
                       selk_ref, selv_ref, eneg_ref, *rest, n_pages, past_len, ts, page0):
    caches = rest[2:6]
    o_ref, nwk_ref, nwv_ref = rest[6:9]
    (past_buf, sems, tail_s, q_s, sc_s, g_s, o_s) = rest[9:]
    n_seq = q_ref.shape[0]
    b = pl.program_id(0)
    slot = b & 1
    scale = HEAD_DIM ** -0.5
    wl = wk_ref.shape[2]
    n_keys = past_len + LANES
    n_slc = -(-(past_len + ts) // L_SLC)
    rows = GQA * N_KV * ts
    grp = N_KV * ts
    lane_t = lax.broadcasted_iota(jnp.int32, (ts, LANES), 1)

    def page_copy(step, dst_slot, u, j, p):
        src = caches[j].at[page0 + pt_ref[(step * n_seq + u) * n_pages + p]]
        dst = past_buf.at[dst_slot, u, j, :, pl.ds(p * PAGE_SIZE, PAGE_SIZE)]
        return pltpu.make_async_copy(src, dst, sems.at[dst_slot])

    def all_pages(step, dst_slot, op):
        for u in range(n_seq):
            for j in range(4):
                for p in range(n_pages):
                    op(page_copy(step, dst_slot, u, j, p))

    @pl.when(b == 0)
    def _():
        all_pages(0, 0, lambda cp: cp.start())

    @pl.when(b + 1 < pl.num_programs(0))
    def _():
        all_pages(b + 1, 1 - slot, lambda cp: cp.start())

    all_pages(b, slot, lambda cp: cp.wait())

    def qpos(shape):
        return past_len + (lax.broadcasted_iota(jnp.int32, shape, 0) & (ts - 1))

    lane_g = lax.broadcasted_iota(jnp.int32, (grp, LANES), 1)
    blk = lane_g >> 1
    is_blk = ((lane_g & 1) == 0) & (blk < n_slc)
    cur = qpos(lane_g.shape) >> L_SLC_SHIFT
    valid_s = blk <= cur
    forced = (blk == 0) | (blk == cur) | (blk == cur - 1)

    def front(u, out):
        def new_cols(j, new_ref):
            tail_s[u, j] = jnp.zeros(tail_s.shape[2:], F32)
            tail_s[u, j, 0:ts] = new_ref[u]
            return tail_s[u, j].T

        kc_p, vc_p, ks_p, vs_p = (past_buf[slot, u, j].astype(BF16) for j in range(4))
        kc_n, vc_n, ks_n, vs_n = (new_cols(j, r).astype(BF16)
                                  for j, r in enumerate((kcn_ref, vcn_ref, ksn_ref, vsn_ref)))

        kw_all = jnp.concatenate([wk_ref[u], new_cols(4, kwn_ref)], axis=1)
        vw_all = jnp.concatenate([wv_ref[u], new_cols(5, vwn_ref)], axis=1)
        nwk_ref[u] = kw_all[:, ts:ts + wl]
        nwv_ref[u] = vw_all[:, ts:ts + wl]
        yield

        for g in range(GQA):
            for k in range(N_KV):
                h = k * GQA + g
                tile = q_ref[u, :, (h // 2) * LANES:(h // 2 + 1) * LANES] * scale
                q_s[u, (g * N_KV + k) * ts:(g * N_KV + k + 1) * ts] = _head_rows(tile, h % 2, k, lane_t)
        qa = q_s[u].astype(BF16)

        def compress(past, new, sel_ref):
            return (jnp.dot(past, sel_ref[0:past_len], preferred_element_type=F32)
                    + jnp.dot(new, sel_ref[past_len:n_keys], preferred_element_type=F32)).astype(BF16)

        ck_t = compress(kc_p, kc_n, selk_ref)
        yield
        cv_t = compress(vc_p, vc_n, selv_ref)
        yield
        s_c = jnp.dot(qa, ck_t, preferred_element_type=F32)
        yield
        cl = lax.broadcasted_iota(jnp.int32, s_c.shape, 1)
        p_c = _masked_softmax(s_c, (cl + 1) * L_CMP - 1 <= qpos(s_c.shape))
        yield
        o_c = _dot_nt(p_c.astype(BF16), cv_t)
        ps = p_c[0:grp]
        for g in range(1, GQA):
            ps = ps + p_c[g * grp:(g + 1) * grp]
        imp = ps + pltpu.roll(ps, LANES - 1, 1)

        score = jnp.where(valid_s, jnp.where(forced, FORCE_SCORE, imp), -1.0)
        sc_s[u * grp:(u + 1) * grp] = jnp.where(is_blk, score, -jnp.inf)
        out[u] = (qa, o_c, ks_p, ks_n, vs_p, vs_n, kw_all, vw_all)

    sc_s[...] = jnp.zeros(sc_s.shape, F32)
    fronts = {}
    _lockstep(front(u, fronts) for u in range(n_seq))
    sel = _topk_masks([sc_s[...]], min(N_SEL, n_slc))[0]
    _lockstep(_nsa_sample_back(u, fronts[u], sel[u * grp:(u + 1) * grp], is_blk & valid_s, qpos, gn_ref, eneg_ref,
                               o_ref, g_s, o_s, past_len=past_len, ts=ts, wl=wl) for u in range(n_seq))


def _lockstep(stages):
    stages = list(stages)
    while stages:
        for gen in list(stages):
            try:
                next(gen)
            except StopIteration:
                stages.remove(gen)


def _nsa_sample_back(u, front, sel, selectable, qpos, gn_ref, eneg_ref, o_ref, g_s, o_s, *, past_len, ts, wl):
    qa, o_c, ks_p, ks_n, vs_p, vs_n, kw_all, vw_all = front
    n_keys = past_len + LANES
    lane_t = lax.broadcasted_iota(jnp.int32, (ts, LANES), 1)
    notsel = (1.0 - jnp.where(selectable, sel, 0.0)).astype(BF16)
    q_aug = jnp.concatenate([qa, jnp.concatenate([notsel] * GQA, axis=0)], axis=1)

    s_s = jnp.concatenate(
        [jnp.dot(q_aug, jnp.concatenate([ks_p, eneg_ref[:, 0:past_len]], axis=0), preferred_element_type=F32),
         jnp.dot(q_aug, jnp.concatenate([ks_n, eneg_ref[:, past_len:n_keys]], axis=0), preferred_element_type=F32)],
        axis=1)
    s_w = jnp.dot(qa, kw_all.astype(BF16), preferred_element_type=F32)
    yield
    kpos = lax.broadcasted_iota(jnp.int32, s_s.shape, 1)
    s_s = jnp.where(kpos <= qpos(s_s.shape), s_s, NEG)
    e = jnp.exp(s_s - jnp.max(s_s, axis=-1, keepdims=True))
    yield
    p_s = (e / jnp.sum(e, axis=-1, keepdims=True)).astype(BF16)
    yield
    o_sel = _dot_nt(p_s[:, 0:past_len], vs_p) + _dot_nt(p_s[:, past_len:n_keys], vs_n)
    yield
    kwpos = past_len - wl + lax.broadcasted_iota(jnp.int32, s_w.shape, 1)
    qp = qpos(s_w.shape)
    okw = (kwpos <= qp) & (kwpos > qp - WINDOW) & (kwpos >= 0)
    p_w = _masked_softmax(s_w, okw)
    yield
    o_w = _dot_nt(p_w.astype(BF16), vw_all.astype(BF16))
    yield

    sig = _sigmoid(gn_ref[u])
    o = None
    for j, ob in enumerate((o_c, o_sel, o_w)):
        for g in range(GQA):
            for k in range(N_KV):
                col = 3 * (k * GQA + g) + j
                g_s[u, j, (g * N_KV + k) * ts:(g * N_KV + k + 1) * ts] = jnp.broadcast_to(sig[:, col:col + 1],
                                                                                        (ts, LANES))
        o = g_s[u, j] * ob if o is None else o + g_s[u, j] * ob
    o_s[u] = o
    for j in range(N_HEADS // 2):
        tile = None
        for h in (2 * j, 2 * j + 1):
            k, g = h // GQA, h % GQA
            piece = _head_rows(o_s[u, (g * N_KV + k) * ts:(g * N_KV + k + 1) * ts], k, h % 2, lane_t)
            tile = piece if tile is None else tile + piece
        o_ref[u, :, j * LANES:(j + 1) * LANES] = tile


def _nsa_sample(page_table, q, gn, new_rows, win_k, win_v, caches, selk, selv, eneg, l, new_win, *, n_pool, depth):
    bd, ts, _ = q.shape
    n_pages = page_table.shape[1]
    past_len = n_pages * PAGE_SIZE
    wl = win_k.shape[2]
    n_keys = past_len + LANES
    assert ts <= LANES and -(-(past_len + ts) // L_SLC) * 2 <= LANES and eneg.shape == (LANES, n_keys)
    rows = GQA * N_KV * ts
    n_seq = NSA_SAMPLE_SEQS
    assert bd % n_seq == 0 and n_seq * N_KV * ts <= LANES
    tok = lambda w: pl.BlockSpec((n_seq, ts, w), lambda b, pt: (b, 0, 0))
    win = pl.BlockSpec((n_seq, KV_W, wl), lambda b, pt: (l * (bd // n_seq) + b, 0, 0))
    sel = pl.BlockSpec((None, n_keys, LANES), lambda b, pt: (l, 0, 0))

    hbm = pl.BlockSpec(memory_space=pl.ANY)
    in_specs = ([tok(N_HEADS * HEAD_DIM), tok(LANES)] + [tok(KV_W)] * 6 + [win, win, sel, sel,
                pl.BlockSpec((LANES, n_keys), lambda b, pt: (0, 0))] + [hbm] * 6)
    grid_spec = pltpu.PrefetchScalarGridSpec(
        num_scalar_prefetch=1,
        grid=(bd // n_seq,),
        in_specs=in_specs,
        out_specs=[tok(N_HEADS * HEAD_DIM), win, win],
        scratch_shapes=[
            pltpu.VMEM((2, n_seq, 4, KV_W, past_len), F32),
            pltpu.SemaphoreType.DMA((2,)),
            pltpu.VMEM((n_seq, 6, LANES, KV_W), F32),
            pltpu.VMEM((n_seq, rows, LANES), F32), pltpu.VMEM((LANES, LANES), F32),
            pltpu.VMEM((n_seq, 3, rows, LANES), F32), pltpu.VMEM((n_seq, rows, LANES), F32),
        ],
    )
    win_shape = jax.ShapeDtypeStruct((depth * bd, KV_W, wl), F32)
    n_fixed = 1 + 2 + 6 + 5
    return pl.pallas_call(
        functools.partial(_nsa_sample_kernel, n_pages=n_pages, past_len=past_len, ts=ts, page0=l * n_pool),
        grid_spec=grid_spec,
        out_shape=[jax.ShapeDtypeStruct((bd, ts, N_HEADS * HEAD_DIM), F32), win_shape, win_shape],
        input_output_aliases={n_fixed: 1, n_fixed + 1: 2},
        compiler_params=_params("arbitrary"),
        name="nsa_sample",
    )(page_table.reshape(-1), q, gn, *new_rows, win_k, win_v, selk, selv, eneg, *new_win, *caches)


def _merge_kernel(x_ref, sc_ref, sh_ref, gt_ref, g_ref, a_ref, b_ref, c_ref, wgm_ref, wp_ref, wc_ref, wa_ref,
                  wo_ref, o_ref):
    x = x_ref[0]
    d = x.shape[-1]
    h = _modnorm(x, g_ref[...], sc_ref[0], sh_ref[0]).astype(BF16)
    branches = (
        jnp.dot(a_ref[0].astype(BF16), wp_ref[...], preferred_element_type=F32),
        jnp.dot(b_ref[0].astype(BF16), wc_ref[...], preferred_element_type=F32),
        jnp.dot(c_ref[0].astype(BF16), wa_ref[...], preferred_element_type=F32),
    )
    merged = None
    for j, br in enumerate(branches):
        gm = _sigmoid(jnp.dot(h, wgm_ref[:, j * d:(j + 1) * d], preferred_element_type=F32))
        merged = gm * br if merged is None else merged + gm * br
    out = jnp.dot(merged.astype(BF16), wo_ref[...], preferred_element_type=F32)
    o_ref[0] = x + gt_ref[0] * out


def _merge(x, mods, g, a, bo, c, wgm, wp, wc, wa, wo, l, *, tm):
    b, t, d = x.shape
    tm = min(tm, t)
    assert t % tm == 0
    row = lambda w: pl.BlockSpec((1, tm, w), lambda bi, i: (bi, i, 0))
    wsp = lambda w: pl.BlockSpec((None,) + w.shape[1:], lambda bi, i: (l, 0, 0))
    return pl.pallas_call(
        _merge_kernel,
        grid=(b, t // tm),
        in_specs=[row(d), _mod_spec(mods, 1, tm, d), _mod_spec(mods, 0, tm, d), _mod_spec(mods, 2, tm, d),
                  pl.BlockSpec((None, 1, d), lambda bi, i: (l, 0, 0)),
                  row(a.shape[-1]), row(bo.shape[-1]), row(c.shape[-1]),
                  wsp(wgm), wsp(wp), wsp(wc), wsp(wa), wsp(wo)],
        out_specs=row(d),
        out_shape=jax.ShapeDtypeStruct((b, t, d), F32),
        compiler_params=_params("parallel", "parallel"),
        name="merge_out",
    )(x, mods, mods, mods, g, a, bo, c, wgm, wp, wc, wa, wo)


def _ffn_kernel(x_ref, sc_ref, sh_ref, gt_ref, g_ref, wgu_ref, wd_ref, gf_ref, *outs, d_ff, n_parts, final):
    x = x_ref[0]
    h = _modnorm(x, g_ref[...], sc_ref[0], sh_ref[0]).astype(BF16)
    part = d_ff // n_parts
    acc = None
    for c in range(n_parts):
        gp = jnp.dot(h, wgu_ref[:, c * part:(c + 1) * part], preferred_element_type=F32)
        up = jnp.dot(h, wgu_ref[:, d_ff + c * part:d_ff + (c + 1) * part], preferred_element_type=F32)
        act = (gp * _sigmoid(gp) * up).astype(BF16)
        dn = jnp.dot(act, wd_ref[c * part:(c + 1) * part, :], preferred_element_type=F32)
        acc = dn if acc is None else acc + dn
    y = x + gt_ref[0] * acc
    outs[0][0] = y
    if final:
        ms = jnp.mean(y * y, axis=-1, keepdims=True)
        outs[1][0] = y * lax.rsqrt(ms + EPS) * gf_ref[...]


def _ffn(x, mods, g, wgu, wd, gf, l, *, tm, final):
    b, t, d = x.shape
    d_ff = wd.shape[1]
    tm = min(tm, t)
    n_parts = 2
    assert t % tm == 0 and d_ff % (n_parts * LANES) == 0
    row = pl.BlockSpec((1, tm, d), lambda bi, i: (bi, i, 0))
    n_out = 2 if final else 1
    return pl.pallas_call(
        functools.partial(_ffn_kernel, d_ff=d_ff, n_parts=n_parts, final=final),
        grid=(b, t // tm),
        in_specs=[row, _mod_spec(mods, 4, tm, d), _mod_spec(mods, 3, tm, d), _mod_spec(mods, 5, tm, d),
                  pl.BlockSpec((None, 1, d), lambda bi, i: (l, 0, 0)),
                  pl.BlockSpec((None, d, 2 * d_ff), lambda bi, i: (l, 0, 0)),
                  pl.BlockSpec((None, d_ff, d), lambda bi, i: (l, 0, 0)),
                  pl.BlockSpec((1, d), lambda bi, i: (0, 0))],
        out_specs=[row] * n_out,
        out_shape=[jax.ShapeDtypeStruct((b, t, d), F32)] * n_out,
        compiler_params=_params("parallel", "parallel"),
        name="ffn",
    )(x, mods, mods, mods, g, wgu, wd, gf)


def _rope_tables(pos):
    half = ROPE_DIM // 2
    inv = ROPE_THETA ** (-jnp.arange(half, dtype=F32) * 2.0 / ROPE_DIM)
    ang = pos.astype(F32)[:, None] * inv[None, :]
    cos, sin = jnp.cos(ang), jnp.sin(ang)
    n = pos.shape[0]
    rest = HEAD_DIM - ROPE_DIM
    c = jnp.concatenate([cos, cos, jnp.ones((n, rest), F32)], axis=1)
    sa = jnp.concatenate([-sin, jnp.zeros((n, half + rest), F32)], axis=1)
    sb = jnp.concatenate([jnp.zeros((n, half), F32), sin, jnp.zeros((n, rest), F32)], axis=1)
    rep = LANES // HEAD_DIM
    return tuple(jnp.tile(a, (1, rep)) for a in (c, sa, sb))


def _block_bias(n_keys, lane_stride):
    key_blk = (jnp.arange(n_keys) // L_SLC)[:, None] * lane_stride
    return jnp.where(key_blk == jnp.arange(LANES)[None, :], NEG, 0.0).astype(BF16)


def kernel(x_prompt, x_sample, cache_cmp_k, cache_cmp_v, cache_slc_k, cache_slc_v, state_win_k, state_win_v,
           state_pool, state_conv, page_table, c_prompt, c_sample, w_ada, b_ada, g_norm_mix, g_norm_ffn, w_in,
           w_pool, s_pool, w_pool_out, w_dw, b_dw, g_conv_ln, b_conv_ln, w_conv_out, w_cmp_k, w_cmp_v,
           w_attn_out, w_out, w_gu, w_down, g_final):
    depth, d, _ = w_in.shape
    bp, tp, _ = x_prompt.shape
    bd, ts, _ = x_sample.shape
    wp_dim = w_pool_out.shape[1]
    wc_dim = w_conv_out.shape[1]
    n_pool = cache_cmp_k.shape[1]
    n_pages = page_table.shape[1]
    past_len = n_pages * PAGE_SIZE
    wl = state_win_k.shape[2]
    n_groups, gw = w_pool.shape[1], w_pool.shape[2]
    assert gw == POOL_WINDOWS_GW and n_groups == len(POOL_WINDOWS) and wp_dim == n_groups * gw

    n_a = wp_dim + 2 * wc_dim + N_HEADS * HEAD_DIM + 6 * KV_W + 3 * N_HEADS
    n_a_pad = n_a - 3 * N_HEADS + LANES
    w_in_a = jnp.pad(w_in[:, :, :n_a], ((0, 0), (0, 0), (0, n_a_pad - n_a))).astype(BF16)
    w_gm = w_in[:, :, n_a:].astype(BF16)
    eye = jnp.eye(n_groups, dtype=F32)
    wpool_bd = (w_pool[:, :, :, None, :] * eye[None, :, None, :, None]).reshape(depth, wp_dim, wp_dim).astype(BF16)
    wpo, wco, wao, wo = (w.astype(BF16) for w in (w_pool_out, w_conv_out, w_attn_out, w_out))
    wgu, wdn = w_gu.astype(BF16), w_down.astype(BF16)
    wck = jnp.broadcast_to(w_cmp_k[:, :, None], (depth, L_CMP, KV_W))
    wcv = jnp.broadcast_to(w_cmp_v[:, :, None], (depth, L_CMP, KV_W))
    vec3 = lambda a: a.reshape(depth, 1, -1)
    g_mix, g_ffn = vec3(g_norm_mix), vec3(g_norm_ffn)
    spool, bdw, gln, bln = vec3(s_pool), vec3(b_dw), vec3(g_conv_ln), vec3(b_conv_ln)
    gf = g_final.reshape(1, d)

    tabs_p = _rope_tables(jnp.arange(tp))
    tabs_s = _rope_tables(jnp.tile(past_len + jnp.arange(ts), bd))
    eneg_p = _block_bias(tp, 1)
    n_keys_s = past_len + LANES
    eneg_s = _block_bias(n_keys_s, 2).T
    key = jnp.arange(n_keys_s)
    blk_of_key = (key[:, None] // L_CMP == jnp.arange(LANES)[None, :]).astype(F32)
    selk = (w_cmp_k[:, key % L_CMP, None] * blk_of_key[None]).astype(BF16)
    selv = (w_cmp_v[:, key % L_CMP, None] * blk_of_key[None]).astype(BF16)

    ada = _ada_all(jnp.concatenate([c_prompt, c_sample], axis=0), w_ada, b_ada)
    pos_minor = lambda a: a.transpose(0, 1, 3, 4, 2).reshape(a.shape[0] * a.shape[1], KV_W, a.shape[2])
    caches = tuple(pos_minor(c) for c in (cache_cmp_k, cache_cmp_v, cache_slc_k, cache_slc_v))
    win_k_all, win_v_all = pos_minor(state_win_k), pos_minor(state_win_v)
    new_win = [jnp.zeros(win_k_all.shape, F32), jnp.zeros(win_v_all.shape, F32)]

    xp = x_prompt
    xs = x_sample.reshape(1, bd * ts, d)
    st_p, st_s = [], []
    yp = ys = None
    for l in range(depth):
        final = l == depth - 1
        mp = ada[l, :bp][:, None, :]
        ms = jnp.repeat(ada[l, bp:], ts, axis=0)[None]

        (up, uc, q, kc, vc, ks, vs, kw, vw, gn, ck, cv, ks_b, vs_b, kw_b, vw_b) = _inproj(
            xp, mp, g_mix, w_in_a, l, tabs_p, wck, wcv, w_pool=wp_dim, w_conv=wc_dim, with_cmp=True,
            q_dtype=BF16, tm=512)
        a_o, b_o, cst = _mix_prompt(up, uc, wpool_bd, spool, w_dw, bdw, gln, bln, l, tt=512)
        n_cmp = tp // L_CMP
        perm = lambda a: a.reshape(bp, n_cmp // 2, 2, KV_W).transpose(0, 2, 1, 3).reshape(bp, n_cmp, KV_W).astype(BF16)
        c_o = _nsa_prompt(q, gn, perm(ck), perm(cv), ks_b, vs_b, kw_b, vw_b, eneg_p)
        xp = _merge(xp, mp, g_mix, a_o, b_o, c_o, w_gm, wpo, wco, wao, wo, l, tm=512)
        res = _ffn(xp, mp, g_ffn, wgu, wdn, gf, l, tm=512, final=final)
        xp = res[0]
        if final:
            yp = res[1]
        kv4 = lambda a: a.reshape(a.shape[0], a.shape[1], N_KV, HEAD_DIM)
        wlp = min(WINDOW, tp)
        st_p.append((kv4(kc), kv4(vc), kv4(ks), kv4(vs), kv4(kw[:, tp - wlp:]), kv4(vw[:, tp - wlp:]),
                     up[:, tp - (max(POOL_WINDOWS) - 1):], cst[:, HALO - (CONV_K - 1):]))

        (up, uc, q, kc, vc, ks, vs, kw, vw, gn) = _inproj(
            xs, ms, g_mix, w_in_a, l, tabs_s, wck, wcv, w_pool=wp_dim, w_conv=wc_dim, with_cmp=False,
            q_dtype=F32, tm=512)
        tmaj = lambda a: a.reshape(bd, ts, a.shape[-1]).transpose(1, 0, 2)
        a_t, b_t, glu_t = _mix_sample(state_pool[l].transpose(1, 0, 2), tmaj(up), state_conv[l].transpose(1, 0, 2),
                                      tmaj(uc), wpool_bd, spool, w_dw, bdw, gln, bln, l)
        bmaj = lambda a: a.transpose(1, 0, 2).reshape(1, bd * ts, a.shape[-1])
        per_seq = lambda a: a.reshape(bd, ts, a.shape[-1])
        c_o, *new_win = _nsa_sample(page_table, per_seq(q), per_seq(gn),
                                    [per_seq(a) for a in (kc, vc, ks, vs, kw, vw)], win_k_all, win_v_all, caches,
                                    selk, selv, eneg_s, l, new_win, n_pool=n_pool, depth=depth)
        xs = _merge(xs, ms, g_mix, bmaj(a_t), bmaj(b_t), c_o.reshape(1, bd * ts, -1), w_gm, wpo,
                    wco, wao, wo, l, tm=512)
        res = _ffn(xs, ms, g_ffn, wgu, wdn, gf, l, tm=512, final=final)
        xs = res[0]
        if final:
            ys = res[1]
        kv4s = lambda a: a.reshape(bd, -1, N_KV, HEAD_DIM)
        new_pool = jnp.concatenate([state_pool[l][:, ts:], per_seq(up)], axis=1)
        new_conv = jnp.concatenate([state_conv[l][:, ts:], glu_t.transpose(1, 0, 2)], axis=1)
        st_s.append((kv4s(kc), kv4s(vc), kv4s(ks), kv4s(vs), None, None, new_pool, new_conv))

    win_out = [w.reshape(depth, bd, N_KV, HEAD_DIM, wl).transpose(0, 1, 4, 2, 3) for w in new_win]
    outs = [yp, ys.reshape(bd, ts, d)]
    for i in range(8):
        outs.append(jnp.stack([s[i] for s in st_p]))
        outs.append(win_out[i - 4] if i in (4, 5) else jnp.stack([s[i] for s in st_s]))
    return tuple(outs)
```

```python
import functools

import jax
import jax.numpy as jnp
from jax import lax
from jax.experimental import pallas as pl
from jax.experimental.pallas import tpu as pltpu

F32 = jnp.float32
BF16 = jnp.bfloat16

HEAD_DIM = 64
N_KV = 2
GQA = 4
N_HEADS = N_KV * GQA
ROPE_DIM = 16
ROPE_THETA = 500000.0
L_CMP = 32
L_SLC = 64
L_SLC_SHIFT = 6
N_SEL = 16
WINDOW = 512
Q_BLOCK = 128
PAGE_SIZE = 128
POOL_WINDOWS = (2, 4, 8, 16)
CONV_K = 31
EPS = 1e-6
NEG = -1e30
FORCE_SCORE = 1e4

LANES = 128
SUBLANES = 8
VMEM_LIMIT_BYTES = 56 * 1024 * 1024

NSA_CHUNK_KEYS = 512
NSA_PROMPT_BLOCKS = 2
NSA_SAMPLE_SEQS = 4
NSA_ROW_TILE = 32
ROW_TILE = 512
ADA_COL_TILE = 1536
POOL_WINDOWS_GW = 64
HALO = 32
KV_W = N_KV * HEAD_DIM


def _params(*sem):
    return pltpu.CompilerParams(dimension_semantics=sem, vmem_limit_bytes=VMEM_LIMIT_BYTES)


def _modnorm(x, g, sc, sh):
    ms = jnp.mean(x * x, axis=-1, keepdims=True)
    return (x * lax.rsqrt(ms + EPS) * g) * (1.0 + sc) + sh


def _sigmoid(x):
    return 1.0 / (1.0 + jnp.exp(-x))


def _dot_nt(a, b):
    return lax.dot_general(a, b, (((1,), (1,)), ((), ())), preferred_element_type=F32)


def _masked_softmax(s, mask):
    sm = jnp.where(mask, s, NEG)
    e = jnp.exp(sm - jnp.max(sm, axis=-1, keepdims=True))
    return jnp.where(mask, e / jnp.sum(e, axis=-1, keepdims=True), 0.0)


def _ada_kernel(c_ref, w_ref, b_ref, o_ref):
    c = c_ref[...]
    a = (c * _sigmoid(c)).astype(BF16)
    o_ref[...] = jnp.dot(a, w_ref[...].astype(BF16), preferred_element_type=F32) + b_ref[...]


def _ada_all(c_all, w_ada, b_ada):
    depth, d, n = w_ada.shape
    tn = ADA_COL_TILE
    assert n % tn == 0
    nb = c_all.shape[0]
    return pl.pallas_call(
        _ada_kernel,
        grid=(depth, n // tn),
        in_specs=[
            pl.BlockSpec((nb, d), lambda l, j: (0, 0)),
            pl.BlockSpec((None, d, tn), lambda l, j: (l, 0, j)),
            pl.BlockSpec((None, 1, tn), lambda l, j: (l, 0, j)),
        ],
        out_specs=pl.BlockSpec((None, nb, tn), lambda l, j: (l, 0, j)),
        out_shape=jax.ShapeDtypeStruct((depth, nb, n), F32),
        compiler_params=_params("parallel", "parallel"),
        name="ada_mod",
    )(c_all, w_ada, b_ada.reshape(depth, 1, n))


def _inproj_kernel(x_ref, sc_ref, sh_ref, g_ref, w_ref, cos_ref, sa_ref, sb_ref, wck_ref, wcv_ref, *outs,
                   w_pool, w_conv, with_cmp):
    (up_ref, uc_ref, q_ref, kc_ref, vc_ref, ks_ref, vs_ref, kw_ref, vw_ref, gn_ref) = outs[:10]
    h = _modnorm(x_ref[0], g_ref[...], sc_ref[0], sh_ref[0]).astype(BF16)
    y = jnp.dot(h, w_ref[...], preferred_element_type=F32)
    cos, sa, sb = cos_ref[...], sa_ref[...], sb_ref[...]

    def rope(t):
        return t * cos + pltpu.roll(t, LANES - ROPE_DIM // 2, 1) * sa + pltpu.roll(t, ROPE_DIM // 2, 1) * sb

    o = 0
    up_ref[0] = y[:, o:o + w_pool]
    o += w_pool
    uc_ref[0] = y[:, o:o + 2 * w_conv]
    o += 2 * w_conv
    for j in range(N_HEADS * HEAD_DIM // LANES):
        q_ref[0, :, j * LANES:(j + 1) * LANES] = rope(y[:, o:o + LANES]).astype(q_ref.dtype)
        o += LANES
    kc = rope(y[:, o:o + KV_W])
    vc = y[:, o + KV_W:o + 2 * KV_W]
    ks = rope(y[:, o + 2 * KV_W:o + 3 * KV_W])
    vs = y[:, o + 3 * KV_W:o + 4 * KV_W]
    kw = rope(y[:, o + 4 * KV_W:o + 5 * KV_W])
    vw = y[:, o + 5 * KV_W:o + 6 * KV_W]
    o += 6 * KV_W
    kc_ref[0], vc_ref[0], ks_ref[0], vs_ref[0], kw_ref[0], vw_ref[0] = kc, vc, ks, vs, kw, vw
    gn_ref[0] = y[:, o:o + LANES]
    if with_cmp:
        ck_ref, cv_ref = outs[10:12]
        tm = kc.shape[0]
        ck_ref[0] = jnp.sum(kc.reshape(tm // L_CMP, L_CMP, KV_W) * wck_ref[...][None], axis=1)
        cv_ref[0] = jnp.sum(vc.reshape(tm // L_CMP, L_CMP, KV_W) * wcv_ref[...][None], axis=1)
        for ref, val in zip(outs[12:], (ks, vs, kw, vw)):
            ref[0] = val.astype(BF16)


def _mod_spec(mods, col, tm, d):
    if mods.shape[1] == 1:
        return pl.BlockSpec((1, 1, d), lambda bi, i: (bi, 0, col))
    return pl.BlockSpec((1, tm, d), lambda bi, i: (bi, i, col))


def _inproj(x, mods, g, w_a, l, tabs, wck, wcv, *, w_pool, w_conv, with_cmp, q_dtype, tm):
    b, t, d = x.shape
    n = w_a.shape[-1]
    tm = min(tm, t)
    assert t % tm == 0 and (tm % L_CMP == 0 or not with_cmp)
    tab_spec = pl.BlockSpec((tm, LANES), lambda bi, i: (i, 0))
    row = lambda w: pl.BlockSpec((1, tm, w), lambda bi, i: (bi, i, 0))
    widths = [w_pool, 2 * w_conv, N_HEADS * HEAD_DIM] + [KV_W] * 6 + [LANES]
    dtypes = [F32, F32, q_dtype] + [F32] * 7
    out_specs = [row(w) for w in widths]
    out_shape = [jax.ShapeDtypeStruct((b, t, w), dt) for w, dt in zip(widths, dtypes)]
    if with_cmp:
        out_specs += [pl.BlockSpec((1, tm // L_CMP, KV_W), lambda bi, i: (bi, i, 0))] * 2 + [row(KV_W)] * 4
        out_shape += ([jax.ShapeDtypeStruct((b, t // L_CMP, KV_W), F32)] * 2
                      + [jax.ShapeDtypeStruct((b, t, KV_W), BF16)] * 4)
    return pl.pallas_call(
        functools.partial(_inproj_kernel, w_pool=w_pool, w_conv=w_conv, with_cmp=with_cmp),
        grid=(b, t // tm),
        in_specs=[
            row(d), _mod_spec(mods, 1, tm, d), _mod_spec(mods, 0, tm, d),
            pl.BlockSpec((None, 1, d), lambda bi, i: (l, 0, 0)),
            pl.BlockSpec((None, d, n), lambda bi, i: (l, 0, 0)),
            tab_spec, tab_spec, tab_spec,
            pl.BlockSpec((None, L_CMP, KV_W), lambda bi, i: (l, 0, 0)),
            pl.BlockSpec((None, L_CMP, KV_W), lambda bi, i: (l, 0, 0)),
        ],
        out_specs=out_specs,
        out_shape=out_shape,
        compiler_params=_params("parallel", "parallel"),
        name="in_proj",
    )(x, mods, mods, g, w_a, *tabs, wck, wcv)


def _pool_means(ext_ref, lvl_ref, base, rows, pos0):
    assert POOL_WINDOWS == (2, 4, 8, 16) and base == 4 * SUBLANES
    n = base + rows
    lane = lax.broadcasted_iota(jnp.int32, (rows, LANES), 1)
    pos = lax.broadcasted_iota(jnp.int32, (rows, LANES), 0) + pos0
    low = lane < POOL_WINDOWS_GW
    means = []
    for tile in range(len(POOL_WINDOWS) // 2):
        cols = slice(tile * LANES, (tile + 1) * LANES)
        w_small, w_big = POOL_WINDOWS[2 * tile], POOL_WINDOWS[2 * tile + 1]
        sums = {}
        src, w, lo = None, 1, 0
        while w < w_big:
            lo += SUBLANES
            if src is None:
                cur = ext_ref[lo:n, cols] + ext_ref[lo - w:n - w, cols]
            else:
                cur = lvl_ref[src, lo:n] + lvl_ref[src, lo - w:n - w]
            w *= 2
            sums[w] = cur[base - lo:]
            if w < w_big:
                src = 0 if src != 0 else 1
                lvl_ref[src, lo:n] = cur
        cnt = jnp.minimum(jnp.where(low, w_small, w_big), pos + 1).astype(F32)
        means.append(jnp.where(low, sums[w_small], sums[w_big]) / cnt)
    return jnp.concatenate(means, axis=1)


def _layernorm_silu(y, g, b):
    mu = jnp.mean(y, axis=-1, keepdims=True)
    yc = y - mu
    var = jnp.mean(yc * yc, axis=-1, keepdims=True)
    z = yc * lax.rsqrt(var + EPS) * g + b
    return z * _sigmoid(z)


def _mix_prompt_kernel(up_ref, uph_ref, uc_ref, uch_ref, wpool_ref, spool_ref, wdw_ref, bdw_ref, gln_ref, bln_ref,
                       a_ref, b_ref, cst_ref, pext_ref, cext_ref, part_ref, lvl_ref, *, tt, w_conv):
    i = pl.program_id(1)
    first = i == 0
    u = up_ref[0]
    pext_ref[0:HALO] = jnp.where(first, 0.0, uph_ref[0])
    pext_ref[HALO:HALO + tt] = u
    d = _pool_means(pext_ref, lvl_ref, HALO, tt, i * tt) - u
    z = jnp.dot(d.astype(BF16), wpool_ref[...], preferred_element_type=F32) * spool_ref[...]
    a_ref[0] = z.astype(a_ref.dtype)
    hc = uch_ref[0]
    cext_ref[0:HALO] = jnp.where(first, 0.0, hc[:, :w_conv] * _sigmoid(hc[:, w_conv:]))
    uc = uc_ref[0]
    cext_ref[HALO:HALO + tt] = uc[:, :w_conv] * _sigmoid(uc[:, w_conv:])
    off = HALO - (CONV_K - 1)
    acc = None
    for m in range(SUBLANES):
        n_rows = tt + SUBLANES * (-(-(off + m) // SUBLANES))
        part = None
        for k in range(m, CONV_K, SUBLANES):
            assert k - m + n_rows <= HALO + tt
            term = wdw_ref[k:k + 1, :] * cext_ref[k - m:k - m + n_rows]
            part = term if part is None else part + term
        part_ref[m, 0:n_rows] = part
        shifted = part_ref[m, off + m:off + m + tt]
        acc = shifted if acc is None else acc + shifted
    b_ref[0] = _layernorm_silu(acc + bdw_ref[...], gln_ref[...], bln_ref[...]).astype(b_ref.dtype)
    cst_ref[0] = cext_ref[tt:tt + HALO]


def _mix_prompt(up, uc, wpool_bd, spool, wdw, bdw, gln, bln, l, *, tt):
    b, t, w_pool = up.shape
    w_conv = uc.shape[-1] // 2
    tt = min(tt, t)
    assert t % tt == 0 and tt % HALO == 0
    r = tt // HALO
    halo = lambda w: pl.BlockSpec((1, HALO, w), lambda bi, i: (bi, jnp.maximum(i * r - 1, 0), 0))
    vec = lambda w: pl.BlockSpec((None, 1, w), lambda bi, i: (l, 0, 0))
    return pl.pallas_call(
        functools.partial(_mix_prompt_kernel, tt=tt, w_conv=w_conv),
        grid=(b, t // tt),
        in_specs=[
            pl.BlockSpec((1, tt, w_pool), lambda bi, i: (bi, i, 0)), halo(w_pool),
            pl.BlockSpec((1, tt, 2 * w_conv), lambda bi, i: (bi, i, 0)), halo(2 * w_conv),
            pl.BlockSpec((None, w_pool, w_pool), lambda bi, i: (l, 0, 0)), vec(w_pool),
            pl.BlockSpec((None, CONV_K, w_conv), lambda bi, i: (l, 0, 0)), vec(w_conv), vec(w_conv), vec(w_conv),
        ],
        out_specs=[
            pl.BlockSpec((1, tt, w_pool), lambda bi, i: (bi, i, 0)),
            pl.BlockSpec((1, tt, w_conv), lambda bi, i: (bi, i, 0)),
            pl.BlockSpec((1, HALO, w_conv), lambda bi, i: (bi, 0, 0)),
        ],
        out_shape=[
            jax.ShapeDtypeStruct((b, t, w_pool), BF16),
            jax.ShapeDtypeStruct((b, t, w_conv), BF16),
            jax.ShapeDtypeStruct((b, HALO, w_conv), F32),
        ],
        scratch_shapes=[pltpu.VMEM((HALO + tt, w_pool), F32), pltpu.VMEM((HALO + tt, w_conv), F32),
                        pltpu.VMEM((SUBLANES, tt + 2 * SUBLANES, w_conv), F32),
                        pltpu.VMEM((2, HALO + tt, LANES), F32)],
        compiler_params=_params("parallel", "arbitrary"),
        name="mix_prompt",
    )(up, up, uc, uc, wpool_bd, spool, wdw, bdw, gln, bln)


def _mix_sample_kernel(pst_ref, up_ref, cst_ref, uc_ref, wpool_ref, spool_ref, wdw_ref, bdw_ref, gln_ref, bln_ref,
                       a_ref, b_ref, glu_ref, *, w_conv):
    n_p, n_c, ts = pst_ref.shape[0], cst_ref.shape[0], up_ref.shape[0]
    prow = lambda r: pst_ref[r] if r < n_p else up_ref[r - n_p]
    low = lax.broadcasted_iota(jnp.int32, (up_ref.shape[1], LANES), 1) < POOL_WINDOWS_GW
    glu = []
    for t in range(ts):
        uc = uc_ref[t]
        glu.append(uc[:, :w_conv] * _sigmoid(uc[:, w_conv:]))
        glu_ref[t] = glu[t]
    crow = lambda r: cst_ref[r] if r < n_c else glu[r - n_c]
    for t in range(ts):
        u = up_ref[t]
        means = None
        acc = u
        sums = {1: acc}
        for s in range(1, max(POOL_WINDOWS)):
            acc = acc + prow(n_p + t - s)
            sums[s + 1] = acc
        tiles = []
        for tile in range(len(POOL_WINDOWS) // 2):
            cols = slice(tile * LANES, (tile + 1) * LANES)
            w_small, w_big = POOL_WINDOWS[2 * tile], POOL_WINDOWS[2 * tile + 1]
            tiles.append(jnp.where(low, sums[w_small][:, cols] / float(w_small),
                                   sums[w_big][:, cols] / float(w_big)))
        means = jnp.concatenate(tiles, axis=1)
        z = jnp.dot((means - u).astype(BF16), wpool_ref[...], preferred_element_type=F32) * spool_ref[...]
        a_ref[t] = z.astype(a_ref.dtype)
        acc = wdw_ref[0:1, :] * crow(t + n_c - (CONV_K - 1))
        for k in range(1, CONV_K):
            acc = acc + wdw_ref[k:k + 1, :] * crow(t + n_c - (CONV_K - 1) + k)
        b_ref[t] = _layernorm_silu(acc + bdw_ref[...], gln_ref[...], bln_ref[...]).astype(b_ref.dtype)


def _mix_sample(pst_t, up_t, cst_t, uc_t, wpool_bd, spool, wdw, bdw, gln, bln, l):
    ts, bd, w_pool = up_t.shape
    w_conv = uc_t.shape[-1] // 2
    full = lambda a: pl.BlockSpec(a.shape, lambda i: (0,) * a.ndim)
    vec = lambda w: pl.BlockSpec((None, 1, w), lambda i: (l, 0, 0))
    return pl.pallas_call(
        functools.partial(_mix_sample_kernel, w_conv=w_conv),
        grid=(1,),
        in_specs=[
            full(pst_t), full(up_t), full(cst_t), full(uc_t),
            pl.BlockSpec((None, w_pool, w_pool), lambda i: (l, 0, 0)), vec(w_pool),
            pl.BlockSpec((None, CONV_K, w_conv), lambda i: (l, 0, 0)), vec(w_conv), vec(w_conv), vec(w_conv),
        ],
        out_specs=[pl.BlockSpec((ts, bd, w_pool), lambda i: (0, 0, 0)),
                   pl.BlockSpec((ts, bd, w_conv), lambda i: (0, 0, 0)),
                   pl.BlockSpec((ts, bd, w_conv), lambda i: (0, 0, 0))],
        out_shape=[jax.ShapeDtypeStruct((ts, bd, w_pool), BF16),
                   jax.ShapeDtypeStruct((ts, bd, w_conv), BF16),
                   jax.ShapeDtypeStruct((ts, bd, w_conv), F32)],
        compiler_params=_params("arbitrary"),
        name="mix_sample",
    )(pst_t, up_t, cst_t, uc_t, wpool_bd, spool, wdw, bdw, gln, bln)


def _topk_masks(scores, n_sel):
    sts = tuple(s.T for s in scores)
    cand = lax.broadcasted_iota(jnp.int32, sts[0].shape, 0).astype(F32)

    def body(_, carry):
        out = []
        for st, sel in carry:
            m = jnp.max(st, axis=0, keepdims=True)
            first = jnp.min(jnp.where(st == m, cand, float(LANES)), axis=0, keepdims=True)
            hit = cand == first
            out.append((jnp.where(hit, -jnp.inf, st), jnp.where(hit, 1.0, sel)))
        return tuple(out)

    res = lax.fori_loop(0, n_sel, body, tuple((st, jnp.zeros(st.shape, F32)) for st in sts))
    return [sel.T for _, sel in res]


def _head_rows(tile, src_half, dst_half, lane):
    if src_half != dst_half:
        tile = pltpu.roll(tile, HEAD_DIM, 1)
    return jnp.where((lane >= dst_half * HEAD_DIM) & (lane < (dst_half + 1) * HEAD_DIM), tile, 0.0)


def _nsa_prompt_multi_kernel(q_ref, gn_ref, ck_ref, cv_ref, ks_ref, vs_ref, kw_ref, vw_ref, eneg_ref, o_ref,
                             qa_scr, s_scr, p_scr, sw_scr, pw_scr, *, ck_keys, n_sub):
    step = pl.program_id(1)
    qb = Q_BLOCK
    rows = GQA * qb
    scale = HEAD_DIM ** -0.5
    n_cmp = ck_ref.shape[1]
    n_slc = n_cmp // 2
    n_wchunks = WINDOW // qb + 1
    wk = n_wchunks * qb
    lane = lax.broadcasted_iota(jnp.int32, (qb, LANES), 1)
    chains = [(u, k) for u in range(n_sub) for k in range(N_KV)]

    def qblk(u):
        return step * n_sub + u

    def qpos(u, shape, row0=0):
        return qblk(u) * qb + ((row0 + lax.broadcasted_iota(jnp.int32, shape, 0)) & (qb - 1))

    def qrows(u, ref, lo, hi):
        return ref[0, u * qb:(u + 1) * qb, lo:hi]

    st = {}

    def phase1(u):
        cur = qpos(u, lane.shape) >> L_SLC_SHIFT
        valid_s = lane <= cur
        forced = (lane == 0) | (lane == cur) | (lane == cur - 1)
        st[u, "valid"] = valid_s
        for k in range(N_KV):
            qh = []
            for g in range(GQA):
                h = k * GQA + g
                tile = qrows(u, q_ref, (h // 2) * LANES, (h // 2 + 1) * LANES).astype(F32) * scale
                qh.append(_head_rows(tile, h % 2, k, lane))
            qk = jnp.concatenate(qh, axis=0).astype(BF16)
            st[u, k, "qk"] = qk
            s_c = _dot_nt(qk, ck_ref[0])
            yield
            cl = lax.broadcasted_iota(jnp.int32, (rows, n_cmp), 1)
            c_end = jnp.where(cl < n_slc, cl * L_SLC + L_CMP - 1, (cl - n_slc) * L_SLC + L_SLC - 1)
            p_c = _masked_softmax(s_c, c_end <= qpos(u, s_c.shape))
            yield
            st[u, k, "o_c"] = jnp.dot(p_c.astype(BF16), cv_ref[0], preferred_element_type=F32)
            ps = p_c[0:qb]
            for g in range(1, GQA):
                ps = ps + p_c[g * qb:(g + 1) * qb]
            imp = ps[:, :n_slc] + ps[:, n_slc:]
            if n_slc < LANES:
                imp = jnp.concatenate([imp, jnp.zeros((qb, LANES - n_slc), F32)], axis=1)
            score = jnp.where(valid_s, jnp.where(forced, FORCE_SCORE, imp), -1.0)
            st[u, k, "score"] = jnp.where(lane < n_slc, score, -jnp.inf)
            yield

    _lockstep(phase1(u) for u in range(n_sub))

    for u in range(n_sub):
        for k, sel in enumerate(_topk_masks([st[u, k, "score"] for k in range(N_KV)], min(N_SEL, n_slc))):
            notsel = (1.0 - jnp.where(st[u, "valid"], sel, 0.0)).astype(BF16)
            qa_scr[u * N_KV + k] = jnp.concatenate([st[u, k, "qk"], jnp.concatenate([notsel] * GQA, axis=0)], axis=1)

    own = lax.broadcasted_iota(jnp.int32, (ck_keys, LANES), 1) < HEAD_DIM

    def scores(c, buf):
        start = pl.multiple_of(c * ck_keys, ck_keys)
        k_aug = jnp.concatenate([ks_ref[0, pl.ds(start, ck_keys), :], eneg_ref[pl.ds(start, ck_keys), :]], axis=1)
        nc = len(chains)
        s_all = _dot_nt(qa_scr[...].reshape(nc * rows, 2 * LANES), k_aug)
        s_scr[buf] = s_all.reshape(nc, rows, ck_keys)

    def probs(c, buf, ms, causal):
        new_m, alphas = [], []
        for n, (u, k) in enumerate(chains):
            m_tiles = []
            for r in range(0, rows, NSA_ROW_TILE):
                rs = slice(r, r + NSA_ROW_TILE)
                s = s_scr[buf, n, rs]
                if causal:
                    kpos = c * ck_keys + lax.broadcasted_iota(jnp.int32, s.shape, 1)
                    s = jnp.where(kpos <= qpos(u, s.shape, r), s, NEG)
                m_new = jnp.maximum(ms[n][rs], jnp.max(s, axis=-1, keepdims=True))
                p_scr[buf, n, rs] = jnp.exp(s - m_new).astype(BF16)
                m_tiles.append(m_new)
            m_new = jnp.concatenate(m_tiles, axis=0)
            alphas.append(jnp.exp(ms[n] - m_new))
            new_m.append(m_new)
        return new_m, alphas

    def values(c, buf, alphas, accs):
        start = pl.multiple_of(c * ck_keys, ck_keys)
        v = vs_ref[0, pl.ds(start, ck_keys), :]
        v_one = [jnp.where(own, v, 1.0), jnp.where(own, 1.0, v)]
        return [alphas[n] * accs[n] + jnp.dot(p_scr[buf, n], v_one[k], preferred_element_type=F32)
                for n, (u, k) in enumerate(chains)]

    def pair(c, carry, causal, last):
        ms, alphas, accs = carry
        accs = values(jnp.maximum(c - 1, 0), 1, alphas, accs)
        ms, alphas = probs(c, 0, ms, causal)
        scores(c + 1, 1)
        accs = values(c, 0, alphas, accs)
        ms, alphas = probs(c + 1, 1, ms, causal)
        if last:
            accs = values(c + 1, 1, alphas, accs)
        else:
            scores(c + 2, 0)
        return ms, alphas, accs

    n_pairs = (qblk(n_sub - 1) * qb + qb + 2 * ck_keys - 1) // (2 * ck_keys)
    p_scr[1] = jnp.zeros(p_scr.shape[1:], BF16)
    scores(0, 0)
    nc = len(chains)
    carry = ([jnp.full((rows, 1), -jnp.inf, F32)] * nc, [jnp.ones((rows, 1), F32)] * nc,
             [jnp.zeros((rows, LANES), F32)] * nc)
    carry = lax.fori_loop(0, n_pairs - 1, lambda j, cr: pair(2 * j, cr, False, False), carry)
    c_last = 2 * (n_pairs - 1)

    def last_single(cr):
        ms, alphas, accs = cr
        accs = values(jnp.maximum(c_last - 1, 0), 1, alphas, accs)
        ms, alphas = probs(c_last, 0, ms, True)
        return values(c_last, 0, alphas, accs)

    n_chunks = (qblk(n_sub - 1) * qb + qb + ck_keys - 1) // ck_keys
    accs = lax.cond(n_chunks == c_last + 1, last_single, lambda cr: pair(c_last, cr, True, True)[2], carry)
    o_ss = [acc / pltpu.roll(acc, HEAD_DIM, 1) for acc in accs]

    def tail(u):
        i = qblk(u)
        kws, vws = [], []
        for dj in range(n_wchunks):
            st0 = pl.multiple_of(jnp.maximum(i - (n_wchunks - 1) + dj, 0) * qb, qb)
            kws.append(kw_ref[0, pl.ds(st0, qb), :])
            vws.append(vw_ref[0, pl.ds(st0, qb), :])
        kw_blk, vw_blk = jnp.concatenate(kws, axis=0), jnp.concatenate(vws, axis=0)
        kwpos = (i - (n_wchunks - 1)) * qb + lax.broadcasted_iota(jnp.int32, (qb, wk), 1)
        qp = qpos(u, (qb, wk))
        wbias = jnp.where((kwpos <= qp) & (kwpos > qp - WINDOW) & (kwpos >= 0), 0.0, NEG)
        own_w = lax.broadcasted_iota(jnp.int32, (wk, LANES), 1) < HEAD_DIM
        vw_one = [jnp.where(own_w, vw_blk, 1.0), jnp.where(own_w, 1.0, vw_blk)]
        for k in range(N_KV):
            sw_scr[u * N_KV + k] = _dot_nt(st[u, k, "qk"], kw_blk)
        yield
        o_ws = []
        for k in range(N_KV):
            n = u * N_KV + k
            for r in range(0, rows, NSA_ROW_TILE):
                rs = slice(r, r + NSA_ROW_TILE)
                s = sw_scr[n, rs] + wbias[r % qb:r % qb + NSA_ROW_TILE]
                pw_scr[n, rs] = jnp.exp(s - jnp.max(s, axis=-1, keepdims=True)).astype(BF16)
            yield
            acc = jnp.dot(pw_scr[n], vw_one[k], preferred_element_type=F32)
            o_ws.append(acc / pltpu.roll(acc, HEAD_DIM, 1))
            yield
        sig = _sigmoid(qrows(u, gn_ref, 0, LANES))
        out_tiles = [None] * (N_HEADS // 2)
        for k in range(N_KV):
            o_c, o_s, o_w = st[u, k, "o_c"], o_ss[u * N_KV + k], o_ws[k]
            for g in range(GQA):
                h = k * GQA + g
                col = 3 * h
                r = slice(g * qb, (g + 1) * qb)
                o = (sig[:, col:col + 1] * o_c[r] + sig[:, col + 1:col + 2] * o_s[r]
                     + sig[:, col + 2:col + 3] * o_w[r])
                o = _head_rows(o, k, h % 2, lane)
                out_tiles[h // 2] = o if out_tiles[h // 2] is None else out_tiles[h // 2] + o
            yield
        for j, tile in enumerate(out_tiles):
            o_ref[0, u * qb:(u + 1) * qb, j * LANES:(j + 1) * LANES] = tile.astype(o_ref.dtype)

    _lockstep(tail(u) for u in range(n_sub))


def _nsa_prompt(q, gn, ckp, cvp, ks, vs, kw, vw, eneg):
    b, t, _ = q.shape
    assert t % Q_BLOCK == 0 and t // L_SLC <= LANES
    ck_keys = min(NSA_CHUNK_KEYS, t // 2)
    assert t % (2 * ck_keys) == 0
    rows = GQA * Q_BLOCK
    n_sub = NSA_PROMPT_BLOCKS
    assert (2 * ck_keys) % (n_sub * Q_BLOCK) == 0 and t % (n_sub * Q_BLOCK) == 0
    nc = n_sub * N_KV
    qspec = lambda w: pl.BlockSpec((1, n_sub * Q_BLOCK, w), lambda bi, i: (bi, i, 0))
    seq = lambda n: pl.BlockSpec((1, n, KV_W), lambda bi, i: (bi, 0, 0))
    return pl.pallas_call(
        functools.partial(_nsa_prompt_multi_kernel, ck_keys=ck_keys, n_sub=n_sub),
        scratch_shapes=[pltpu.VMEM((nc, rows, 2 * LANES), BF16), pltpu.VMEM((2, nc, rows, ck_keys), F32),
                        pltpu.VMEM((2, nc, rows, ck_keys), BF16),
                        pltpu.VMEM((nc, rows, WINDOW + Q_BLOCK), F32),
                        pltpu.VMEM((nc, rows, WINDOW + Q_BLOCK), BF16)],
        grid=(b, t // (n_sub * Q_BLOCK)),
        in_specs=[qspec(N_HEADS * HEAD_DIM), qspec(LANES), seq(t // L_CMP), seq(t // L_CMP),
                  seq(t), seq(t), seq(t), seq(t), pl.BlockSpec((t, LANES), lambda bi, i: (0, 0))],
        out_specs=qspec(N_HEADS * HEAD_DIM),
        out_shape=jax.ShapeDtypeStruct((b, t, N_HEADS * HEAD_DIM), BF16),
        compiler_params=_params("parallel", "parallel"),
        name="nsa_prompt",
    )(q, gn, ckp, cvp, ks, vs, kw, vw, eneg)


def _nsa_sample_kernel(pt_ref, q_ref, gn_ref, kcn_ref, vcn_ref, ksn_ref, vsn_ref, kwn_ref, vwn_ref, wk_ref, wv_ref,
                       selk_ref, selv_ref, eneg_ref, *rest, n_pages, past_len, ts, page0):
    caches = rest[2:6]
    o_ref, nwk_ref, nwv_ref = rest[6:9]
    (past_buf, sems, tail_s, q_s, sc_s, g_s, o_s) = rest[9:]
    n_seq = q_ref.shape[0]
    b = pl.program_id(0)
    slot = b & 1
    scale = HEAD_DIM ** -0.5
    wl = wk_ref.shape[2]
    n_keys = past_len + LANES
    n_slc = -(-(past_len + ts) // L_SLC)
    rows = GQA * N_KV * ts
    grp = N_KV * ts
    lane_t = lax.broadcasted_iota(jnp.int32, (ts, LANES), 1)

    def page_copy(step, dst_slot, u, j, p):
        src = caches[j].at[page0 + pt_ref[(step * n_seq + u) * n_pages + p]]
        dst = past_buf.at[dst_slot, u, j, :, pl.ds(p * PAGE_SIZE, PAGE_SIZE)]
        return pltpu.make_async_copy(src, dst, sems.at[dst_slot])

    def all_pages(step, dst_slot, op):
        for u in range(n_seq):
            for j in range(4):
                for p in range(n_pages):
                    op(page_copy(step, dst_slot, u, j, p))

    @pl.when(b == 0)
    def _():
        all_pages(0, 0, lambda cp: cp.start())

    @pl.when(b + 1 < pl.num_programs(0))
    def _():
        all_pages(b + 1, 1 - slot, lambda cp: cp.start())

    all_pages(b, slot, lambda cp: cp.wait())

    def qpos(shape):
        return past_len + (lax.broadcasted_iota(jnp.int32, shape, 0) & (ts - 1))

    lane_g = lax.broadcasted_iota(jnp.int32, (grp, LANES), 1)
    blk = lane_g >> 1
    is_blk = ((lane_g & 1) == 0) & (blk < n_slc)
    cur = qpos(lane_g.shape) >> L_SLC_SHIFT
    valid_s = blk <= cur
    forced = (blk == 0) | (blk == cur) | (blk == cur - 1)

    def front(u, out):
        def new_cols(j, new_ref):
            tail_s[u, j] = jnp.zeros(tail_s.shape[2:], F32)
            tail_s[u, j, 0:ts] = new_ref[u]
            return tail_s[u, j].T

        kc_p, vc_p, ks_p, vs_p = (past_buf[slot, u, j].astype(BF16) for j in range(4))
        kc_n, vc_n, ks_n, vs_n = (new_cols(j, r).astype(BF16)
                                  for j, r in enumerate((kcn_ref, vcn_ref, ksn_ref, vsn_ref)))

        kw_all = jnp.concatenate([wk_ref[u], new_cols(4, kwn_ref)], axis=1)
        vw_all = jnp.concatenate([wv_ref[u], new_cols(5, vwn_ref)], axis=1)
        nwk_ref[u] = kw_all[:, ts:ts + wl]
        nwv_ref[u] = vw_all[:, ts:ts + wl]
        yield

        for g in range(GQA):
            for k in range(N_KV):
                h = k * GQA + g
                tile = q_ref[u, :, (h // 2) * LANES:(h // 2 + 1) * LANES] * scale
                q_s[u, (g * N_KV + k) * ts:(g * N_KV + k + 1) * ts] = _head_rows(tile, h % 2, k, lane_t)
        qa = q_s[u].astype(BF16)

        def compress(past, new, sel_ref):
            return (jnp.dot(past, sel_ref[0:past_len], preferred_element_type=F32)
                    + jnp.dot(new, sel_ref[past_len:n_keys], preferred_element_type=F32)).astype(BF16)

        ck_t = compress(kc_p, kc_n, selk_ref)
        yield
        cv_t = compress(vc_p, vc_n, selv_ref)
        yield
        s_c = jnp.dot(qa, ck_t, preferred_element_type=F32)
        yield
        cl = lax.broadcasted_iota(jnp.int32, s_c.shape, 1)
        p_c = _masked_softmax(s_c, (cl + 1) * L_CMP - 1 <= qpos(s_c.shape))
        yield
        o_c = _dot_nt(p_c.astype(BF16), cv_t)
        ps = p_c[0:grp]
        for g in range(1, GQA):
            ps = ps + p_c[g * grp:(g + 1) * grp]
        imp = ps + pltpu.roll(ps, LANES - 1, 1)

        score = jnp.where(valid_s, jnp.where(forced, FORCE_SCORE, imp), -1.0)
        sc_s[u * grp:(u + 1) * grp] = jnp.where(is_blk, score, -jnp.inf)
        out[u] = (qa, o_c, ks_p, ks_n, vs_p, vs_n, kw_all, vw_all)

    sc_s[...] = jnp.zeros(sc_s.shape, F32)
    fronts = {}
    _lockstep(front(u, fronts) for u in range(n_seq))
    sel = _topk_masks([sc_s[...]], min(N_SEL, n_slc))[0]
    _lockstep(_nsa_sample_back(u, fronts[u], sel[u * grp:(u + 1) * grp], is_blk & valid_s, qpos, gn_ref, eneg_ref,
                               o_ref, g_s, o_s, past_len=past_len, ts=ts, wl=wl) for u in range(n_seq))


def _lockstep(stages):
    stages = list(stages)
    while stages:
        for gen in list(stages):
            try:
                next(gen)
            except StopIteration:
                stages.remove(gen)


def _nsa_sample_back(u, front, sel, selectable, qpos, gn_ref, eneg_ref, o_ref, g_s, o_s, *, past_len, ts, wl):
    qa, o_c, ks_p, ks_n, vs_p, vs_n, kw_all, vw_all = front
    n_keys = past_len + LANES
    lane_t = lax.broadcasted_iota(jnp.int32, (ts, LANES), 1)
    notsel = (1.0 - jnp.where(selectable, sel, 0.0)).astype(BF16)
    q_aug = jnp.concatenate([qa, jnp.concatenate([notsel] * GQA, axis=0)], axis=1)

    s_s = jnp.concatenate(
        [jnp.dot(q_aug, jnp.concatenate([ks_p, eneg_ref[:, 0:past_len]], axis=0), preferred_element_type=F32),
         jnp.dot(q_aug, jnp.concatenate([ks_n, eneg_ref[:, past_len:n_keys]], axis=0), preferred_element_type=F32)],
        axis=1)
    s_w = jnp.dot(qa, kw_all.astype(BF16), preferred_element_type=F32)
    yield
    kpos = lax.broadcasted_iota(jnp.int32, s_s.shape, 1)
    s_s = jnp.where(kpos <= qpos(s_s.shape), s_s, NEG)
    e = jnp.exp(s_s - jnp.max(s_s, axis=-1, keepdims=True))
    yield
    p_s = (e / jnp.sum(e, axis=-1, keepdims=True)).astype(BF16)
    yield
    o_sel = _dot_nt(p_s[:, 0:past_len], vs_p) + _dot_nt(p_s[:, past_len:n_keys], vs_n)
    yield
    kwpos = past_len - wl + lax.broadcasted_iota(jnp.int32, s_w.shape, 1)
    qp = qpos(s_w.shape)
    okw = (kwpos <= qp) & (kwpos > qp - WINDOW) & (kwpos >= 0)
    p_w = _masked_softmax(s_w, okw)
    yield
    o_w = _dot_nt(p_w.astype(BF16), vw_all.astype(BF16))
    yield

    sig = _sigmoid(gn_ref[u])
    o = None
    for j, ob in enumerate((o_c, o_sel, o_w)):
        for g in range(GQA):
            for k in range(N_KV):
                col = 3 * (k * GQA + g) + j
                g_s[u, j, (g * N_KV + k) * ts:(g * N_KV + k + 1) * ts] = jnp.broadcast_to(sig[:, col:col + 1],
                                                                                        (ts, LANES))
        o = g_s[u, j] * ob if o is None else o + g_s[u, j] * ob
    o_s[u] = o
    for j in range(N_HEADS // 2):
        tile = None
        for h in (2 * j, 2 * j + 1):
            k, g = h // GQA, h % GQA
            piece = _head_rows(o_s[u, (g * N_KV + k) * ts:(g * N_KV + k + 1) * ts], k, h % 2, lane_t)
            tile = piece if tile is None else tile + piece
        o_ref[u, :, j * LANES:(j + 1) * LANES] = tile


def _nsa_sample(page_table, q, gn, new_rows, win_k, win_v, caches, selk, selv, eneg, l, new_win, *, n_pool, depth):
    bd, ts, _ = q.shape
    n_pages = page_table.shape[1]
    past_len = n_pages * PAGE_SIZE
    wl = win_k.shape[2]
    n_keys = past_len + LANES
    assert ts <= LANES and -(-(past_len + ts) // L_SLC) * 2 <= LANES and eneg.shape == (LANES, n_keys)
    rows = GQA * N_KV * ts
    n_seq = NSA_SAMPLE_SEQS
    assert bd % n_seq == 0 and n_seq * N_KV * ts <= LANES
    tok = lambda w: pl.BlockSpec((n_seq, ts, w), lambda b, pt: (b, 0, 0))
    win = pl.BlockSpec((n_seq, KV_W, wl), lambda b, pt: (l * (bd // n_seq) + b, 0, 0))
    sel = pl.BlockSpec((None, n_keys, LANES), lambda b, pt: (l, 0, 0))

    hbm = pl.BlockSpec(memory_space=pl.ANY)
    in_specs = ([tok(N_HEADS * HEAD_DIM), tok(LANES)] + [tok(KV_W)] * 6 + [win, win, sel, sel,
                pl.BlockSpec((LANES, n_keys), lambda b, pt: (0, 0))] + [hbm] * 6)
    grid_spec = pltpu.PrefetchScalarGridSpec(
        num_scalar_prefetch=1,
        grid=(bd // n_seq,),
        in_specs=in_specs,
        out_specs=[tok(N_HEADS * HEAD_DIM), win, win],
        scratch_shapes=[
            pltpu.VMEM((2, n_seq, 4, KV_W, past_len), F32),
            pltpu.SemaphoreType.DMA((2,)),
            pltpu.VMEM((n_seq, 6, LANES, KV_W), F32),
            pltpu.VMEM((n_seq, rows, LANES), F32), pltpu.VMEM((LANES, LANES), F32),
            pltpu.VMEM((n_seq, 3, rows, LANES), F32), pltpu.VMEM((n_seq, rows, LANES), F32),
        ],
    )
    win_shape = jax.ShapeDtypeStruct((depth * bd, KV_W, wl), F32)
    n_fixed = 1 + 2 + 6 + 5
    return pl.pallas_call(
        functools.partial(_nsa_sample_kernel, n_pages=n_pages, past_len=past_len, ts=ts, page0=l * n_pool),
        grid_spec=grid_spec,
        out_shape=[jax.ShapeDtypeStruct((bd, ts, N_HEADS * HEAD_DIM), F32), win_shape, win_shape],
        input_output_aliases={n_fixed: 1, n_fixed + 1: 2},
        compiler_params=_params("arbitrary"),
        name="nsa_sample",
    )(page_table.reshape(-1), q, gn, *new_rows, win_k, win_v, selk, selv, eneg, *new_win, *caches)


def _merge_ffn_kernel(x_ref, sc1_ref, sh1_ref, gt1_ref, sc2_ref, sh2_ref, gt2_ref, g1_ref, g2_ref, a_ref, b_ref, c_ref,
                      wgm_ref, wp_ref, wc_ref, wa_ref, wo_ref, wgu_ref, wd_ref, gf_ref, *outs, d_ff, n_parts, final):
    x = x_ref[0]
    d = x.shape[-1]
    h = _modnorm(x, g1_ref[...], sc1_ref[0], sh1_ref[0]).astype(BF16)
    branches = (
        jnp.dot(a_ref[0].astype(BF16), wp_ref[...], preferred_element_type=F32),
        jnp.dot(b_ref[0].astype(BF16), wc_ref[...], preferred_element_type=F32),
        jnp.dot(c_ref[0].astype(BF16), wa_ref[...], preferred_element_type=F32),
    )
    merged = None
    for j, br in enumerate(branches):
        gm = _sigmoid(jnp.dot(h, wgm_ref[:, j * d:(j + 1) * d], preferred_element_type=F32))
        merged = gm * br if merged is None else merged + gm * br
    x = x + gt1_ref[0] * jnp.dot(merged.astype(BF16), wo_ref[...], preferred_element_type=F32)

    h = _modnorm(x, g2_ref[...], sc2_ref[0], sh2_ref[0]).astype(BF16)
    part = d_ff // n_parts
    acc = None
    for c in range(n_parts):
        gp = jnp.dot(h, wgu_ref[:, c * part:(c + 1) * part], preferred_element_type=F32)
        up = jnp.dot(h, wgu_ref[:, d_ff + c * part:d_ff + (c + 1) * part], preferred_element_type=F32)
        act = (gp * _sigmoid(gp) * up).astype(BF16)
        dn = jnp.dot(act, wd_ref[c * part:(c + 1) * part, :], preferred_element_type=F32)
        acc = dn if acc is None else acc + dn
    y = x + gt2_ref[0] * acc
    outs[0][0] = y
    if final:
        ms = jnp.mean(y * y, axis=-1, keepdims=True)
        outs[1][0] = y * lax.rsqrt(ms + EPS) * gf_ref[...]


def _merge_ffn(x, mods, g1, g2, a, bo, c, wgm, wp, wc, wa, wo, wgu, wd, gf, l, *, tm, final):
    b, t, d = x.shape
    d_ff = wd.shape[1]
    tm = min(tm, t)
    n_parts = 2
    assert t % tm == 0 and d_ff % (n_parts * LANES) == 0
    row = lambda w: pl.BlockSpec((1, tm, w), lambda bi, i: (bi, i, 0))
    wsp = lambda w: pl.BlockSpec((None,) + w.shape[1:], lambda bi, i: (l, 0, 0), pipeline_mode=pl.Buffered(1))
    vec = pl.BlockSpec((None, 1, d), lambda bi, i: (l, 0, 0))
    n_out = 2 if final else 1
    return pl.pallas_call(
        functools.partial(_merge_ffn_kernel, d_ff=d_ff, n_parts=n_parts, final=final),
        grid=(b, t // tm),
        in_specs=[row(d)] + [_mod_spec(mods, col, tm, d) for col in (1, 0, 2, 4, 3, 5)] + [vec, vec,
                  row(a.shape[-1]), row(bo.shape[-1]), row(c.shape[-1]),
                  wsp(wgm), wsp(wp), wsp(wc), wsp(wa), wsp(wo), wsp(wgu), wsp(wd),
                  pl.BlockSpec((1, d), lambda bi, i: (0, 0))],
        out_specs=[row(d)] * n_out,
        out_shape=[jax.ShapeDtypeStruct((b, t, d), F32)] * n_out,
        compiler_params=_params("parallel", "parallel"),
        name="merge_ffn",
    )(x, *[mods] * 6, g1, g2, a, bo, c, wgm, wp, wc, wa, wo, wgu, wd, gf)


def _rope_tables(pos):
    half = ROPE_DIM // 2
    inv = ROPE_THETA ** (-jnp.arange(half, dtype=F32) * 2.0 / ROPE_DIM)
    ang = pos.astype(F32)[:, None] * inv[None, :]
    cos, sin = jnp.cos(ang), jnp.sin(ang)
    n = pos.shape[0]
    rest = HEAD_DIM - ROPE_DIM
    c = jnp.concatenate([cos, cos, jnp.ones((n, rest), F32)], axis=1)
    sa = jnp.concatenate([-sin, jnp.zeros((n, half + rest), F32)], axis=1)
    sb = jnp.concatenate([jnp.zeros((n, half), F32), sin, jnp.zeros((n, rest), F32)], axis=1)
    rep = LANES // HEAD_DIM
    return tuple(jnp.tile(a, (1, rep)) for a in (c, sa, sb))


def _block_bias(n_keys, lane_stride):
    key_blk = (jnp.arange(n_keys) // L_SLC)[:, None] * lane_stride
    return jnp.where(key_blk == jnp.arange(LANES)[None, :], NEG, 0.0).astype(BF16)


def kernel(x_prompt, x_sample, cache_cmp_k, cache_cmp_v, cache_slc_k, cache_slc_v, state_win_k, state_win_v,
           state_pool, state_conv, page_table, c_prompt, c_sample, w_ada, b_ada, g_norm_mix, g_norm_ffn, w_in,
           w_pool, s_pool, w_pool_out, w_dw, b_dw, g_conv_ln, b_conv_ln, w_conv_out, w_cmp_k, w_cmp_v,
           w_attn_out, w_out, w_gu, w_down, g_final):
    depth, d, _ = w_in.shape
    bp, tp, _ = x_prompt.shape
    bd, ts, _ = x_sample.shape
    wp_dim = w_pool_out.shape[1]
    wc_dim = w_conv_out.shape[1]
    n_pool = cache_cmp_k.shape[1]
    n_pages = page_table.shape[1]
    past_len = n_pages * PAGE_SIZE
    wl = state_win_k.shape[2]
    n_groups, gw = w_pool.shape[1], w_pool.shape[2]
    assert gw == POOL_WINDOWS_GW and n_groups == len(POOL_WINDOWS) and wp_dim == n_groups * gw

    n_a = wp_dim + 2 * wc_dim + N_HEADS * HEAD_DIM + 6 * KV_W + 3 * N_HEADS
    n_a_pad = n_a - 3 * N_HEADS + LANES
    w_in_a = jnp.pad(w_in[:, :, :n_a], ((0, 0), (0, 0), (0, n_a_pad - n_a))).astype(BF16)
    w_gm = w_in[:, :, n_a:].astype(BF16)
    eye = jnp.eye(n_groups, dtype=F32)
    wpool_bd = (w_pool[:, :, :, None, :] * eye[None, :, None, :, None]).reshape(depth, wp_dim, wp_dim).astype(BF16)
    wpo, wco, wao, wo = (w.astype(BF16) for w in (w_pool_out, w_conv_out, w_attn_out, w_out))
    wgu, wdn = w_gu.astype(BF16), w_down.astype(BF16)
    wck = jnp.broadcast_to(w_cmp_k[:, :, None], (depth, L_CMP, KV_W))
    wcv = jnp.broadcast_to(w_cmp_v[:, :, None], (depth, L_CMP, KV_W))
    vec3 = lambda a: a.reshape(depth, 1, -1)
    g_mix, g_ffn = vec3(g_norm_mix), vec3(g_norm_ffn)
    spool, bdw, gln, bln = vec3(s_pool), vec3(b_dw), vec3(g_conv_ln), vec3(b_conv_ln)
    gf = g_final.reshape(1, d)

    tabs_p = _rope_tables(jnp.arange(tp))
    tabs_s = _rope_tables(jnp.tile(past_len + jnp.arange(ts), bd))
    eneg_p = _block_bias(tp, 1)
    n_keys_s = past_len + LANES
    eneg_s = _block_bias(n_keys_s, 2).T
    key = jnp.arange(n_keys_s)
    blk_of_key = (key[:, None] // L_CMP == jnp.arange(LANES)[None, :]).astype(F32)
    selk = (w_cmp_k[:, key % L_CMP, None] * blk_of_key[None]).astype(BF16)
    selv = (w_cmp_v[:, key % L_CMP, None] * blk_of_key[None]).astype(BF16)

    ada = _ada_all(jnp.concatenate([c_prompt, c_sample], axis=0), w_ada, b_ada)
    pos_minor = lambda a: a.transpose(0, 1, 3, 4, 2).reshape(a.shape[0] * a.shape[1], KV_W, a.shape[2])
    caches = tuple(pos_minor(c) for c in (cache_cmp_k, cache_cmp_v, cache_slc_k, cache_slc_v))
    win_k_all, win_v_all = pos_minor(state_win_k), pos_minor(state_win_v)
    new_win = [jnp.zeros(win_k_all.shape, F32), jnp.zeros(win_v_all.shape, F32)]

    xp = x_prompt
    xs = x_sample.reshape(1, bd * ts, d)
    st_p, st_s = [], []
    yp = ys = None
    for l in range(depth):
        final = l == depth - 1
        mp = ada[l, :bp][:, None, :]
        ms = jnp.repeat(ada[l, bp:], ts, axis=0)[None]

        (up, uc, q, kc, vc, ks, vs, kw, vw, gn, ck, cv, ks_b, vs_b, kw_b, vw_b) = _inproj(
            xp, mp, g_mix, w_in_a, l, tabs_p, wck, wcv, w_pool=wp_dim, w_conv=wc_dim, with_cmp=True,
            q_dtype=BF16, tm=ROW_TILE)
        a_o, b_o, cst = _mix_prompt(up, uc, wpool_bd, spool, w_dw, bdw, gln, bln, l, tt=ROW_TILE)
        n_cmp = tp // L_CMP
        perm = lambda a: a.reshape(bp, n_cmp // 2, 2, KV_W).transpose(0, 2, 1, 3).reshape(bp, n_cmp, KV_W).astype(BF16)
        c_o = _nsa_prompt(q, gn, perm(ck), perm(cv), ks_b, vs_b, kw_b, vw_b, eneg_p)
        res = _merge_ffn(xp, mp, g_mix, g_ffn, a_o, b_o, c_o, w_gm, wpo, wco, wao, wo, wgu, wdn, gf, l,
                         tm=ROW_TILE, final=final)
        xp = res[0]
        if final:
            yp = res[1]
        kv4 = lambda a: a.reshape(a.shape[0], a.shape[1], N_KV, HEAD_DIM)
        wlp = min(WINDOW, tp)
        st_p.append((kv4(kc), kv4(vc), kv4(ks), kv4(vs), kv4(kw[:, tp - wlp:]), kv4(vw[:, tp - wlp:]),
                     up[:, tp - (max(POOL_WINDOWS) - 1):], cst[:, HALO - (CONV_K - 1):]))

        (up, uc, q, kc, vc, ks, vs, kw, vw, gn) = _inproj(
            xs, ms, g_mix, w_in_a, l, tabs_s, wck, wcv, w_pool=wp_dim, w_conv=wc_dim, with_cmp=False,
            q_dtype=F32, tm=ROW_TILE)
        tmaj = lambda a: a.reshape(bd, ts, a.shape[-1]).transpose(1, 0, 2)
        a_t, b_t, glu_t = _mix_sample(state_pool[l].transpose(1, 0, 2), tmaj(up), state_conv[l].transpose(1, 0, 2),
                                      tmaj(uc), wpool_bd, spool, w_dw, bdw, gln, bln, l)
        bmaj = lambda a: a.transpose(1, 0, 2).reshape(1, bd * ts, a.shape[-1])
        per_seq = lambda a: a.reshape(bd, ts, a.shape[-1])
        c_o, *new_win = _nsa_sample(page_table, per_seq(q), per_seq(gn),
                                    [per_seq(a) for a in (kc, vc, ks, vs, kw, vw)], win_k_all, win_v_all, caches,
                                    selk, selv, eneg_s, l, new_win, n_pool=n_pool, depth=depth)
        res = _merge_ffn(xs, ms, g_mix, g_ffn, bmaj(a_t), bmaj(b_t), c_o.reshape(1, bd * ts, -1), w_gm, wpo, wco, wao,
                         wo, wgu, wdn, gf, l, tm=ROW_TILE, final=final)
        xs = res[0]
        if final:
            ys = res[1]
        kv4s = lambda a: a.reshape(bd, -1, N_KV, HEAD_DIM)
        new_pool = jnp.concatenate([state_pool[l][:, ts:], per_seq(up)], axis=1)
        new_conv = jnp.concatenate([state_conv[l][:, ts:], glu_t.transpose(1, 0, 2)], axis=1)
        st_s.append((kv4s(kc), kv4s(vc), kv4s(ks), kv4s(vs), None, None, new_pool, new_conv))

    win_out = [w.reshape(depth, bd, N_KV, HEAD_DIM, wl).transpose(0, 1, 4, 2, 3) for w in new_win]
    outs = [yp, ys.reshape(bd, ts, d)]
    for i in range(8):
        outs.append(jnp.stack([s[i] for s in st_p]))
        outs.append(win_out[i - 4] if i in (4, 5) else jnp.stack([s[i] for s in st_s]))
    return tuple(outs)
```

```python
import functools

import jax
import jax.numpy as jnp
from jax import lax
from jax.experimental import pallas as pl
from jax.experimental.pallas import tpu as pltpu

F32 = jnp.float32
BF16 = jnp.bfloat16

HEAD_DIM = 64
N_KV = 2
GQA = 4
N_HEADS = N_KV * GQA
ROPE_DIM = 16
ROPE_THETA = 500000.0
L_CMP = 32
L_SLC = 64
L_SLC_SHIFT = 6
N_SEL = 16
WINDOW = 512
Q_BLOCK = 128
PAGE_SIZE = 128
POOL_WINDOWS = (2, 4, 8, 16)
CONV_K = 31
EPS = 1e-6
NEG = -1e30
FORCE_SCORE = 1e4

LANES = 128
SUBLANES = 8
VMEM_LIMIT_BYTES = 56 * 1024 * 1024

NSA_CHUNK_KEYS = 512
NSA_PROMPT_BLOCKS = 2
NSA_SAMPLE_SEQS = 4
NSA_ROW_TILE = 32
ROW_TILE = 512
ADA_COL_TILE = 1536
POOL_WINDOWS_GW = 64
HALO = 32
KV_W = N_KV * HEAD_DIM


def _params(*sem):
    return pltpu.CompilerParams(dimension_semantics=sem, vmem_limit_bytes=VMEM_LIMIT_BYTES)


def _modnorm(x, g, sc, sh):
    ms = jnp.mean(x * x, axis=-1, keepdims=True)
    return (x * lax.rsqrt(ms + EPS) * g) * (1.0 + sc) + sh


def _sigmoid(x):
    return 1.0 / (1.0 + jnp.exp(-x))


def _dot_nt(a, b):
    return lax.dot_general(a, b, (((1,), (1,)), ((), ())), preferred_element_type=F32)


def _masked_softmax(s, mask):
    sm = jnp.where(mask, s, NEG)
    e = jnp.exp(sm - jnp.max(sm, axis=-1, keepdims=True))
    return jnp.where(mask, e / jnp.sum(e, axis=-1, keepdims=True), 0.0)


def _ada_kernel(c_ref, w_ref, b_ref, o_ref):
    c = c_ref[...]
    a = (c * _sigmoid(c)).astype(BF16)
    o_ref[...] = jnp.dot(a, w_ref[...].astype(BF16), preferred_element_type=F32) + b_ref[...]


def _ada_all(c_all, w_ada, b_ada):
    depth, d, n = w_ada.shape
    tn = ADA_COL_TILE
    assert n % tn == 0
    nb = c_all.shape[0]
    return pl.pallas_call(
        _ada_kernel,
        grid=(depth, n // tn),
        in_specs=[
            pl.BlockSpec((nb, d), lambda l, j: (0, 0)),
            pl.BlockSpec((None, d, tn), lambda l, j: (l, 0, j)),
            pl.BlockSpec((None, 1, tn), lambda l, j: (l, 0, j)),
        ],
        out_specs=pl.BlockSpec((None, nb, tn), lambda l, j: (l, 0, j)),
        out_shape=jax.ShapeDtypeStruct((depth, nb, n), F32),
        compiler_params=_params("parallel", "parallel"),
        name="ada_mod",
    )(c_all, w_ada, b_ada.reshape(depth, 1, n))


def _inproj_kernel(x_ref, sc_ref, sh_ref, g_ref, w_ref, cos_ref, sa_ref, sb_ref, wck_ref, wcv_ref, *outs,
                   w_pool, w_conv, with_cmp):
    (up_ref, uc_ref, q_ref, kc_ref, vc_ref, ks_ref, vs_ref, kw_ref, vw_ref, gn_ref) = outs[:10]
    h = _modnorm(x_ref[0], g_ref[...], sc_ref[0], sh_ref[0]).astype(BF16)
    y = jnp.dot(h, w_ref[...], preferred_element_type=F32)
    cos, sa, sb = cos_ref[...], sa_ref[...], sb_ref[...]

    def rope(t):
        return t * cos + pltpu.roll(t, LANES - ROPE_DIM // 2, 1) * sa + pltpu.roll(t, ROPE_DIM // 2, 1) * sb

    o = 0
    up_ref[0] = y[:, o:o + w_pool]
    o += w_pool
    uc_ref[0] = y[:, o:o + 2 * w_conv]
    o += 2 * w_conv
    for j in range(N_HEADS * HEAD_DIM // LANES):
        q_ref[0, :, j * LANES:(j + 1) * LANES] = rope(y[:, o:o + LANES]).astype(q_ref.dtype)
        o += LANES
    kc = rope(y[:, o:o + KV_W])
    vc = y[:, o + KV_W:o + 2 * KV_W]
    ks = rope(y[:, o + 2 * KV_W:o + 3 * KV_W])
    vs = y[:, o + 3 * KV_W:o + 4 * KV_W]
    kw = rope(y[:, o + 4 * KV_W:o + 5 * KV_W])
    vw = y[:, o + 5 * KV_W:o + 6 * KV_W]
    o += 6 * KV_W
    kc_ref[0], vc_ref[0], ks_ref[0], vs_ref[0], kw_ref[0], vw_ref[0] = kc, vc, ks, vs, kw, vw
    gn_ref[0] = y[:, o:o + LANES]
    if with_cmp:
        ck_ref, cv_ref = outs[10:12]
        tm = kc.shape[0]
        ck_ref[0] = jnp.sum(kc.reshape(tm // L_CMP, L_CMP, KV_W) * wck_ref[...][None], axis=1)
        cv_ref[0] = jnp.sum(vc.reshape(tm // L_CMP, L_CMP, KV_W) * wcv_ref[...][None], axis=1)
        for ref, val in zip(outs[12:], (ks, vs, kw, vw)):
            ref[0] = val.astype(BF16)


def _mod_spec(mods, col, tm, d, layer=None):
    if layer is not None:
        return pl.BlockSpec((1, tm, d), lambda bi, i: (layer, i, col))
    assert mods.shape[1] == 1
    return pl.BlockSpec((1, 1, d), lambda bi, i: (bi, 0, col))


def _inproj(x, mods, g, w_a, l, tabs, wck, wcv, *, w_pool, w_conv, with_cmp, q_dtype, tm, mod_layer=None):
    b, t, d = x.shape
    n = w_a.shape[-1]
    tm = min(tm, t)
    assert t % tm == 0 and (tm % L_CMP == 0 or not with_cmp)
    tab_spec = pl.BlockSpec((tm, LANES), lambda bi, i: (i, 0))
    row = lambda w: pl.BlockSpec((1, tm, w), lambda bi, i: (bi, i, 0))
    widths = [w_pool, 2 * w_conv, N_HEADS * HEAD_DIM] + [KV_W] * 6 + [LANES]
    dtypes = [F32, F32, q_dtype] + [F32] * 7
    out_specs = [row(w) for w in widths]
    out_shape = [jax.ShapeDtypeStruct((b, t, w), dt) for w, dt in zip(widths, dtypes)]
    if with_cmp:
        out_specs += [pl.BlockSpec((1, tm // L_CMP, KV_W), lambda bi, i: (bi, i, 0))] * 2 + [row(KV_W)] * 4
        out_shape += ([jax.ShapeDtypeStruct((b, t // L_CMP, KV_W), F32)] * 2
                      + [jax.ShapeDtypeStruct((b, t, KV_W), BF16)] * 4)
    return pl.pallas_call(
        functools.partial(_inproj_kernel, w_pool=w_pool, w_conv=w_conv, with_cmp=with_cmp),
        grid=(b, t // tm),
        in_specs=[
            row(d), _mod_spec(mods, 1, tm, d, mod_layer), _mod_spec(mods, 0, tm, d, mod_layer),
            pl.BlockSpec((None, 1, d), lambda bi, i: (l, 0, 0)),
            pl.BlockSpec((None, d, n), lambda bi, i: (l, 0, 0)),
            tab_spec, tab_spec, tab_spec,
            pl.BlockSpec((None, L_CMP, KV_W), lambda bi, i: (l, 0, 0)),
            pl.BlockSpec((None, L_CMP, KV_W), lambda bi, i: (l, 0, 0)),
        ],
        out_specs=out_specs,
        out_shape=out_shape,
        compiler_params=_params("parallel", "parallel"),
        name="in_proj",
    )(x, mods, mods, g, w_a, *tabs, wck, wcv)


def _pool_means(ext_ref, lvl_ref, base, rows, pos0):
    assert POOL_WINDOWS == (2, 4, 8, 16) and base == 4 * SUBLANES
    n = base + rows
    lane = lax.broadcasted_iota(jnp.int32, (rows, LANES), 1)
    pos = lax.broadcasted_iota(jnp.int32, (rows, LANES), 0) + pos0
    low = lane < POOL_WINDOWS_GW
    means = []
    for tile in range(len(POOL_WINDOWS) // 2):
        cols = slice(tile * LANES, (tile + 1) * LANES)
        w_small, w_big = POOL_WINDOWS[2 * tile], POOL_WINDOWS[2 * tile + 1]
        sums = {}
        src, w, lo = None, 1, 0
        while w < w_big:
            lo += SUBLANES
            if src is None:
                cur = ext_ref[lo:n, cols] + ext_ref[lo - w:n - w, cols]
            else:
                cur = lvl_ref[src, lo:n] + lvl_ref[src, lo - w:n - w]
            w *= 2
            sums[w] = cur[base - lo:]
            if w < w_big:
                src = 0 if src != 0 else 1
                lvl_ref[src, lo:n] = cur
        cnt = jnp.minimum(jnp.where(low, w_small, w_big), pos + 1).astype(F32)
        means.append(jnp.where(low, sums[w_small], sums[w_big]) / cnt)
    return jnp.concatenate(means, axis=1)


def _layernorm_silu(y, g, b):
    mu = jnp.mean(y, axis=-1, keepdims=True)
    yc = y - mu
    var = jnp.mean(yc * yc, axis=-1, keepdims=True)
    z = yc * lax.rsqrt(var + EPS) * g + b
    return z * _sigmoid(z)


def _mix_prompt_kernel(up_ref, uph_ref, uc_ref, uch_ref, wpool_ref, spool_ref, wdw_ref, bdw_ref, gln_ref, bln_ref,
                       a_ref, b_ref, cst_ref, pext_ref, cext_ref, part_ref, lvl_ref, *, tt, w_conv):
    i = pl.program_id(1)
    first = i == 0
    u = up_ref[0]
    pext_ref[0:HALO] = jnp.where(first, 0.0, uph_ref[0])
    pext_ref[HALO:HALO + tt] = u
    d = _pool_means(pext_ref, lvl_ref, HALO, tt, i * tt) - u
    z = jnp.dot(d.astype(BF16), wpool_ref[...], preferred_element_type=F32) * spool_ref[...]
    a_ref[0] = z.astype(a_ref.dtype)
    hc = uch_ref[0]
    cext_ref[0:HALO] = jnp.where(first, 0.0, hc[:, :w_conv] * _sigmoid(hc[:, w_conv:]))
    uc = uc_ref[0]
    cext_ref[HALO:HALO + tt] = uc[:, :w_conv] * _sigmoid(uc[:, w_conv:])
    off = HALO - (CONV_K - 1)
    acc = None
    for m in range(SUBLANES):
        n_rows = tt + SUBLANES * (-(-(off + m) // SUBLANES))
        part = None
        for k in range(m, CONV_K, SUBLANES):
            assert k - m + n_rows <= HALO + tt
            term = wdw_ref[k:k + 1, :] * cext_ref[k - m:k - m + n_rows]
            part = term if part is None else part + term
        part_ref[m, 0:n_rows] = part
        shifted = part_ref[m, off + m:off + m + tt]
        acc = shifted if acc is None else acc + shifted
    b_ref[0] = _layernorm_silu(acc + bdw_ref[...], gln_ref[...], bln_ref[...]).astype(b_ref.dtype)
    cst_ref[0] = cext_ref[tt:tt + HALO]


def _mix_prompt(up, uc, wpool_bd, spool, wdw, bdw, gln, bln, l, *, tt):
    b, t, w_pool = up.shape
    w_conv = uc.shape[-1] // 2
    tt = min(tt, t)
    assert t % tt == 0 and tt % HALO == 0
    r = tt // HALO
    halo = lambda w: pl.BlockSpec((1, HALO, w), lambda bi, i: (bi, jnp.maximum(i * r - 1, 0), 0))
    vec = lambda w: pl.BlockSpec((None, 1, w), lambda bi, i: (l, 0, 0))
    return pl.pallas_call(
        functools.partial(_mix_prompt_kernel, tt=tt, w_conv=w_conv),
        grid=(b, t // tt),
        in_specs=[
            pl.BlockSpec((1, tt, w_pool), lambda bi, i: (bi, i, 0)), halo(w_pool),
            pl.BlockSpec((1, tt, 2 * w_conv), lambda bi, i: (bi, i, 0)), halo(2 * w_conv),
            pl.BlockSpec((None, w_pool, w_pool), lambda bi, i: (l, 0, 0)), vec(w_pool),
            pl.BlockSpec((None, CONV_K, w_conv), lambda bi, i: (l, 0, 0)), vec(w_conv), vec(w_conv), vec(w_conv),
        ],
        out_specs=[
            pl.BlockSpec((1, tt, w_pool), lambda bi, i: (bi, i, 0)),
            pl.BlockSpec((1, tt, w_conv), lambda bi, i: (bi, i, 0)),
            pl.BlockSpec((1, HALO, w_conv), lambda bi, i: (bi, 0, 0)),
        ],
        out_shape=[
            jax.ShapeDtypeStruct((b, t, w_pool), BF16),
            jax.ShapeDtypeStruct((b, t, w_conv), BF16),
            jax.ShapeDtypeStruct((b, HALO, w_conv), F32),
        ],
        scratch_shapes=[pltpu.VMEM((HALO + tt, w_pool), F32), pltpu.VMEM((HALO + tt, w_conv), F32),
                        pltpu.VMEM((SUBLANES, tt + 2 * SUBLANES, w_conv), F32),
                        pltpu.VMEM((2, HALO + tt, LANES), F32)],
        compiler_params=_params("parallel", "arbitrary"),
        name="mix_prompt",
    )(up, up, uc, uc, wpool_bd, spool, wdw, bdw, gln, bln)


def _mix_sample_kernel(pst_ref, up_ref, cst_ref, uc_ref, wpool_ref, spool_ref, wdw_ref, bdw_ref, gln_ref, bln_ref,
                       a_ref, b_ref, glu_ref, *, w_conv):
    n_p, n_c, ts = pst_ref.shape[0], cst_ref.shape[0], up_ref.shape[0]
    prow = lambda r: pst_ref[r] if r < n_p else up_ref[r - n_p]
    low = lax.broadcasted_iota(jnp.int32, (up_ref.shape[1], LANES), 1) < POOL_WINDOWS_GW
    glu = []
    for t in range(ts):
        uc = uc_ref[t]
        glu.append(uc[:, :w_conv] * _sigmoid(uc[:, w_conv:]))
        glu_ref[t] = glu[t]
    crow = lambda r: cst_ref[r] if r < n_c else glu[r - n_c]
    for t in range(ts):
        u = up_ref[t]
        means = None
        acc = u
        sums = {1: acc}
        for s in range(1, max(POOL_WINDOWS)):
            acc = acc + prow(n_p + t - s)
            sums[s + 1] = acc
        tiles = []
        for tile in range(len(POOL_WINDOWS) // 2):
            cols = slice(tile * LANES, (tile + 1) * LANES)
            w_small, w_big = POOL_WINDOWS[2 * tile], POOL_WINDOWS[2 * tile + 1]
            tiles.append(jnp.where(low, sums[w_small][:, cols] / float(w_small),
                                   sums[w_big][:, cols] / float(w_big)))
        means = jnp.concatenate(tiles, axis=1)
        z = jnp.dot((means - u).astype(BF16), wpool_ref[...], preferred_element_type=F32) * spool_ref[...]
        a_ref[t] = z.astype(a_ref.dtype)
        acc = wdw_ref[0:1, :] * crow(t + n_c - (CONV_K - 1))
        for k in range(1, CONV_K):
            acc = acc + wdw_ref[k:k + 1, :] * crow(t + n_c - (CONV_K - 1) + k)
        b_ref[t] = _layernorm_silu(acc + bdw_ref[...], gln_ref[...], bln_ref[...]).astype(b_ref.dtype)


def _mix_sample(pst_t, up_t, cst_t, uc_t, wpool_bd, spool, wdw, bdw, gln, bln, l):
    ts, bd, w_pool = up_t.shape
    w_conv = uc_t.shape[-1] // 2
    full = lambda a: pl.BlockSpec(a.shape, lambda i: (0,) * a.ndim)
    vec = lambda w: pl.BlockSpec((None, 1, w), lambda i: (l, 0, 0))
    return pl.pallas_call(
        functools.partial(_mix_sample_kernel, w_conv=w_conv),
        grid=(1,),
        in_specs=[
            full(pst_t), full(up_t), full(cst_t), full(uc_t),
            pl.BlockSpec((None, w_pool, w_pool), lambda i: (l, 0, 0)), vec(w_pool),
            pl.BlockSpec((None, CONV_K, w_conv), lambda i: (l, 0, 0)), vec(w_conv), vec(w_conv), vec(w_conv),
        ],
        out_specs=[pl.BlockSpec((ts, bd, w_pool), lambda i: (0, 0, 0)),
                   pl.BlockSpec((ts, bd, w_conv), lambda i: (0, 0, 0)),
                   pl.BlockSpec((ts, bd, w_conv), lambda i: (0, 0, 0))],
        out_shape=[jax.ShapeDtypeStruct((ts, bd, w_pool), BF16),
                   jax.ShapeDtypeStruct((ts, bd, w_conv), BF16),
                   jax.ShapeDtypeStruct((ts, bd, w_conv), F32)],
        compiler_params=_params("arbitrary"),
        name="mix_sample",
    )(pst_t, up_t, cst_t, uc_t, wpool_bd, spool, wdw, bdw, gln, bln)


def _topk_masks(scores, n_sel):
    sts = tuple(s.T for s in scores)
    cand = lax.broadcasted_iota(jnp.int32, sts[0].shape, 0).astype(F32)

    def body(_, carry):
        out = []
        for st, sel in carry:
            m = jnp.max(st, axis=0, keepdims=True)
            first = jnp.min(jnp.where(st == m, cand, float(LANES)), axis=0, keepdims=True)
            hit = cand == first
            out.append((jnp.where(hit, -jnp.inf, st), jnp.where(hit, 1.0, sel)))
        return tuple(out)

    res = lax.fori_loop(0, n_sel, body, tuple((st, jnp.zeros(st.shape, F32)) for st in sts))
    return [sel.T for _, sel in res]


def _head_rows(tile, src_half, dst_half, lane):
    if src_half != dst_half:
        tile = pltpu.roll(tile, HEAD_DIM, 1)
    return jnp.where((lane >= dst_half * HEAD_DIM) & (lane < (dst_half + 1) * HEAD_DIM), tile, 0.0)


def _nsa_prompt_multi_kernel(q_ref, gn_ref, ck_ref, cv_ref, ks_ref, vs_ref, kw_ref, vw_ref, eneg_ref, o_ref,
                             qa_scr, s_scr, p_scr, sw_scr, pw_scr, *, ck_keys, n_sub):
    step = pl.program_id(1)
    qb = Q_BLOCK
    rows = GQA * qb
    scale = HEAD_DIM ** -0.5
    n_cmp = ck_ref.shape[1]
    n_slc = n_cmp // 2
    n_wchunks = WINDOW // qb + 1
    wk = n_wchunks * qb
    lane = lax.broadcasted_iota(jnp.int32, (qb, LANES), 1)
    chains = [(u, k) for u in range(n_sub) for k in range(N_KV)]

    def qblk(u):
        return step * n_sub + u

    def qpos(u, shape, row0=0):
        return qblk(u) * qb + ((row0 + lax.broadcasted_iota(jnp.int32, shape, 0)) & (qb - 1))

    def qrows(u, ref, lo, hi):
        return ref[0, u * qb:(u + 1) * qb, lo:hi]

    st = {}

    def phase1(u):
        cur = qpos(u, lane.shape) >> L_SLC_SHIFT
        valid_s = lane <= cur
        forced = (lane == 0) | (lane == cur) | (lane == cur - 1)
        st[u, "valid"] = valid_s
        for k in range(N_KV):
            qh = []
            for g in range(GQA):
                h = k * GQA + g
                tile = qrows(u, q_ref, (h // 2) * LANES, (h // 2 + 1) * LANES).astype(F32) * scale
                qh.append(_head_rows(tile, h % 2, k, lane))
            qk = jnp.concatenate(qh, axis=0).astype(BF16)
            st[u, k, "qk"] = qk
            s_c = _dot_nt(qk, ck_ref[0])
            yield
            cl = lax.broadcasted_iota(jnp.int32, (rows, n_cmp), 1)
            c_end = jnp.where(cl < n_slc, cl * L_SLC + L_CMP - 1, (cl - n_slc) * L_SLC + L_SLC - 1)
            p_c = _masked_softmax(s_c, c_end <= qpos(u, s_c.shape))
            yield
            st[u, k, "o_c"] = jnp.dot(p_c.astype(BF16), cv_ref[0], preferred_element_type=F32)
            ps = p_c[0:qb]
            for g in range(1, GQA):
                ps = ps + p_c[g * qb:(g + 1) * qb]
            imp = ps[:, :n_slc] + ps[:, n_slc:]
            if n_slc < LANES:
                imp = jnp.concatenate([imp, jnp.zeros((qb, LANES - n_slc), F32)], axis=1)
            score = jnp.where(valid_s, jnp.where(forced, FORCE_SCORE, imp), -1.0)
            st[u, k, "score"] = jnp.where(lane < n_slc, score, -jnp.inf)
            yield

    _lockstep(phase1(u) for u in range(n_sub))

    for u in range(n_sub):
        for k, sel in enumerate(_topk_masks([st[u, k, "score"] for k in range(N_KV)], min(N_SEL, n_slc))):
            notsel = (1.0 - jnp.where(st[u, "valid"], sel, 0.0)).astype(BF16)
            qa_scr[u * N_KV + k] = jnp.concatenate([st[u, k, "qk"], jnp.concatenate([notsel] * GQA, axis=0)], axis=1)

    own = lax.broadcasted_iota(jnp.int32, (ck_keys, LANES), 1) < HEAD_DIM

    def scores(c, buf):
        start = pl.multiple_of(c * ck_keys, ck_keys)
        k_aug = jnp.concatenate([ks_ref[0, pl.ds(start, ck_keys), :], eneg_ref[pl.ds(start, ck_keys), :]], axis=1)
        nc = len(chains)
        s_all = _dot_nt(qa_scr[...].reshape(nc * rows, 2 * LANES), k_aug)
        s_scr[buf] = s_all.reshape(nc, rows, ck_keys)

    def probs(c, buf, ms, causal):
        new_m, alphas = [], []
        for n, (u, k) in enumerate(chains):
            m_tiles = []
            for r in range(0, rows, NSA_ROW_TILE):
                rs = slice(r, r + NSA_ROW_TILE)
                s = s_scr[buf, n, rs]
                if causal:
                    kpos = c * ck_keys + lax.broadcasted_iota(jnp.int32, s.shape, 1)
                    s = jnp.where(kpos <= qpos(u, s.shape, r), s, NEG)
                m_new = jnp.maximum(ms[n][rs], jnp.max(s, axis=-1, keepdims=True))
                p_scr[buf, n, rs] = jnp.exp(s - m_new).astype(BF16)
                m_tiles.append(m_new)
            m_new = jnp.concatenate(m_tiles, axis=0)
            alphas.append(jnp.exp(ms[n] - m_new))
            new_m.append(m_new)
        return new_m, alphas

    def values(c, buf, alphas, accs):
        start = pl.multiple_of(c * ck_keys, ck_keys)
        v = vs_ref[0, pl.ds(start, ck_keys), :]
        v_one = [jnp.where(own, v, 1.0), jnp.where(own, 1.0, v)]
        return [alphas[n] * accs[n] + jnp.dot(p_scr[buf, n], v_one[k], preferred_element_type=F32)
                for n, (u, k) in enumerate(chains)]

    def pair(c, carry, causal, last):
        ms, alphas, accs = carry
        accs = values(jnp.maximum(c - 1, 0), 1, alphas, accs)
        ms, alphas = probs(c, 0, ms, causal)
        scores(c + 1, 1)
        accs = values(c, 0, alphas, accs)
        ms, alphas = probs(c + 1, 1, ms, causal)
        if last:
            accs = values(c + 1, 1, alphas, accs)
        else:
            scores(c + 2, 0)
        return ms, alphas, accs

    n_pairs = (qblk(n_sub - 1) * qb + qb + 2 * ck_keys - 1) // (2 * ck_keys)
    p_scr[1] = jnp.zeros(p_scr.shape[1:], BF16)
    scores(0, 0)
    nc = len(chains)
    carry = ([jnp.full((rows, 1), -jnp.inf, F32)] * nc, [jnp.ones((rows, 1), F32)] * nc,
             [jnp.zeros((rows, LANES), F32)] * nc)
    carry = lax.fori_loop(0, n_pairs - 1, lambda j, cr: pair(2 * j, cr, False, False), carry)
    c_last = 2 * (n_pairs - 1)

    def last_single(cr):
        ms, alphas, accs = cr
        accs = values(jnp.maximum(c_last - 1, 0), 1, alphas, accs)
        ms, alphas = probs(c_last, 0, ms, True)
        return values(c_last, 0, alphas, accs)

    n_chunks = (qblk(n_sub - 1) * qb + qb + ck_keys - 1) // ck_keys
    accs = lax.cond(n_chunks == c_last + 1, last_single, lambda cr: pair(c_last, cr, True, True)[2], carry)
    o_ss = [acc / pltpu.roll(acc, HEAD_DIM, 1) for acc in accs]

    def tail(u):
        i = qblk(u)
        kws, vws = [], []
        for dj in range(n_wchunks):
            st0 = pl.multiple_of(jnp.maximum(i - (n_wchunks - 1) + dj, 0) * qb, qb)
            kws.append(kw_ref[0, pl.ds(st0, qb), :])
            vws.append(vw_ref[0, pl.ds(st0, qb), :])
        kw_blk, vw_blk = jnp.concatenate(kws, axis=0), jnp.concatenate(vws, axis=0)
        kwpos = (i - (n_wchunks - 1)) * qb + lax.broadcasted_iota(jnp.int32, (qb, wk), 1)
        qp = qpos(u, (qb, wk))
        wbias = jnp.where((kwpos <= qp) & (kwpos > qp - WINDOW) & (kwpos >= 0), 0.0, NEG)
        own_w = lax.broadcasted_iota(jnp.int32, (wk, LANES), 1) < HEAD_DIM
        vw_one = [jnp.where(own_w, vw_blk, 1.0), jnp.where(own_w, 1.0, vw_blk)]
        for k in range(N_KV):
            sw_scr[u * N_KV + k] = _dot_nt(st[u, k, "qk"], kw_blk)
        yield
        o_ws = []
        for k in range(N_KV):
            n = u * N_KV + k
            for r in range(0, rows, NSA_ROW_TILE):
                rs = slice(r, r + NSA_ROW_TILE)
                s = sw_scr[n, rs] + wbias[r % qb:r % qb + NSA_ROW_TILE]
                pw_scr[n, rs] = jnp.exp(s - jnp.max(s, axis=-1, keepdims=True)).astype(BF16)
            yield
            acc = jnp.dot(pw_scr[n], vw_one[k], preferred_element_type=F32)
            o_ws.append(acc / pltpu.roll(acc, HEAD_DIM, 1))
            yield
        sig = _sigmoid(qrows(u, gn_ref, 0, LANES))
        out_tiles = [None] * (N_HEADS // 2)
        for k in range(N_KV):
            o_c, o_s, o_w = st[u, k, "o_c"], o_ss[u * N_KV + k], o_ws[k]
            for g in range(GQA):
                h = k * GQA + g
                col = 3 * h
                r = slice(g * qb, (g + 1) * qb)
                o = (sig[:, col:col + 1] * o_c[r] + sig[:, col + 1:col + 2] * o_s[r]
                     + sig[:, col + 2:col + 3] * o_w[r])
                o = _head_rows(o, k, h % 2, lane)
                out_tiles[h // 2] = o if out_tiles[h // 2] is None else out_tiles[h // 2] + o
            yield
        for j, tile in enumerate(out_tiles):
            o_ref[0, u * qb:(u + 1) * qb, j * LANES:(j + 1) * LANES] = tile.astype(o_ref.dtype)

    _lockstep(tail(u) for u in range(n_sub))


def _nsa_prompt(q, gn, ckp, cvp, ks, vs, kw, vw, eneg):
    b, t, _ = q.shape
    assert t % Q_BLOCK == 0 and t // L_SLC <= LANES
    ck_keys = min(NSA_CHUNK_KEYS, t // 2)
    assert t % (2 * ck_keys) == 0
    rows = GQA * Q_BLOCK
    n_sub = NSA_PROMPT_BLOCKS
    assert (2 * ck_keys) % (n_sub * Q_BLOCK) == 0 and t % (n_sub * Q_BLOCK) == 0
    nc = n_sub * N_KV
    qspec = lambda w: pl.BlockSpec((1, n_sub * Q_BLOCK, w), lambda bi, i: (bi, i, 0))
    seq = lambda n: pl.BlockSpec((1, n, KV_W), lambda bi, i: (bi, 0, 0))
    return pl.pallas_call(
        functools.partial(_nsa_prompt_multi_kernel, ck_keys=ck_keys, n_sub=n_sub),
        scratch_shapes=[pltpu.VMEM((nc, rows, 2 * LANES), BF16), pltpu.VMEM((2, nc, rows, ck_keys), F32),
                        pltpu.VMEM((2, nc, rows, ck_keys), BF16),
                        pltpu.VMEM((nc, rows, WINDOW + Q_BLOCK), F32),
                        pltpu.VMEM((nc, rows, WINDOW + Q_BLOCK), BF16)],
        grid=(b, t // (n_sub * Q_BLOCK)),
        in_specs=[qspec(N_HEADS * HEAD_DIM), qspec(LANES), seq(t // L_CMP), seq(t // L_CMP),
                  seq(t), seq(t), seq(t), seq(t), pl.BlockSpec((t, LANES), lambda bi, i: (0, 0))],
        out_specs=qspec(N_HEADS * HEAD_DIM),
        out_shape=jax.ShapeDtypeStruct((b, t, N_HEADS * HEAD_DIM), BF16),
        compiler_params=_params("parallel", "parallel"),
        name="nsa_prompt",
    )(q, gn, ckp, cvp, ks, vs, kw, vw, eneg)


def _nsa_sample_kernel(pt_ref, q_ref, gn_ref, kcn_ref, vcn_ref, ksn_ref, vsn_ref, kwn_ref, vwn_ref, wk_ref, wv_ref,
                       selk_ref, selv_ref, eneg_ref, *rest, n_pages, past_len, ts, page0):
    caches = rest[2:6]
    o_ref, nwk_ref, nwv_ref = rest[6:9]
    (past_buf, sems, tail_s, q_s, sc_s, g_s, o_s) = rest[9:]
    n_seq = q_ref.shape[0]
    b = pl.program_id(0)
    slot = b & 1
    scale = HEAD_DIM ** -0.5
    wl = wk_ref.shape[2]
    n_keys = past_len + LANES
    n_slc = -(-(past_len + ts) // L_SLC)
    rows = GQA * N_KV * ts
    grp = N_KV * ts
    lane_t = lax.broadcasted_iota(jnp.int32, (ts, LANES), 1)

    def page_copy(step, dst_slot, u, j, p):
        src = caches[j].at[page0 + pt_ref[(step * n_seq + u) * n_pages + p]]
        dst = past_buf.at[dst_slot, u, j, :, pl.ds(p * PAGE_SIZE, PAGE_SIZE)]
        return pltpu.make_async_copy(src, dst, sems.at[dst_slot])

    def all_pages(step, dst_slot, op):
        for u in range(n_seq):
            for j in range(4):
                for p in range(n_pages):
                    op(page_copy(step, dst_slot, u, j, p))

    @pl.when(b == 0)
    def _():
        all_pages(0, 0, lambda cp: cp.start())

    @pl.when(b + 1 < pl.num_programs(0))
    def _():
        all_pages(b + 1, 1 - slot, lambda cp: cp.start())

    all_pages(b, slot, lambda cp: cp.wait())

    def qpos(shape):
        return past_len + (lax.broadcasted_iota(jnp.int32, shape, 0) & (ts - 1))

    lane_g = lax.broadcasted_iota(jnp.int32, (grp, LANES), 1)
    blk = lane_g >> 1
    is_blk = ((lane_g & 1) == 0) & (blk < n_slc)
    cur = qpos(lane_g.shape) >> L_SLC_SHIFT
    valid_s = blk <= cur
    forced = (blk == 0) | (blk == cur) | (blk == cur - 1)

    def front(u, out):
        def new_cols(j, new_ref):
            tail_s[u, j] = jnp.zeros(tail_s.shape[2:], F32)
            tail_s[u, j, 0:ts] = new_ref[u]
            return tail_s[u, j].T

        kc_p, vc_p, ks_p, vs_p = (past_buf[slot, u, j].astype(BF16) for j in range(4))
        kc_n, vc_n, ks_n, vs_n = (new_cols(j, r).astype(BF16)
                                  for j, r in enumerate((kcn_ref, vcn_ref, ksn_ref, vsn_ref)))

        kw_all = jnp.concatenate([wk_ref[u], new_cols(4, kwn_ref)], axis=1)
        vw_all = jnp.concatenate([wv_ref[u], new_cols(5, vwn_ref)], axis=1)
        nwk_ref[u] = kw_all[:, ts:ts + wl]
        nwv_ref[u] = vw_all[:, ts:ts + wl]
        yield

        for g in range(GQA):
            for k in range(N_KV):
                h = k * GQA + g
                tile = q_ref[u, :, (h // 2) * LANES:(h // 2 + 1) * LANES] * scale
                q_s[u, (g * N_KV + k) * ts:(g * N_KV + k + 1) * ts] = _head_rows(tile, h % 2, k, lane_t)
        qa = q_s[u].astype(BF16)

        def compress(past, new, sel_ref):
            return (jnp.dot(past, sel_ref[0:past_len], preferred_element_type=F32)
                    + jnp.dot(new, sel_ref[past_len:n_keys], preferred_element_type=F32)).astype(BF16)

        ck_t = compress(kc_p, kc_n, selk_ref)
        yield
        cv_t = compress(vc_p, vc_n, selv_ref)
        yield
        s_c = jnp.dot(qa, ck_t, preferred_element_type=F32)
        yield
        cl = lax.broadcasted_iota(jnp.int32, s_c.shape, 1)
        p_c = _masked_softmax(s_c, (cl + 1) * L_CMP - 1 <= qpos(s_c.shape))
        yield
        o_c = _dot_nt(p_c.astype(BF16), cv_t)
        ps = p_c[0:grp]
        for g in range(1, GQA):
            ps = ps + p_c[g * grp:(g + 1) * grp]
        imp = ps + pltpu.roll(ps, LANES - 1, 1)

        score = jnp.where(valid_s, jnp.where(forced, FORCE_SCORE, imp), -1.0)
        sc_s[u * grp:(u + 1) * grp] = jnp.where(is_blk, score, -jnp.inf)
        out[u] = (qa, o_c, ks_p, ks_n, vs_p, vs_n, kw_all, vw_all)

    sc_s[...] = jnp.zeros(sc_s.shape, F32)
    fronts = {}
    _lockstep(front(u, fronts) for u in range(n_seq))
    sel = _topk_masks([sc_s[...]], min(N_SEL, n_slc))[0]
    _lockstep(_nsa_sample_back(u, fronts[u], sel[u * grp:(u + 1) * grp], is_blk & valid_s, qpos, gn_ref, eneg_ref,
                               o_ref, g_s, o_s, past_len=past_len, ts=ts, wl=wl) for u in range(n_seq))


def _lockstep(stages):
    stages = list(stages)
    while stages:
        for gen in list(stages):
            try:
                next(gen)
            except StopIteration:
                stages.remove(gen)


def _nsa_sample_back(u, front, sel, selectable, qpos, gn_ref, eneg_ref, o_ref, g_s, o_s, *, past_len, ts, wl):
    qa, o_c, ks_p, ks_n, vs_p, vs_n, kw_all, vw_all = front
    n_keys = past_len + LANES
    lane_t = lax.broadcasted_iota(jnp.int32, (ts, LANES), 1)
    notsel = (1.0 - jnp.where(selectable, sel, 0.0)).astype(BF16)
    q_aug = jnp.concatenate([qa, jnp.concatenate([notsel] * GQA, axis=0)], axis=1)

    s_s = jnp.concatenate(
        [jnp.dot(q_aug, jnp.concatenate([ks_p, eneg_ref[:, 0:past_len]], axis=0), preferred_element_type=F32),
         jnp.dot(q_aug, jnp.concatenate([ks_n, eneg_ref[:, past_len:n_keys]], axis=0), preferred_element_type=F32)],
        axis=1)
    s_w = jnp.dot(qa, kw_all.astype(BF16), preferred_element_type=F32)
    yield
    kpos = lax.broadcasted_iota(jnp.int32, s_s.shape, 1)
    s_s = jnp.where(kpos <= qpos(s_s.shape), s_s, NEG)
    e = jnp.exp(s_s - jnp.max(s_s, axis=-1, keepdims=True))
    yield
    p_s = (e / jnp.sum(e, axis=-1, keepdims=True)).astype(BF16)
    yield
    o_sel = _dot_nt(p_s[:, 0:past_len], vs_p) + _dot_nt(p_s[:, past_len:n_keys], vs_n)
    yield
    kwpos = past_len - wl + lax.broadcasted_iota(jnp.int32, s_w.shape, 1)
    qp = qpos(s_w.shape)
    okw = (kwpos <= qp) & (kwpos > qp - WINDOW) & (kwpos >= 0)
    p_w = _masked_softmax(s_w, okw)
    yield
    o_w = _dot_nt(p_w.astype(BF16), vw_all.astype(BF16))
    yield

    sig = _sigmoid(gn_ref[u])
    o = None
    for j, ob in enumerate((o_c, o_sel, o_w)):
        for g in range(GQA):
            for k in range(N_KV):
                col = 3 * (k * GQA + g) + j
                g_s[u, j, (g * N_KV + k) * ts:(g * N_KV + k + 1) * ts] = jnp.broadcast_to(sig[:, col:col + 1],
                                                                                        (ts, LANES))
        o = g_s[u, j] * ob if o is None else o + g_s[u, j] * ob
    o_s[u] = o
    for j in range(N_HEADS // 2):
        tile = None
        for h in (2 * j, 2 * j + 1):
            k, g = h // GQA, h % GQA
            piece = _head_rows(o_s[u, (g * N_KV + k) * ts:(g * N_KV + k + 1) * ts], k, h % 2, lane_t)
            tile = piece if tile is None else tile + piece
        o_ref[u, :, j * LANES:(j + 1) * LANES] = tile


def _nsa_sample(page_table, q, gn, new_rows, win_k, win_v, caches, selk, selv, eneg, l, new_win, *, n_pool, depth):
    bd, ts, _ = q.shape
    n_pages = page_table.shape[1]
    past_len = n_pages * PAGE_SIZE
    wl = win_k.shape[2]
    n_keys = past_len + LANES
    assert ts <= LANES and -(-(past_len + ts) // L_SLC) * 2 <= LANES and eneg.shape == (LANES, n_keys)
    rows = GQA * N_KV * ts
    n_seq = NSA_SAMPLE_SEQS
    assert bd % n_seq == 0 and n_seq * N_KV * ts <= LANES
    tok = lambda w: pl.BlockSpec((n_seq, ts, w), lambda b, pt: (b, 0, 0))
    win = pl.BlockSpec((n_seq, KV_W, wl), lambda b, pt: (l * (bd // n_seq) + b, 0, 0))
    sel = pl.BlockSpec((None, n_keys, LANES), lambda b, pt: (l, 0, 0))

    hbm = pl.BlockSpec(memory_space=pl.ANY)
    in_specs = ([tok(N_HEADS * HEAD_DIM), tok(LANES)] + [tok(KV_W)] * 6 + [win, win, sel, sel,
                pl.BlockSpec((LANES, n_keys), lambda b, pt: (0, 0))] + [hbm] * 6)
    grid_spec = pltpu.PrefetchScalarGridSpec(
        num_scalar_prefetch=1,
        grid=(bd // n_seq,),
        in_specs=in_specs,
        out_specs=[tok(N_HEADS * HEAD_DIM), win, win],
        scratch_shapes=[
            pltpu.VMEM((2, n_seq, 4, KV_W, past_len), F32),
            pltpu.SemaphoreType.DMA((2,)),
            pltpu.VMEM((n_seq, 6, LANES, KV_W), F32),
            pltpu.VMEM((n_seq, rows, LANES), F32), pltpu.VMEM((LANES, LANES), F32),
            pltpu.VMEM((n_seq, 3, rows, LANES), F32), pltpu.VMEM((n_seq, rows, LANES), F32),
        ],
    )
    win_shape = jax.ShapeDtypeStruct((depth * bd, KV_W, wl), F32)
    n_fixed = 1 + 2 + 6 + 5
    return pl.pallas_call(
        functools.partial(_nsa_sample_kernel, n_pages=n_pages, past_len=past_len, ts=ts, page0=l * n_pool),
        grid_spec=grid_spec,
        out_shape=[jax.ShapeDtypeStruct((bd, ts, N_HEADS * HEAD_DIM), F32), win_shape, win_shape],
        input_output_aliases={n_fixed: 1, n_fixed + 1: 2},
        compiler_params=_params("arbitrary"),
        name="nsa_sample",
    )(page_table.reshape(-1), q, gn, *new_rows, win_k, win_v, selk, selv, eneg, *new_win, *caches)


def _merge_ffn_kernel(x_ref, sc1_ref, sh1_ref, gt1_ref, sc2_ref, sh2_ref, gt2_ref, g1_ref, g2_ref, a_ref, b_ref, c_ref,
                      wgm_ref, wp_ref, wc_ref, wa_ref, wo_ref, wgu_ref, wd_ref, gf_ref, *outs, d_ff, n_parts, final):
    x = x_ref[0]
    d = x.shape[-1]
    h = _modnorm(x, g1_ref[...], sc1_ref[0], sh1_ref[0]).astype(BF16)
    branches = (
        jnp.dot(a_ref[0].astype(BF16), wp_ref[...], preferred_element_type=F32),
        jnp.dot(b_ref[0].astype(BF16), wc_ref[...], preferred_element_type=F32),
        jnp.dot(c_ref[0].astype(BF16), wa_ref[...], preferred_element_type=F32),
    )
    merged = None
    for j, br in enumerate(branches):
        gm = _sigmoid(jnp.dot(h, wgm_ref[:, j * d:(j + 1) * d], preferred_element_type=F32))
        merged = gm * br if merged is None else merged + gm * br
    x = x + gt1_ref[0] * jnp.dot(merged.astype(BF16), wo_ref[...], preferred_element_type=F32)

    h = _modnorm(x, g2_ref[...], sc2_ref[0], sh2_ref[0]).astype(BF16)
    part = d_ff // n_parts
    acc = None
    for c in range(n_parts):
        gp = jnp.dot(h, wgu_ref[:, c * part:(c + 1) * part], preferred_element_type=F32)
        up = jnp.dot(h, wgu_ref[:, d_ff + c * part:d_ff + (c + 1) * part], preferred_element_type=F32)
        act = (gp * _sigmoid(gp) * up).astype(BF16)
        dn = jnp.dot(act, wd_ref[c * part:(c + 1) * part, :], preferred_element_type=F32)
        acc = dn if acc is None else acc + dn
    y = x + gt2_ref[0] * acc
    outs[0][0] = y
    if final:
        ms = jnp.mean(y * y, axis=-1, keepdims=True)
        outs[1][0] = y * lax.rsqrt(ms + EPS) * gf_ref[...]


def _merge_ffn(x, mods, g1, g2, a, bo, c, wgm, wp, wc, wa, wo, wgu, wd, gf, l, *, tm, final, mod_layer=None):
    b, t, d = x.shape
    d_ff = wd.shape[1]
    tm = min(tm, t)
    n_parts = 2
    assert t % tm == 0 and d_ff % (n_parts * LANES) == 0
    row = lambda w: pl.BlockSpec((1, tm, w), lambda bi, i: (bi, i, 0))
    wsp = lambda w: pl.BlockSpec((None,) + w.shape[1:], lambda bi, i: (l, 0, 0), pipeline_mode=pl.Buffered(1))
    vec = pl.BlockSpec((None, 1, d), lambda bi, i: (l, 0, 0))
    n_out = 2 if final else 1
    return pl.pallas_call(
        functools.partial(_merge_ffn_kernel, d_ff=d_ff, n_parts=n_parts, final=final),
        grid=(b, t // tm),
        in_specs=[row(d)] + [_mod_spec(mods, col, tm, d, mod_layer) for col in (1, 0, 2, 4, 3, 5)] + [vec, vec,
                  row(a.shape[-1]), row(bo.shape[-1]), row(c.shape[-1]),
                  wsp(wgm), wsp(wp), wsp(wc), wsp(wa), wsp(wo), wsp(wgu), wsp(wd),
                  pl.BlockSpec((1, d), lambda bi, i: (0, 0))],
        out_specs=[row(d)] * n_out,
        out_shape=[jax.ShapeDtypeStruct((b, t, d), F32)] * n_out,
        compiler_params=_params("parallel", "parallel"),
        name="merge_ffn",
    )(x, *[mods] * 6, g1, g2, a, bo, c, wgm, wp, wc, wa, wo, wgu, wd, gf)


def _rope_tables(pos):
    half = ROPE_DIM // 2
    inv = ROPE_THETA ** (-jnp.arange(half, dtype=F32) * 2.0 / ROPE_DIM)
    ang = pos.astype(F32)[:, None] * inv[None, :]
    cos, sin = jnp.cos(ang), jnp.sin(ang)
    n = pos.shape[0]
    rest = HEAD_DIM - ROPE_DIM
    c = jnp.concatenate([cos, cos, jnp.ones((n, rest), F32)], axis=1)
    sa = jnp.concatenate([-sin, jnp.zeros((n, half + rest), F32)], axis=1)
    sb = jnp.concatenate([jnp.zeros((n, half), F32), sin, jnp.zeros((n, rest), F32)], axis=1)
    rep = LANES // HEAD_DIM
    return tuple(jnp.tile(a, (1, rep)) for a in (c, sa, sb))


def _block_bias(n_keys, lane_stride):
    key_blk = (jnp.arange(n_keys) // L_SLC)[:, None] * lane_stride
    return jnp.where(key_blk == jnp.arange(LANES)[None, :], NEG, 0.0).astype(BF16)


def kernel(x_prompt, x_sample, cache_cmp_k, cache_cmp_v, cache_slc_k, cache_slc_v, state_win_k, state_win_v,
           state_pool, state_conv, page_table, c_prompt, c_sample, w_ada, b_ada, g_norm_mix, g_norm_ffn, w_in,
           w_pool, s_pool, w_pool_out, w_dw, b_dw, g_conv_ln, b_conv_ln, w_conv_out, w_cmp_k, w_cmp_v,
           w_attn_out, w_out, w_gu, w_down, g_final):
    depth, d, _ = w_in.shape
    bp, tp, _ = x_prompt.shape
    bd, ts, _ = x_sample.shape
    wp_dim = w_pool_out.shape[1]
    wc_dim = w_conv_out.shape[1]
    n_pool = cache_cmp_k.shape[1]
    n_pages = page_table.shape[1]
    past_len = n_pages * PAGE_SIZE
    wl = state_win_k.shape[2]
    n_groups, gw = w_pool.shape[1], w_pool.shape[2]
    assert gw == POOL_WINDOWS_GW and n_groups == len(POOL_WINDOWS) and wp_dim == n_groups * gw

    n_a = wp_dim + 2 * wc_dim + N_HEADS * HEAD_DIM + 6 * KV_W + 3 * N_HEADS
    n_a_pad = n_a - 3 * N_HEADS + LANES
    w_in_a = jnp.pad(w_in[:, :, :n_a], ((0, 0), (0, 0), (0, n_a_pad - n_a))).astype(BF16)
    w_gm = w_in[:, :, n_a:].astype(BF16)
    eye = jnp.eye(n_groups, dtype=F32)
    wpool_bd = (w_pool[:, :, :, None, :] * eye[None, :, None, :, None]).reshape(depth, wp_dim, wp_dim).astype(BF16)
    wpo, wco, wao, wo = (w.astype(BF16) for w in (w_pool_out, w_conv_out, w_attn_out, w_out))
    wgu, wdn = w_gu.astype(BF16), w_down.astype(BF16)
    wck = jnp.broadcast_to(w_cmp_k[:, :, None], (depth, L_CMP, KV_W))
    wcv = jnp.broadcast_to(w_cmp_v[:, :, None], (depth, L_CMP, KV_W))
    vec3 = lambda a: a.reshape(depth, 1, -1)
    g_mix, g_ffn = vec3(g_norm_mix), vec3(g_norm_ffn)
    spool, bdw, gln, bln = vec3(s_pool), vec3(b_dw), vec3(g_conv_ln), vec3(b_conv_ln)
    gf = g_final.reshape(1, d)

    tabs_p = _rope_tables(jnp.arange(tp))
    tabs_s = _rope_tables(jnp.tile(past_len + jnp.arange(ts), bd))
    eneg_p = _block_bias(tp, 1)
    n_keys_s = past_len + LANES
    eneg_s = _block_bias(n_keys_s, 2).T
    key = jnp.arange(n_keys_s)
    blk_of_key = (key[:, None] // L_CMP == jnp.arange(LANES)[None, :]).astype(F32)
    selk = (w_cmp_k[:, key % L_CMP, None] * blk_of_key[None]).astype(BF16)
    selv = (w_cmp_v[:, key % L_CMP, None] * blk_of_key[None]).astype(BF16)

    n_srows = bd * ts
    ada = _ada_all(jnp.concatenate([jnp.repeat(c_sample, ts, axis=0), c_prompt], axis=0), w_ada, b_ada)
    pos_minor = lambda a: a.transpose(0, 1, 3, 4, 2).reshape(a.shape[0] * a.shape[1], KV_W, a.shape[2])
    caches = tuple(pos_minor(c) for c in (cache_cmp_k, cache_cmp_v, cache_slc_k, cache_slc_v))
    win_k_all, win_v_all = pos_minor(state_win_k), pos_minor(state_win_v)
    new_win = [jnp.zeros(win_k_all.shape, F32), jnp.zeros(win_v_all.shape, F32)]

    xp = x_prompt
    xs = x_sample.reshape(1, bd * ts, d)
    st_p, st_s = [], []
    yp = ys = None
    for l in range(depth):
        final = l == depth - 1
        mp = ada[l, n_srows:][:, None, :]

        (up, uc, q, kc, vc, ks, vs, kw, vw, gn, ck, cv, ks_b, vs_b, kw_b, vw_b) = _inproj(
            xp, mp, g_mix, w_in_a, l, tabs_p, wck, wcv, w_pool=wp_dim, w_conv=wc_dim, with_cmp=True,
            q_dtype=BF16, tm=ROW_TILE)
        a_o, b_o, cst = _mix_prompt(up, uc, wpool_bd, spool, w_dw, bdw, gln, bln, l, tt=ROW_TILE)
        n_cmp = tp // L_CMP
        perm = lambda a: a.reshape(bp, n_cmp // 2, 2, KV_W).transpose(0, 2, 1, 3).reshape(bp, n_cmp, KV_W).astype(BF16)
        c_o = _nsa_prompt(q, gn, perm(ck), perm(cv), ks_b, vs_b, kw_b, vw_b, eneg_p)
        res = _merge_ffn(xp, mp, g_mix, g_ffn, a_o, b_o, c_o, w_gm, wpo, wco, wao, wo, wgu, wdn, gf, l,
                         tm=ROW_TILE, final=final)
        xp = res[0]
        if final:
            yp = res[1]
        kv4 = lambda a: a.reshape(a.shape[0], a.shape[1], N_KV, HEAD_DIM)
        wlp = min(WINDOW, tp)
        st_p.append((kv4(kc), kv4(vc), kv4(ks), kv4(vs), kv4(kw[:, tp - wlp:]), kv4(vw[:, tp - wlp:]),
                     up[:, tp - (max(POOL_WINDOWS) - 1):], cst[:, HALO - (CONV_K - 1):]))

        (up, uc, q, kc, vc, ks, vs, kw, vw, gn) = _inproj(
            xs, ada, g_mix, w_in_a, l, tabs_s, wck, wcv, w_pool=wp_dim, w_conv=wc_dim, with_cmp=False,
            q_dtype=F32, tm=ROW_TILE, mod_layer=l)
        tmaj = lambda a: a.reshape(bd, ts, a.shape[-1]).transpose(1, 0, 2)
        a_t, b_t, glu_t = _mix_sample(state_pool[l].transpose(1, 0, 2), tmaj(up), state_conv[l].transpose(1, 0, 2),
                                      tmaj(uc), wpool_bd, spool, w_dw, bdw, gln, bln, l)
        bmaj = lambda a: a.transpose(1, 0, 2).reshape(1, bd * ts, a.shape[-1])
        per_seq = lambda a: a.reshape(bd, ts, a.shape[-1])
        c_o, *new_win = _nsa_sample(page_table, per_seq(q), per_seq(gn),
                                    [per_seq(a) for a in (kc, vc, ks, vs, kw, vw)], win_k_all, win_v_all, caches,
                                    selk, selv, eneg_s, l, new_win, n_pool=n_pool, depth=depth)
        res = _merge_ffn(xs, ada, g_mix, g_ffn, bmaj(a_t), bmaj(b_t), c_o.reshape(1, bd * ts, -1), w_gm, wpo, wco, wao,
                         wo, wgu, wdn, gf, l, tm=ROW_TILE, final=final, mod_layer=l)
        xs = res[0]
        if final:
            ys = res[1]
        kv4s = lambda a: a.reshape(bd, -1, N_KV, HEAD_DIM)
        new_pool = jnp.concatenate([state_pool[l][:, ts:], per_seq(up)], axis=1)
        new_conv = jnp.concatenate([state_conv[l][:, ts:], glu_t.transpose(1, 0, 2)], axis=1)
        st_s.append((kv4s(kc), kv4s(vc), kv4s(ks), kv4s(vs), None, None, new_pool, new_conv))

    win_out = [w.reshape(depth, bd, N_KV, HEAD_DIM, wl).transpose(0, 1, 4, 2, 3) for w in new_win]
    outs = [yp, ys.reshape(bd, ts, d)]
    for i in range(8):
        outs.append(jnp.stack([s[i] for s in st_p]))
        outs.append(win_out[i - 4] if i in (4, 5) else jnp.stack([s[i] for s in st_s]))
    return tuple(outs)
```

```python
import functools

import jax
import jax.numpy as jnp
from jax import lax
from jax.experimental import pallas as pl
from jax.experimental.pallas import tpu as pltpu

F32 = jnp.float32
BF16 = jnp.bfloat16

HEAD_DIM = 64
N_KV = 2
GQA = 4
N_HEADS = N_KV * GQA
ROPE_DIM = 16
ROPE_THETA = 500000.0
L_CMP = 32
L_SLC = 64
L_SLC_SHIFT = 6
N_SEL = 16
WINDOW = 512
Q_BLOCK = 128
PAGE_SIZE = 128
POOL_WINDOWS = (2, 4, 8, 16)
CONV_K = 31
EPS = 1e-6
NEG = -1e30
FORCE_SCORE = 1e4

LANES = 128
SUBLANES = 8
VMEM_LIMIT_BYTES = 56 * 1024 * 1024

NSA_CHUNK_KEYS = 512
NSA_PROMPT_BLOCKS = 2
NSA_SAMPLE_SEQS = 4
NSA_ROW_TILE = 32
ROW_TILE = 512
ADA_COL_TILE = 1536
POOL_WINDOWS_GW = 64
HALO = 32
KV_W = N_KV * HEAD_DIM


def _params(*sem):
    return pltpu.CompilerParams(dimension_semantics=sem, vmem_limit_bytes=VMEM_LIMIT_BYTES)


def _modnorm(x, g, sc, sh):
    ms = jnp.mean(x * x, axis=-1, keepdims=True)
    return (x * lax.rsqrt(ms + EPS) * g) * (1.0 + sc) + sh


def _sigmoid(x):
    return 1.0 / (1.0 + jnp.exp(-x))


def _dot_nt(a, b):
    return lax.dot_general(a, b, (((1,), (1,)), ((), ())), preferred_element_type=F32)


def _masked_softmax(s, mask):
    sm = jnp.where(mask, s, NEG)
    e = jnp.exp(sm - jnp.max(sm, axis=-1, keepdims=True))
    return jnp.where(mask, e / jnp.sum(e, axis=-1, keepdims=True), 0.0)


def _ada_kernel(c_ref, w_ref, b_ref, o_ref):
    c = c_ref[...]
    a = (c * _sigmoid(c)).astype(BF16)
    o_ref[...] = jnp.dot(a, w_ref[...].astype(BF16), preferred_element_type=F32) + b_ref[...]


def _ada_all(c_all, w_ada, b_ada):
    depth, d, n = w_ada.shape
    tn = ADA_COL_TILE
    assert n % tn == 0
    nb = c_all.shape[0]
    return pl.pallas_call(
        _ada_kernel,
        grid=(depth, n // tn),
        in_specs=[
            pl.BlockSpec((nb, d), lambda l, j: (0, 0)),
            pl.BlockSpec((None, d, tn), lambda l, j: (l, 0, j)),
            pl.BlockSpec((None, 1, tn), lambda l, j: (l, 0, j)),
        ],
        out_specs=pl.BlockSpec((None, nb, tn), lambda l, j: (l, 0, j)),
        out_shape=jax.ShapeDtypeStruct((depth, nb, n), F32),
        compiler_params=_params("parallel", "parallel"),
        name="ada_mod",
    )(c_all, w_ada, b_ada.reshape(depth, 1, n))


def _inproj_kernel(x_ref, sc_ref, sh_ref, g_ref, w_ref, cos_ref, sa_ref, sb_ref, wck_ref, wcv_ref, *outs,
                   w_pool, w_conv, with_cmp):
    (up_ref, uc_ref, q_ref, kc_ref, vc_ref, ks_ref, vs_ref, kw_ref, vw_ref, gn_ref) = outs[:10]
    h = _modnorm(x_ref[0], g_ref[...], sc_ref[0], sh_ref[0]).astype(BF16)
    y = jnp.dot(h, w_ref[...], preferred_element_type=F32)
    cos, sa, sb = cos_ref[...], sa_ref[...], sb_ref[...]

    def rope(t):
        return t * cos + pltpu.roll(t, LANES - ROPE_DIM // 2, 1) * sa + pltpu.roll(t, ROPE_DIM // 2, 1) * sb

    o = 0
    up_ref[0] = y[:, o:o + w_pool]
    o += w_pool
    uc_ref[0] = y[:, o:o + 2 * w_conv]
    o += 2 * w_conv
    for j in range(N_HEADS * HEAD_DIM // LANES):
        q_ref[0, :, j * LANES:(j + 1) * LANES] = rope(y[:, o:o + LANES]).astype(q_ref.dtype)
        o += LANES
    kc = rope(y[:, o:o + KV_W])
    vc = y[:, o + KV_W:o + 2 * KV_W]
    ks = rope(y[:, o + 2 * KV_W:o + 3 * KV_W])
    vs = y[:, o + 3 * KV_W:o + 4 * KV_W]
    kw = rope(y[:, o + 4 * KV_W:o + 5 * KV_W])
    vw = y[:, o + 5 * KV_W:o + 6 * KV_W]
    o += 6 * KV_W
    kc_ref[0], vc_ref[0], ks_ref[0], vs_ref[0], kw_ref[0], vw_ref[0] = kc, vc, ks, vs, kw, vw
    gn_ref[0] = y[:, o:o + LANES]
    if with_cmp:
        ck_ref, cv_ref = outs[10:12]
        tm = kc.shape[0]
        ck_ref[0] = jnp.sum(kc.reshape(tm // L_CMP, L_CMP, KV_W) * wck_ref[...][None], axis=1)
        cv_ref[0] = jnp.sum(vc.reshape(tm // L_CMP, L_CMP, KV_W) * wcv_ref[...][None], axis=1)
        for ref, val in zip(outs[12:], (ks, vs, kw, vw)):
            ref[0] = val.astype(BF16)


def _mod_spec(mods, col, tm, d, layer=None):
    if layer is not None:
        return pl.BlockSpec((1, tm, d), lambda bi, i: (layer, i, col))
    assert mods.shape[1] == 1
    return pl.BlockSpec((1, 1, d), lambda bi, i: (bi, 0, col))


def _inproj(x, mods, g, w_a, l, tabs, wck, wcv, *, w_pool, w_conv, with_cmp, q_dtype, tm, mod_layer=None):
    b, t, d = x.shape
    n = w_a.shape[-1]
    tm = min(tm, t)
    assert t % tm == 0 and (tm % L_CMP == 0 or not with_cmp)
    tab_spec = pl.BlockSpec((tm, LANES), lambda bi, i: (i, 0))
    row = lambda w: pl.BlockSpec((1, tm, w), lambda bi, i: (bi, i, 0))
    widths = [w_pool, 2 * w_conv, N_HEADS * HEAD_DIM] + [KV_W] * 6 + [LANES]
    dtypes = [F32, F32, q_dtype] + [F32] * 7
    out_specs = [row(w) for w in widths]
    out_shape = [jax.ShapeDtypeStruct((b, t, w), dt) for w, dt in zip(widths, dtypes)]
    if with_cmp:
        out_specs += [pl.BlockSpec((1, tm // L_CMP, KV_W), lambda bi, i: (bi, i, 0))] * 2 + [row(KV_W)] * 4
        out_shape += ([jax.ShapeDtypeStruct((b, t // L_CMP, KV_W), F32)] * 2
                      + [jax.ShapeDtypeStruct((b, t, KV_W), BF16)] * 4)
    return pl.pallas_call(
        functools.partial(_inproj_kernel, w_pool=w_pool, w_conv=w_conv, with_cmp=with_cmp),
        grid=(b, t // tm),
        in_specs=[
            row(d), _mod_spec(mods, 1, tm, d, mod_layer), _mod_spec(mods, 0, tm, d, mod_layer),
            pl.BlockSpec((None, 1, d), lambda bi, i: (l, 0, 0)),
            pl.BlockSpec((None, d, n), lambda bi, i: (l, 0, 0)),
            tab_spec, tab_spec, tab_spec,
            pl.BlockSpec((None, L_CMP, KV_W), lambda bi, i: (l, 0, 0)),
            pl.BlockSpec((None, L_CMP, KV_W), lambda bi, i: (l, 0, 0)),
        ],
        out_specs=out_specs,
        out_shape=out_shape,
        compiler_params=_params("parallel", "parallel"),
        name="in_proj",
    )(x, mods, mods, g, w_a, *tabs, wck, wcv)


def _pool_means(ext_ref, lvl_ref, base, rows, pos0):
    assert POOL_WINDOWS == (2, 4, 8, 16) and base == 4 * SUBLANES
    n = base + rows
    lane = lax.broadcasted_iota(jnp.int32, (rows, LANES), 1)
    pos = lax.broadcasted_iota(jnp.int32, (rows, LANES), 0) + pos0
    low = lane < POOL_WINDOWS_GW
    means = []
    for tile in range(len(POOL_WINDOWS) // 2):
        cols = slice(tile * LANES, (tile + 1) * LANES)
        w_small, w_big = POOL_WINDOWS[2 * tile], POOL_WINDOWS[2 * tile + 1]
        sums = {}
        src, w, lo = None, 1, 0
        while w < w_big:
            lo += SUBLANES
            if src is None:
                cur = ext_ref[lo:n, cols] + ext_ref[lo - w:n - w, cols]
            else:
                cur = lvl_ref[src, lo:n] + lvl_ref[src, lo - w:n - w]
            w *= 2
            sums[w] = cur[base - lo:]
            if w < w_big:
                src = 0 if src != 0 else 1
                lvl_ref[src, lo:n] = cur
        cnt = jnp.minimum(jnp.where(low, w_small, w_big), pos + 1).astype(F32)
        means.append(jnp.where(low, sums[w_small], sums[w_big]) / cnt)
    return jnp.concatenate(means, axis=1)


def _layernorm_silu(y, g, b):
    mu = jnp.mean(y, axis=-1, keepdims=True)
    yc = y - mu
    var = jnp.mean(yc * yc, axis=-1, keepdims=True)
    z = yc * lax.rsqrt(var + EPS) * g + b
    return z * _sigmoid(z)


def _mix_prompt_kernel(up_ref, uph_ref, uc_ref, uch_ref, wpool_ref, spool_ref, wdw_ref, bdw_ref, gln_ref, bln_ref,
                       a_ref, b_ref, cst_ref, pext_ref, cext_ref, part_ref, lvl_ref, *, tt, w_conv):
    i = pl.program_id(1)
    first = i == 0
    u = up_ref[0]
    pext_ref[0:HALO] = jnp.where(first, 0.0, uph_ref[0])
    pext_ref[HALO:HALO + tt] = u
    d = _pool_means(pext_ref, lvl_ref, HALO, tt, i * tt) - u
    z = jnp.dot(d.astype(BF16), wpool_ref[...], preferred_element_type=F32) * spool_ref[...]
    a_ref[0] = z.astype(a_ref.dtype)
    hc = uch_ref[0]
    cext_ref[0:HALO] = jnp.where(first, 0.0, hc[:, :w_conv] * _sigmoid(hc[:, w_conv:]))
    uc = uc_ref[0]
    cext_ref[HALO:HALO + tt] = uc[:, :w_conv] * _sigmoid(uc[:, w_conv:])
    off = HALO - (CONV_K - 1)
    acc = None
    for m in range(SUBLANES):
        n_rows = tt + SUBLANES * (-(-(off + m) // SUBLANES))
        part = None
        for k in range(m, CONV_K, SUBLANES):
            assert k - m + n_rows <= HALO + tt
            term = wdw_ref[k:k + 1, :] * cext_ref[k - m:k - m + n_rows]
            part = term if part is None else part + term
        part_ref[m, 0:n_rows] = part
        shifted = part_ref[m, off + m:off + m + tt]
        acc = shifted if acc is None else acc + shifted
    b_ref[0] = _layernorm_silu(acc + bdw_ref[...], gln_ref[...], bln_ref[...]).astype(b_ref.dtype)
    cst_ref[0] = cext_ref[tt:tt + HALO]


def _mix_prompt(up, uc, wpool_bd, spool, wdw, bdw, gln, bln, l, *, tt):
    b, t, w_pool = up.shape
    w_conv = uc.shape[-1] // 2
    tt = min(tt, t)
    assert t % tt == 0 and tt % HALO == 0
    r = tt // HALO
    halo = lambda w: pl.BlockSpec((1, HALO, w), lambda bi, i: (bi, jnp.maximum(i * r - 1, 0), 0))
    vec = lambda w: pl.BlockSpec((None, 1, w), lambda bi, i: (l, 0, 0))
    return pl.pallas_call(
        functools.partial(_mix_prompt_kernel, tt=tt, w_conv=w_conv),
        grid=(b, t // tt),
        in_specs=[
            pl.BlockSpec((1, tt, w_pool), lambda bi, i: (bi, i, 0)), halo(w_pool),
            pl.BlockSpec((1, tt, 2 * w_conv), lambda bi, i: (bi, i, 0)), halo(2 * w_conv),
            pl.BlockSpec((None, w_pool, w_pool), lambda bi, i: (l, 0, 0)), vec(w_pool),
            pl.BlockSpec((None, CONV_K, w_conv), lambda bi, i: (l, 0, 0)), vec(w_conv), vec(w_conv), vec(w_conv),
        ],
        out_specs=[
            pl.BlockSpec((1, tt, w_pool), lambda bi, i: (bi, i, 0)),
            pl.BlockSpec((1, tt, w_conv), lambda bi, i: (bi, i, 0)),
            pl.BlockSpec((1, HALO, w_conv), lambda bi, i: (bi, 0, 0)),
        ],
        out_shape=[
            jax.ShapeDtypeStruct((b, t, w_pool), BF16),
            jax.ShapeDtypeStruct((b, t, w_conv), BF16),
            jax.ShapeDtypeStruct((b, HALO, w_conv), F32),
        ],
        scratch_shapes=[pltpu.VMEM((HALO + tt, w_pool), F32), pltpu.VMEM((HALO + tt, w_conv), F32),
                        pltpu.VMEM((SUBLANES, tt + 2 * SUBLANES, w_conv), F32),
                        pltpu.VMEM((2, HALO + tt, LANES), F32)],
        compiler_params=_params("parallel", "arbitrary"),
        name="mix_prompt",
    )(up, up, uc, uc, wpool_bd, spool, wdw, bdw, gln, bln)


def _mix_sample_kernel(pst_ref, up_ref, cst_ref, uc_ref, wpool_ref, spool_ref, wdw_ref, bdw_ref, gln_ref, bln_ref,
                       a_ref, b_ref, glu_ref, *, w_conv):
    n_p, n_c, ts = pst_ref.shape[0], cst_ref.shape[0], up_ref.shape[0]
    prow = lambda r: pst_ref[r] if r < n_p else up_ref[r - n_p]
    low = lax.broadcasted_iota(jnp.int32, (up_ref.shape[1], LANES), 1) < POOL_WINDOWS_GW
    glu = []
    for t in range(ts):
        uc = uc_ref[t]
        glu.append(uc[:, :w_conv] * _sigmoid(uc[:, w_conv:]))
        glu_ref[t] = glu[t]
    crow = lambda r: cst_ref[r] if r < n_c else glu[r - n_c]
    for t in range(ts):
        u = up_ref[t]
        means = None
        acc = u
        sums = {1: acc}
        for s in range(1, max(POOL_WINDOWS)):
            acc = acc + prow(n_p + t - s)
            sums[s + 1] = acc
        tiles = []
        for tile in range(len(POOL_WINDOWS) // 2):
            cols = slice(tile * LANES, (tile + 1) * LANES)
            w_small, w_big = POOL_WINDOWS[2 * tile], POOL_WINDOWS[2 * tile + 1]
            tiles.append(jnp.where(low, sums[w_small][:, cols] / float(w_small),
                                   sums[w_big][:, cols] / float(w_big)))
        means = jnp.concatenate(tiles, axis=1)
        z = jnp.dot((means - u).astype(BF16), wpool_ref[...], preferred_element_type=F32) * spool_ref[...]
        a_ref[t] = z.astype(a_ref.dtype)
        acc = wdw_ref[0:1, :] * crow(t + n_c - (CONV_K - 1))
        for k in range(1, CONV_K):
            acc = acc + wdw_ref[k:k + 1, :] * crow(t + n_c - (CONV_K - 1) + k)
        b_ref[t] = _layernorm_silu(acc + bdw_ref[...], gln_ref[...], bln_ref[...]).astype(b_ref.dtype)


def _mix_sample(pst_t, up_t, cst_t, uc_t, wpool_bd, spool, wdw, bdw, gln, bln, l):
    ts, bd, w_pool = up_t.shape
    w_conv = uc_t.shape[-1] // 2
    full = lambda a: pl.BlockSpec(a.shape, lambda i: (0,) * a.ndim)
    vec = lambda w: pl.BlockSpec((None, 1, w), lambda i: (l, 0, 0))
    return pl.pallas_call(
        functools.partial(_mix_sample_kernel, w_conv=w_conv),
        grid=(1,),
        in_specs=[
            full(pst_t), full(up_t), full(cst_t), full(uc_t),
            pl.BlockSpec((None, w_pool, w_pool), lambda i: (l, 0, 0)), vec(w_pool),
            pl.BlockSpec((None, CONV_K, w_conv), lambda i: (l, 0, 0)), vec(w_conv), vec(w_conv), vec(w_conv),
        ],
        out_specs=[pl.BlockSpec((ts, bd, w_pool), lambda i: (0, 0, 0)),
                   pl.BlockSpec((ts, bd, w_conv), lambda i: (0, 0, 0)),
                   pl.BlockSpec((ts, bd, w_conv), lambda i: (0, 0, 0))],
        out_shape=[jax.ShapeDtypeStruct((ts, bd, w_pool), BF16),
                   jax.ShapeDtypeStruct((ts, bd, w_conv), BF16),
                   jax.ShapeDtypeStruct((ts, bd, w_conv), F32)],
        compiler_params=_params("arbitrary"),
        name="mix_sample",
    )(pst_t, up_t, cst_t, uc_t, wpool_bd, spool, wdw, bdw, gln, bln)


def _topk_masks(scores, n_sel, forced=None, n_rounds=None):
    sts = tuple(s.T for s in scores)
    cand = lax.broadcasted_iota(jnp.int32, sts[0].shape, 0).astype(F32)
    sel0 = jnp.zeros(sts[0].shape, F32)
    quota = None
    if forced is not None:
        sel0 = forced.T
        quota = n_sel - jnp.sum(sel0, axis=0, keepdims=True)
        sts = tuple(jnp.where(sel0 > 0, -jnp.inf, st) for st in sts)

    def body(t, carry):
        out = []
        for st, sel in carry:
            m = jnp.max(st, axis=0, keepdims=True)
            first = jnp.min(jnp.where(st == m, cand, float(LANES)), axis=0, keepdims=True)
            hit = cand == first
            if quota is not None:
                hit = hit & (t.astype(F32) < quota)
            out.append((jnp.where(hit, -jnp.inf, st), jnp.where(hit, 1.0, sel)))
        return tuple(out)

    res = lax.fori_loop(0, n_sel if n_rounds is None else n_rounds, body, tuple((st, sel0) for st in sts))
    return [sel.T for _, sel in res]


def _head_rows(tile, src_half, dst_half, lane):
    if src_half != dst_half:
        tile = pltpu.roll(tile, HEAD_DIM, 1)
    return jnp.where((lane >= dst_half * HEAD_DIM) & (lane < (dst_half + 1) * HEAD_DIM), tile, 0.0)


def _nsa_prompt_multi_kernel(q_ref, gn_ref, ck_ref, cv_ref, ks_ref, vs_ref, kw_ref, vw_ref, eneg_ref, o_ref,
                             qa_scr, s_scr, p_scr, sw_scr, pw_scr, *, ck_keys, n_sub):
    step = pl.program_id(1)
    qb = Q_BLOCK
    rows = GQA * qb
    scale = HEAD_DIM ** -0.5
    n_cmp = ck_ref.shape[1]
    n_slc = n_cmp // 2
    n_wchunks = WINDOW // qb + 1
    wk = n_wchunks * qb
    lane = lax.broadcasted_iota(jnp.int32, (qb, LANES), 1)
    chains = [(u, k) for u in range(n_sub) for k in range(N_KV)]

    def qblk(u):
        return step * n_sub + u

    def qpos(u, shape, row0=0):
        return qblk(u) * qb + ((row0 + lax.broadcasted_iota(jnp.int32, shape, 0)) & (qb - 1))

    def qrows(u, ref, lo, hi):
        return ref[0, u * qb:(u + 1) * qb, lo:hi]

    st = {}

    def phase1(u):
        cur = qpos(u, lane.shape) >> L_SLC_SHIFT
        valid_s = lane <= cur
        forced = (lane == 0) | (lane == cur) | (lane == cur - 1)
        st[u, "valid"] = valid_s
        st[u, "forced"] = jnp.where(forced & valid_s, 1.0, 0.0)
        for k in range(N_KV):
            qh = []
            for g in range(GQA):
                h = k * GQA + g
                tile = qrows(u, q_ref, (h // 2) * LANES, (h // 2 + 1) * LANES).astype(F32) * scale
                qh.append(_head_rows(tile, h % 2, k, lane))
            qk = jnp.concatenate(qh, axis=0).astype(BF16)
            st[u, k, "qk"] = qk
            s_c = _dot_nt(qk, ck_ref[0])
            yield
            cl = lax.broadcasted_iota(jnp.int32, (rows, n_cmp), 1)
            c_end = jnp.where(cl < n_slc, cl * L_SLC + L_CMP - 1, (cl - n_slc) * L_SLC + L_SLC - 1)
            p_c = _masked_softmax(s_c, c_end <= qpos(u, s_c.shape))
            yield
            st[u, k, "o_c"] = jnp.dot(p_c.astype(BF16), cv_ref[0], preferred_element_type=F32)
            ps = p_c[0:qb]
            for g in range(1, GQA):
                ps = ps + p_c[g * qb:(g + 1) * qb]
            imp = ps[:, :n_slc] + ps[:, n_slc:]
            if n_slc < LANES:
                imp = jnp.concatenate([imp, jnp.zeros((qb, LANES - n_slc), F32)], axis=1)
            score = jnp.where(valid_s, jnp.where(forced, FORCE_SCORE, imp), -1.0)
            st[u, k, "score"] = jnp.where(lane < n_slc, score, -jnp.inf)
            yield

    _lockstep(phase1(u) for u in range(n_sub))

    for u in range(n_sub):
        n_sel = min(N_SEL, n_slc)
        n_rounds = n_sel - jnp.where(qblk(u) == 0, 1, 3)
        for k, sel in enumerate(_topk_masks([st[u, k, "score"] for k in range(N_KV)], n_sel, st[u, "forced"],
                                            n_rounds)):
            notsel = (1.0 - jnp.where(st[u, "valid"], sel, 0.0)).astype(BF16)
            qa_scr[u * N_KV + k] = jnp.concatenate([st[u, k, "qk"], jnp.concatenate([notsel] * GQA, axis=0)], axis=1)

    own = lax.broadcasted_iota(jnp.int32, (ck_keys, LANES), 1) < HEAD_DIM

    def scores(c, buf):
        start = pl.multiple_of(c * ck_keys, ck_keys)
        k_aug = jnp.concatenate([ks_ref[0, pl.ds(start, ck_keys), :], eneg_ref[pl.ds(start, ck_keys), :]], axis=1)
        nc = len(chains)
        s_all = _dot_nt(qa_scr[...].reshape(nc * rows, 2 * LANES), k_aug)
        s_scr[buf] = s_all.reshape(nc, rows, ck_keys)

    def probs(c, buf, ms, causal):
        new_m, alphas = [], []
        for n, (u, k) in enumerate(chains):
            m_tiles = []
            for r in range(0, rows, NSA_ROW_TILE):
                rs = slice(r, r + NSA_ROW_TILE)
                s = s_scr[buf, n, rs]
                if causal:
                    kpos = c * ck_keys + lax.broadcasted_iota(jnp.int32, s.shape, 1)
                    s = jnp.where(kpos <= qpos(u, s.shape, r), s, NEG)
                m_new = jnp.maximum(ms[n][rs], jnp.max(s, axis=-1, keepdims=True))
                p_scr[buf, n, rs] = jnp.exp(s - m_new).astype(BF16)
                m_tiles.append(m_new)
            m_new = jnp.concatenate(m_tiles, axis=0)
            alphas.append(jnp.exp(ms[n] - m_new))
            new_m.append(m_new)
        return new_m, alphas

    def values(c, buf, alphas, accs):
        start = pl.multiple_of(c * ck_keys, ck_keys)
        v = vs_ref[0, pl.ds(start, ck_keys), :]
        v_one = [jnp.where(own, v, 1.0), jnp.where(own, 1.0, v)]
        return [alphas[n] * accs[n] + jnp.dot(p_scr[buf, n], v_one[k], preferred_element_type=F32)
                for n, (u, k) in enumerate(chains)]

    def pair(c, carry, causal, last):
        ms, alphas, accs = carry
        accs = values(jnp.maximum(c - 1, 0), 1, alphas, accs)
        ms, alphas = probs(c, 0, ms, causal)
        scores(c + 1, 1)
        accs = values(c, 0, alphas, accs)
        ms, alphas = probs(c + 1, 1, ms, causal)
        if last:
            accs = values(c + 1, 1, alphas, accs)
        else:
            scores(c + 2, 0)
        return ms, alphas, accs

    n_pairs = (qblk(n_sub - 1) * qb + qb + 2 * ck_keys - 1) // (2 * ck_keys)
    p_scr[1] = jnp.zeros(p_scr.shape[1:], BF16)
    scores(0, 0)
    nc = len(chains)
    carry = ([jnp.full((rows, 1), -jnp.inf, F32)] * nc, [jnp.ones((rows, 1), F32)] * nc,
             [jnp.zeros((rows, LANES), F32)] * nc)
    carry = lax.fori_loop(0, n_pairs - 1, lambda j, cr: pair(2 * j, cr, False, False), carry)
    c_last = 2 * (n_pairs - 1)

    def last_single(cr):
        ms, alphas, accs = cr
        accs = values(jnp.maximum(c_last - 1, 0), 1, alphas, accs)
        ms, alphas = probs(c_last, 0, ms, True)
        return values(c_last, 0, alphas, accs)

    n_chunks = (qblk(n_sub - 1) * qb + qb + ck_keys - 1) // ck_keys
    accs = lax.cond(n_chunks == c_last + 1, last_single, lambda cr: pair(c_last, cr, True, True)[2], carry)
    o_ss = [acc / pltpu.roll(acc, HEAD_DIM, 1) for acc in accs]

    def tail(u):
        i = qblk(u)
        kws, vws = [], []
        for dj in range(n_wchunks):
            st0 = pl.multiple_of(jnp.maximum(i - (n_wchunks - 1) + dj, 0) * qb, qb)
            kws.append(kw_ref[0, pl.ds(st0, qb), :])
            vws.append(vw_ref[0, pl.ds(st0, qb), :])
        kw_blk, vw_blk = jnp.concatenate(kws, axis=0), jnp.concatenate(vws, axis=0)
        kwpos = (i - (n_wchunks - 1)) * qb + lax.broadcasted_iota(jnp.int32, (qb, wk), 1)
        qp = qpos(u, (qb, wk))
        wbias = jnp.where((kwpos <= qp) & (kwpos > qp - WINDOW) & (kwpos >= 0), 0.0, NEG)
        own_w = lax.broadcasted_iota(jnp.int32, (wk, LANES), 1) < HEAD_DIM
        vw_one = [jnp.where(own_w, vw_blk, 1.0), jnp.where(own_w, 1.0, vw_blk)]
        for k in range(N_KV):
            sw_scr[u * N_KV + k] = _dot_nt(st[u, k, "qk"], kw_blk)
        yield
        o_ws = []
        for k in range(N_KV):
            n = u * N_KV + k
            for r in range(0, rows, NSA_ROW_TILE):
                rs = slice(r, r + NSA_ROW_TILE)
                s = sw_scr[n, rs] + wbias[r % qb:r % qb + NSA_ROW_TILE]
                pw_scr[n, rs] = jnp.exp(s - jnp.max(s, axis=-1, keepdims=True)).astype(BF16)
            yield
            acc = jnp.dot(pw_scr[n], vw_one[k], preferred_element_type=F32)
            o_ws.append(acc / pltpu.roll(acc, HEAD_DIM, 1))
            yield
        sig = _sigmoid(qrows(u, gn_ref, 0, LANES))
        out_tiles = [None] * (N_HEADS // 2)
        for k in range(N_KV):
            o_c, o_s, o_w = st[u, k, "o_c"], o_ss[u * N_KV + k], o_ws[k]
            for g in range(GQA):
                h = k * GQA + g
                col = 3 * h
                r = slice(g * qb, (g + 1) * qb)
                o = (sig[:, col:col + 1] * o_c[r] + sig[:, col + 1:col + 2] * o_s[r]
                     + sig[:, col + 2:col + 3] * o_w[r])
                o = _head_rows(o, k, h % 2, lane)
                out_tiles[h // 2] = o if out_tiles[h // 2] is None else out_tiles[h // 2] + o
            yield
        for j, tile in enumerate(out_tiles):
            o_ref[0, u * qb:(u + 1) * qb, j * LANES:(j + 1) * LANES] = tile.astype(o_ref.dtype)

    _lockstep(tail(u) for u in range(n_sub))


def _nsa_prompt(q, gn, ckp, cvp, ks, vs, kw, vw, eneg):
    b, t, _ = q.shape
    assert t % Q_BLOCK == 0 and t // L_SLC <= LANES
    ck_keys = min(NSA_CHUNK_KEYS, t // 2)
    assert t % (2 * ck_keys) == 0
    rows = GQA * Q_BLOCK
    n_sub = NSA_PROMPT_BLOCKS
    assert (2 * ck_keys) % (n_sub * Q_BLOCK) == 0 and t % (n_sub * Q_BLOCK) == 0
    nc = n_sub * N_KV
    qspec = lambda w: pl.BlockSpec((1, n_sub * Q_BLOCK, w), lambda bi, i: (bi, i, 0))
    seq = lambda n: pl.BlockSpec((1, n, KV_W), lambda bi, i: (bi, 0, 0))
    return pl.pallas_call(
        functools.partial(_nsa_prompt_multi_kernel, ck_keys=ck_keys, n_sub=n_sub),
        scratch_shapes=[pltpu.VMEM((nc, rows, 2 * LANES), BF16), pltpu.VMEM((2, nc, rows, ck_keys), F32),
                        pltpu.VMEM((2, nc, rows, ck_keys), BF16),
                        pltpu.VMEM((nc, rows, WINDOW + Q_BLOCK), F32),
                        pltpu.VMEM((nc, rows, WINDOW + Q_BLOCK), BF16)],
        grid=(b, t // (n_sub * Q_BLOCK)),
        in_specs=[qspec(N_HEADS * HEAD_DIM), qspec(LANES), seq(t // L_CMP), seq(t // L_CMP),
                  seq(t), seq(t), seq(t), seq(t), pl.BlockSpec((t, LANES), lambda bi, i: (0, 0))],
        out_specs=qspec(N_HEADS * HEAD_DIM),
        out_shape=jax.ShapeDtypeStruct((b, t, N_HEADS * HEAD_DIM), BF16),
        compiler_params=_params("parallel", "parallel"),
        name="nsa_prompt",
    )(q, gn, ckp, cvp, ks, vs, kw, vw, eneg)


def _nsa_sample_kernel(pt_ref, q_ref, gn_ref, kcn_ref, vcn_ref, ksn_ref, vsn_ref, kwn_ref, vwn_ref, wk_ref, wv_ref,
                       selk_ref, selv_ref, eneg_ref, *rest, n_pages, past_len, ts, page0):
    caches = rest[2:6]
    o_ref, nwk_ref, nwv_ref = rest[6:9]
    (past_buf, sems, tail_s, q_s, sc_s, g_s, o_s) = rest[9:]
    n_seq = q_ref.shape[0]
    b = pl.program_id(0)
    slot = b & 1
    scale = HEAD_DIM ** -0.5
    wl = wk_ref.shape[2]
    n_keys = past_len + LANES
    n_slc = -(-(past_len + ts) // L_SLC)
    rows = GQA * N_KV * ts
    grp = N_KV * ts
    lane_t = lax.broadcasted_iota(jnp.int32, (ts, LANES), 1)

    def page_copy(step, dst_slot, u, j, p):
        src = caches[j].at[page0 + pt_ref[(step * n_seq + u) * n_pages + p]]
        dst = past_buf.at[dst_slot, u, j, :, pl.ds(p * PAGE_SIZE, PAGE_SIZE)]
        return pltpu.make_async_copy(src, dst, sems.at[dst_slot])

    def all_pages(step, dst_slot, op):
        for u in range(n_seq):
            for j in range(4):
                for p in range(n_pages):
                    op(page_copy(step, dst_slot, u, j, p))

    @pl.when(b == 0)
    def _():
        all_pages(0, 0, lambda cp: cp.start())

    @pl.when(b + 1 < pl.num_programs(0))
    def _():
        all_pages(b + 1, 1 - slot, lambda cp: cp.start())

    all_pages(b, slot, lambda cp: cp.wait())

    def qpos(shape):
        return past_len + (lax.broadcasted_iota(jnp.int32, shape, 0) & (ts - 1))

    lane_g = lax.broadcasted_iota(jnp.int32, (grp, LANES), 1)
    blk = lane_g >> 1
    is_blk = ((lane_g & 1) == 0) & (blk < n_slc)
    cur = qpos(lane_g.shape) >> L_SLC_SHIFT
    valid_s = blk <= cur
    forced = (blk == 0) | (blk == cur) | (blk == cur - 1)

    def front(u, out):
        def new_cols(j, new_ref):
            tail_s[u, j] = jnp.zeros(tail_s.shape[2:], F32)
            tail_s[u, j, 0:ts] = new_ref[u]
            return tail_s[u, j].T

        kc_p, vc_p, ks_p, vs_p = (past_buf[slot, u, j].astype(BF16) for j in range(4))
        kc_n, vc_n, ks_n, vs_n = (new_cols(j, r).astype(BF16)
                                  for j, r in enumerate((kcn_ref, vcn_ref, ksn_ref, vsn_ref)))

        kw_all = jnp.concatenate([wk_ref[u], new_cols(4, kwn_ref)], axis=1)
        vw_all = jnp.concatenate([wv_ref[u], new_cols(5, vwn_ref)], axis=1)
        nwk_ref[u] = kw_all[:, ts:ts + wl]
        nwv_ref[u] = vw_all[:, ts:ts + wl]
        yield

        for g in range(GQA):
            for k in range(N_KV):
                h = k * GQA + g
                tile = q_ref[u, :, (h // 2) * LANES:(h // 2 + 1) * LANES] * scale
                q_s[u, (g * N_KV + k) * ts:(g * N_KV + k + 1) * ts] = _head_rows(tile, h % 2, k, lane_t)
        qa = q_s[u].astype(BF16)

        def compress(past, new, sel_ref):
            return (jnp.dot(past, sel_ref[0:past_len], preferred_element_type=F32)
                    + jnp.dot(new, sel_ref[past_len:n_keys], preferred_element_type=F32)).astype(BF16)

        ck_t = compress(kc_p, kc_n, selk_ref)
        yield
        cv_t = compress(vc_p, vc_n, selv_ref)
        yield
        s_c = jnp.dot(qa, ck_t, preferred_element_type=F32)
        yield
        cl = lax.broadcasted_iota(jnp.int32, s_c.shape, 1)
        p_c = _masked_softmax(s_c, (cl + 1) * L_CMP - 1 <= qpos(s_c.shape))
        yield
        o_c = _dot_nt(p_c.astype(BF16), cv_t)
        ps = p_c[0:grp]
        for g in range(1, GQA):
            ps = ps + p_c[g * grp:(g + 1) * grp]
        imp = ps + pltpu.roll(ps, LANES - 1, 1)

        score = jnp.where(valid_s, jnp.where(forced, FORCE_SCORE, imp), -1.0)
        sc_s[u * grp:(u + 1) * grp] = jnp.where(is_blk, score, -jnp.inf)
        out[u] = (qa, o_c, ks_p, ks_n, vs_p, vs_n, kw_all, vw_all)

    sc_s[...] = jnp.zeros(sc_s.shape, F32)
    fronts = {}
    _lockstep(front(u, fronts) for u in range(n_seq))
    sel = _topk_masks([sc_s[...]], min(N_SEL, n_slc))[0]
    _lockstep(_nsa_sample_back(u, fronts[u], sel[u * grp:(u + 1) * grp], is_blk & valid_s, qpos, gn_ref, eneg_ref,
                               o_ref, g_s, o_s, past_len=past_len, ts=ts, wl=wl) for u in range(n_seq))


def _lockstep(stages):
    stages = list(stages)
    while stages:
        for gen in list(stages):
            try:
                next(gen)
            except StopIteration:
                stages.remove(gen)


def _nsa_sample_back(u, front, sel, selectable, qpos, gn_ref, eneg_ref, o_ref, g_s, o_s, *, past_len, ts, wl):
    qa, o_c, ks_p, ks_n, vs_p, vs_n, kw_all, vw_all = front
    n_keys = past_len + LANES
    lane_t = lax.broadcasted_iota(jnp.int32, (ts, LANES), 1)
    notsel = (1.0 - jnp.where(selectable, sel, 0.0)).astype(BF16)
    q_aug = jnp.concatenate([qa, jnp.concatenate([notsel] * GQA, axis=0)], axis=1)

    s_s = jnp.concatenate(
        [jnp.dot(q_aug, jnp.concatenate([ks_p, eneg_ref[:, 0:past_len]], axis=0), preferred_element_type=F32),
         jnp.dot(q_aug, jnp.concatenate([ks_n, eneg_ref[:, past_len:n_keys]], axis=0), preferred_element_type=F32)],
        axis=1)
    s_w = jnp.dot(qa, kw_all.astype(BF16), preferred_element_type=F32)
    yield
    kpos = lax.broadcasted_iota(jnp.int32, s_s.shape, 1)
    s_s = jnp.where(kpos <= qpos(s_s.shape), s_s, NEG)
    e = jnp.exp(s_s - jnp.max(s_s, axis=-1, keepdims=True))
    yield
    p_s = (e / jnp.sum(e, axis=-1, keepdims=True)).astype(BF16)
    yield
    o_sel = _dot_nt(p_s[:, 0:past_len], vs_p) + _dot_nt(p_s[:, past_len:n_keys], vs_n)
    yield
    kwpos = past_len - wl + lax.broadcasted_iota(jnp.int32, s_w.shape, 1)
    qp = qpos(s_w.shape)
    okw = (kwpos <= qp) & (kwpos > qp - WINDOW) & (kwpos >= 0)
    p_w = _masked_softmax(s_w, okw)
    yield
    o_w = _dot_nt(p_w.astype(BF16), vw_all.astype(BF16))
    yield

    sig = _sigmoid(gn_ref[u])
    o = None
    for j, ob in enumerate((o_c, o_sel, o_w)):
        for g in range(GQA):
            for k in range(N_KV):
                col = 3 * (k * GQA + g) + j
                g_s[u, j, (g * N_KV + k) * ts:(g * N_KV + k + 1) * ts] = jnp.broadcast_to(sig[:, col:col + 1],
                                                                                        (ts, LANES))
        o = g_s[u, j] * ob if o is None else o + g_s[u, j] * ob
    o_s[u] = o
    for j in range(N_HEADS // 2):
        tile = None
        for h in (2 * j, 2 * j + 1):
            k, g = h // GQA, h % GQA
            piece = _head_rows(o_s[u, (g * N_KV + k) * ts:(g * N_KV + k + 1) * ts], k, h % 2, lane_t)
            tile = piece if tile is None else tile + piece
        o_ref[u, :, j * LANES:(j + 1) * LANES] = tile


def _nsa_sample(page_table, q, gn, new_rows, win_k, win_v, caches, selk, selv, eneg, l, new_win, *, n_pool, depth):
    bd, ts, _ = q.shape
    n_pages = page_table.shape[1]
    past_len = n_pages * PAGE_SIZE
    wl = win_k.shape[2]
    n_keys = past_len + LANES
    assert ts <= LANES and -(-(past_len + ts) // L_SLC) * 2 <= LANES and eneg.shape == (LANES, n_keys)
    rows = GQA * N_KV * ts
    n_seq = NSA_SAMPLE_SEQS
    assert bd % n_seq == 0 and n_seq * N_KV * ts <= LANES
    tok = lambda w: pl.BlockSpec((n_seq, ts, w), lambda b, pt: (b, 0, 0))
    win = pl.BlockSpec((n_seq, KV_W, wl), lambda b, pt: (l * (bd // n_seq) + b, 0, 0))
    sel = pl.BlockSpec((None, n_keys, LANES), lambda b, pt: (l, 0, 0))

    hbm = pl.BlockSpec(memory_space=pl.ANY)
    in_specs = ([tok(N_HEADS * HEAD_DIM), tok(LANES)] + [tok(KV_W)] * 6 + [win, win, sel, sel,
                pl.BlockSpec((LANES, n_keys), lambda b, pt: (0, 0))] + [hbm] * 6)
    grid_spec = pltpu.PrefetchScalarGridSpec(
        num_scalar_prefetch=1,
        grid=(bd // n_seq,),
        in_specs=in_specs,
        out_specs=[tok(N_HEADS * HEAD_DIM), win, win],
        scratch_shapes=[
            pltpu.VMEM((2, n_seq, 4, KV_W, past_len), F32),
            pltpu.SemaphoreType.DMA((2,)),
            pltpu.VMEM((n_seq, 6, LANES, KV_W), F32),
            pltpu.VMEM((n_seq, rows, LANES), F32), pltpu.VMEM((LANES, LANES), F32),
            pltpu.VMEM((n_seq, 3, rows, LANES), F32), pltpu.VMEM((n_seq, rows, LANES), F32),
        ],
    )
    win_shape = jax.ShapeDtypeStruct((depth * bd, KV_W, wl), F32)
    n_fixed = 1 + 2 + 6 + 5
    return pl.pallas_call(
        functools.partial(_nsa_sample_kernel, n_pages=n_pages, past_len=past_len, ts=ts, page0=l * n_pool),
        grid_spec=grid_spec,
        out_shape=[jax.ShapeDtypeStruct((bd, ts, N_HEADS * HEAD_DIM), F32), win_shape, win_shape],
        input_output_aliases={n_fixed: 1, n_fixed + 1: 2},
        compiler_params=_params("arbitrary"),
        name="nsa_sample",
    )(page_table.reshape(-1), q, gn, *new_rows, win_k, win_v, selk, selv, eneg, *new_win, *caches)


def _merge_ffn_kernel(x_ref, sc1_ref, sh1_ref, gt1_ref, sc2_ref, sh2_ref, gt2_ref, g1_ref, g2_ref, a_ref, b_ref, c_ref,
                      wgm_ref, wp_ref, wc_ref, wa_ref, wo_ref, wgu_ref, wd_ref, gf_ref, *outs, d_ff, n_parts, final):
    x = x_ref[0]
    d = x.shape[-1]
    h = _modnorm(x, g1_ref[...], sc1_ref[0], sh1_ref[0]).astype(BF16)
    branches = (
        jnp.dot(a_ref[0].astype(BF16), wp_ref[...], preferred_element_type=F32),
        jnp.dot(b_ref[0].astype(BF16), wc_ref[...], preferred_element_type=F32),
        jnp.dot(c_ref[0].astype(BF16), wa_ref[...], preferred_element_type=F32),
    )
    merged = None
    for j, br in enumerate(branches):
        gm = _sigmoid(jnp.dot(h, wgm_ref[:, j * d:(j + 1) * d], preferred_element_type=F32))
        merged = gm * br if merged is None else merged + gm * br
    x = x + gt1_ref[0] * jnp.dot(merged.astype(BF16), wo_ref[...], preferred_element_type=F32)

    h = _modnorm(x, g2_ref[...], sc2_ref[0], sh2_ref[0]).astype(BF16)
    part = d_ff // n_parts
    acc = None
    for c in range(n_parts):
        gp = jnp.dot(h, wgu_ref[:, c * part:(c + 1) * part], preferred_element_type=F32)
        up = jnp.dot(h, wgu_ref[:, d_ff + c * part:d_ff + (c + 1) * part], preferred_element_type=F32)
        act = (gp * _sigmoid(gp) * up).astype(BF16)
        dn = jnp.dot(act, wd_ref[c * part:(c + 1) * part, :], preferred_element_type=F32)
        acc = dn if acc is None else acc + dn
    y = x + gt2_ref[0] * acc
    outs[0][0] = y
    if final:
        ms = jnp.mean(y * y, axis=-1, keepdims=True)
        outs[1][0] = y * lax.rsqrt(ms + EPS) * gf_ref[...]


def _merge_ffn(x, mods, g1, g2, a, bo, c, wgm, wp, wc, wa, wo, wgu, wd, gf, l, *, tm, final, mod_layer=None):
    b, t, d = x.shape
    d_ff = wd.shape[1]
    tm = min(tm, t)
    n_parts = 2
    assert t % tm == 0 and d_ff % (n_parts * LANES) == 0
    row = lambda w: pl.BlockSpec((1, tm, w), lambda bi, i: (bi, i, 0))
    wsp = lambda w: pl.BlockSpec((None,) + w.shape[1:], lambda bi, i: (l, 0, 0), pipeline_mode=pl.Buffered(1))
    vec = pl.BlockSpec((None, 1, d), lambda bi, i: (l, 0, 0))
    n_out = 2 if final else 1
    return pl.pallas_call(
        functools.partial(_merge_ffn_kernel, d_ff=d_ff, n_parts=n_parts, final=final),
        grid=(b, t // tm),
        in_specs=[row(d)] + [_mod_spec(mods, col, tm, d, mod_layer) for col in (1, 0, 2, 4, 3, 5)] + [vec, vec,
                  row(a.shape[-1]), row(bo.shape[-1]), row(c.shape[-1]),
                  wsp(wgm), wsp(wp), wsp(wc), wsp(wa), wsp(wo), wsp(wgu), wsp(wd),
                  pl.BlockSpec((1, d), lambda bi, i: (0, 0))],
        out_specs=[row(d)] * n_out,
        out_shape=[jax.ShapeDtypeStruct((b, t, d), F32)] * n_out,
        compiler_params=_params("parallel", "parallel"),
        name="merge_ffn",
    )(x, *[mods] * 6, g1, g2, a, bo, c, wgm, wp, wc, wa, wo, wgu, wd, gf)


def _rope_tables(pos):
    half = ROPE_DIM // 2
    inv = ROPE_THETA ** (-jnp.arange(half, dtype=F32) * 2.0 / ROPE_DIM)
    ang = pos.astype(F32)[:, None] * inv[None, :]
    cos, sin = jnp.cos(ang), jnp.sin(ang)
    n = pos.shape[0]
    rest = HEAD_DIM - ROPE_DIM
    c = jnp.concatenate([cos, cos, jnp.ones((n, rest), F32)], axis=1)
    sa = jnp.concatenate([-sin, jnp.zeros((n, half + rest), F32)], axis=1)
    sb = jnp.concatenate([jnp.zeros((n, half), F32), sin, jnp.zeros((n, rest), F32)], axis=1)
    rep = LANES // HEAD_DIM
    return tuple(jnp.tile(a, (1, rep)) for a in (c, sa, sb))


def _block_bias(n_keys, lane_stride):
    key_blk = (jnp.arange(n_keys) // L_SLC)[:, None] * lane_stride
    return jnp.where(key_blk == jnp.arange(LANES)[None, :], NEG, 0.0).astype(BF16)


def kernel(x_prompt, x_sample, cache_cmp_k, cache_cmp_v, cache_slc_k, cache_slc_v, state_win_k, state_win_v,
           state_pool, state_conv, page_table, c_prompt, c_sample, w_ada, b_ada, g_norm_mix, g_norm_ffn, w_in,
           w_pool, s_pool, w_pool_out, w_dw, b_dw, g_conv_ln, b_conv_ln, w_conv_out, w_cmp_k, w_cmp_v,
           w_attn_out, w_out, w_gu, w_down, g_final):
    depth, d, _ = w_in.shape
    bp, tp, _ = x_prompt.shape
    bd, ts, _ = x_sample.shape
    wp_dim = w_pool_out.shape[1]
    wc_dim = w_conv_out.shape[1]
    n_pool = cache_cmp_k.shape[1]
    n_pages = page_table.shape[1]
    past_len = n_pages * PAGE_SIZE
    wl = state_win_k.shape[2]
    n_groups, gw = w_pool.shape[1], w_pool.shape[2]
    assert gw == POOL_WINDOWS_GW and n_groups == len(POOL_WINDOWS) and wp_dim == n_groups * gw

    n_a = wp_dim + 2 * wc_dim + N_HEADS * HEAD_DIM + 6 * KV_W + 3 * N_HEADS
    n_a_pad = n_a - 3 * N_HEADS + LANES
    w_in_a = jnp.pad(w_in[:, :, :n_a], ((0, 0), (0, 0), (0, n_a_pad - n_a))).astype(BF16)
    w_gm = w_in[:, :, n_a:].astype(BF16)
    eye = jnp.eye(n_groups, dtype=F32)
    wpool_bd = (w_pool[:, :, :, None, :] * eye[None, :, None, :, None]).reshape(depth, wp_dim, wp_dim).astype(BF16)
    wpo, wco, wao, wo = (w.astype(BF16) for w in (w_pool_out, w_conv_out, w_attn_out, w_out))
    wgu, wdn = w_gu.astype(BF16), w_down.astype(BF16)
    wck = jnp.broadcast_to(w_cmp_k[:, :, None], (depth, L_CMP, KV_W))
    wcv = jnp.broadcast_to(w_cmp_v[:, :, None], (depth, L_CMP, KV_W))
    vec3 = lambda a: a.reshape(depth, 1, -1)
    g_mix, g_ffn = vec3(g_norm_mix), vec3(g_norm_ffn)
    spool, bdw, gln, bln = vec3(s_pool), vec3(b_dw), vec3(g_conv_ln), vec3(b_conv_ln)
    gf = g_final.reshape(1, d)

    tabs_p = _rope_tables(jnp.arange(tp))
    tabs_s = _rope_tables(jnp.tile(past_len + jnp.arange(ts), bd))
    eneg_p = _block_bias(tp, 1)
    n_keys_s = past_len + LANES
    eneg_s = _block_bias(n_keys_s, 2).T
    key = jnp.arange(n_keys_s)
    blk_of_key = (key[:, None] // L_CMP == jnp.arange(LANES)[None, :]).astype(F32)
    selk = (w_cmp_k[:, key % L_CMP, None] * blk_of_key[None]).astype(BF16)
    selv = (w_cmp_v[:, key % L_CMP, None] * blk_of_key[None]).astype(BF16)

    n_srows = bd * ts
    ada = _ada_all(jnp.concatenate([jnp.repeat(c_sample, ts, axis=0), c_prompt], axis=0), w_ada, b_ada)
    pos_minor = lambda a: a.transpose(0, 1, 3, 4, 2).reshape(a.shape[0] * a.shape[1], KV_W, a.shape[2])
    caches = tuple(pos_minor(c) for c in (cache_cmp_k, cache_cmp_v, cache_slc_k, cache_slc_v))
    win_k_all, win_v_all = pos_minor(state_win_k), pos_minor(state_win_v)
    new_win = [jnp.zeros(win_k_all.shape, F32), jnp.zeros(win_v_all.shape, F32)]

    xp = x_prompt
    xs = x_sample.reshape(1, bd * ts, d)
    st_p, st_s = [], []
    yp = ys = None
    for l in range(depth):
        final = l == depth - 1
        mp = ada[l, n_srows:][:, None, :]

        (up, uc, q, kc, vc, ks, vs, kw, vw, gn, ck, cv, ks_b, vs_b, kw_b, vw_b) = _inproj(
            xp, mp, g_mix, w_in_a, l, tabs_p, wck, wcv, w_pool=wp_dim, w_conv=wc_dim, with_cmp=True,
            q_dtype=BF16, tm=ROW_TILE)
        a_o, b_o, cst = _mix_prompt(up, uc, wpool_bd, spool, w_dw, bdw, gln, bln, l, tt=ROW_TILE)
        n_cmp = tp // L_CMP
        perm = lambda a: a.reshape(bp, n_cmp // 2, 2, KV_W).transpose(0, 2, 1, 3).reshape(bp, n_cmp, KV_W).astype(BF16)
        c_o = _nsa_prompt(q, gn, perm(ck), perm(cv), ks_b, vs_b, kw_b, vw_b, eneg_p)
        res = _merge_ffn(xp, mp, g_mix, g_ffn, a_o, b_o, c_o, w_gm, wpo, wco, wao, wo, wgu, wdn, gf, l,
                         tm=ROW_TILE, final=final)
        xp = res[0]
        if final:
            yp = res[1]
        kv4 = lambda a: a.reshape(a.shape[0], a.shape[1], N_KV, HEAD_DIM)
        wlp = min(WINDOW, tp)
        st_p.append((kv4(kc), kv4(vc), kv4(ks), kv4(vs), kv4(kw[:, tp - wlp:]), kv4(vw[:, tp - wlp:]),
                     up[:, tp - (max(POOL_WINDOWS) - 1):], cst[:, HALO - (CONV_K - 1):]))

        (up, uc, q, kc, vc, ks, vs, kw, vw, gn) = _inproj(
            xs, ada, g_mix, w_in_a, l, tabs_s, wck, wcv, w_pool=wp_dim, w_conv=wc_dim, with_cmp=False,
            q_dtype=F32, tm=ROW_TILE, mod_layer=l)
        tmaj = lambda a: a.reshape(bd, ts, a.shape[-1]).transpose(1, 0, 2)
        a_t, b_t, glu_t = _mix_sample(state_pool[l].transpose(1, 0, 2), tmaj(up), state_conv[l].transpose(1, 0, 2),
                                      tmaj(uc), wpool_bd, spool, w_dw, bdw, gln, bln, l)
        bmaj = lambda a: a.transpose(1, 0, 2).reshape(1, bd * ts, a.shape[-1])
        per_seq = lambda a: a.reshape(bd, ts, a.shape[-1])
        c_o, *new_win = _nsa_sample(page_table, per_seq(q), per_seq(gn),
                                    [per_seq(a) for a in (kc, vc, ks, vs, kw, vw)], win_k_all, win_v_all, caches,
                                    selk, selv, eneg_s, l, new_win, n_pool=n_pool, depth=depth)
        res = _merge_ffn(xs, ada, g_mix, g_ffn, bmaj(a_t), bmaj(b_t), c_o.reshape(1, bd * ts, -1), w_gm, wpo, wco, wao,
                         wo, wgu, wdn, gf, l, tm=ROW_TILE, final=final, mod_layer=l)
        xs = res[0]
        if final:
            ys = res[1]
        kv4s = lambda a: a.reshape(bd, -1, N_KV, HEAD_DIM)
        new_pool = jnp.concatenate([state_pool[l][:, ts:], per_seq(up)], axis=1)
        new_conv = jnp.concatenate([state_conv[l][:, ts:], glu_t.transpose(1, 0, 2)], axis=1)
        st_s.append((kv4s(kc), kv4s(vc), kv4s(ks), kv4s(vs), None, None, new_pool, new_conv))

    win_out = [w.reshape(depth, bd, N_KV, HEAD_DIM, wl).transpose(0, 1, 4, 2, 3) for w in new_win]
    outs = [yp, ys.reshape(bd, ts, d)]
    for i in range(8):
        outs.append(jnp.stack([s[i] for s in st_p]))
        outs.append(win_out[i - 4] if i in (4, 5) else jnp.stack([s[i] for s in st_s]))
    return tuple(outs)
```
